```python
import math
import jax, jax.numpy as jnp
from jax import lax
import numpy as np

D_MODEL = 1024
BATCH = 16
SEQ = 256
DEPTH = 4
DEC_BATCH = 2
DEC_SEQ = 4096
PAST_LEN = 256

GRID_W = 64
HEAD_DIM = 64
BLOCK = 128
ROPE_THETA = 10000.0
RET_HEADS = 4
RET_WIDTH = RET_HEADS * HEAD_DIM
GQA_HEADS = 4
GQA_KV_HEADS = 2
GQA_WIDTH = GQA_HEADS * HEAD_DIM
SSD_HEADS = 4
SSD_HEAD_DIM = 64
SSD_INNER = SSD_HEADS * SSD_HEAD_DIM
SSD_GROUPS = 2
SSD_STATE = 128
SSD_CONV = 3
SSD_CONV_DIM = SSD_INNER + 2 * SSD_GROUPS * SSD_STATE
SWA_HEADS = 4
SWA_KV_HEADS = 2
SWA_WIDTH = SWA_HEADS * HEAD_DIM
WINDOW = 128
MIX_WIDTH = RET_WIDTH + GQA_WIDTH + SSD_INNER + SWA_WIDTH
D_FF = -(-(8 * D_MODEL) // (3 * 256)) * 256
IN_SPLITS = (RET_WIDTH, RET_WIDTH, RET_WIDTH, RET_WIDTH,
             GQA_WIDTH, GQA_KV_HEADS * HEAD_DIM, GQA_KV_HEADS * HEAD_DIM,
             SSD_INNER, SSD_CONV_DIM, 2 * SSD_HEADS,
             SWA_WIDTH, SWA_KV_HEADS * HEAD_DIM, SWA_KV_HEADS * HEAD_DIM)
IN_WIDTH = sum(IN_SPLITS)
ALPHA = (2.0 * DEPTH) ** 0.25
BETA = (8.0 * DEPTH) ** -0.25

kernel_name = 'hybrid_diffusion_parallel_heads_step'


def layer_norm(x, g, b, eps=1e-5):
    xf = x.astype(jnp.float32)
    mu = jnp.mean(xf, -1, keepdims=True)
    var = jnp.mean(jnp.square(xf - mu), -1, keepdims=True)
    y = (xf - mu) * lax.rsqrt(var + eps) * g.astype(jnp.float32) + b.astype(jnp.float32)
    return y.astype(x.dtype)


def rms_norm(x, g, eps=1e-6):
    xf = x.astype(jnp.float32)
    y = xf * lax.rsqrt(jnp.mean(jnp.square(xf), -1, keepdims=True) + eps) * g.astype(jnp.float32)
    return y.astype(x.dtype)


def head_group_norm(o, g, eps=1e-5):
    mu = jnp.mean(o, -1, keepdims=True)
    var = jnp.mean(jnp.square(o - mu), -1, keepdims=True)
    on = (o - mu) * lax.rsqrt(var + eps)
    return on.reshape(o.shape[:2] + (-1,)) * g.astype(jnp.float32)


def split_columns(p):
    points, acc = [], 0
    for w in IN_SPLITS[:-1]:
        acc += w
        points.append(acc)
    return jnp.split(p, points, axis=-1)


def axial_rope(x):
    L = x.shape[1]
    rows = L // GRID_W
    row = jnp.repeat(jnp.arange(rows, dtype=jnp.float32), GRID_W)
    col = jnp.tile(jnp.arange(GRID_W, dtype=jnp.float32), rows)
    half = x.shape[-1] // 2
    quarter = half // 2
    inv_freq = ROPE_THETA ** (-jnp.arange(quarter, dtype=jnp.float32) / quarter)

    def rotate(xa, pos):
        ang = pos[:, None] * inv_freq[None, :]
        cos = jnp.cos(ang)[None, :, None, :]
        sin = jnp.sin(ang)[None, :, None, :]
        x1, x2 = xa[..., :quarter], xa[..., quarter:]
        return jnp.concatenate([x1 * cos - x2 * sin, x2 * cos + x1 * sin], axis=-1)

    xf = x.astype(jnp.float32)
    out = jnp.concatenate([rotate(xf[..., :half], row), rotate(xf[..., half:], col)], axis=-1)
    return out.astype(x.dtype)


def sink_softmax(s, sink):
    col = jnp.broadcast_to(sink.astype(jnp.float32)[:, :, None, None], s.shape[:-1] + (1,))
    return jax.nn.softmax(jnp.concatenate([s, col], axis=-1), axis=-1)[..., :-1]


def blocked_attention(q, k, v, sink=None):
    bsz, L, hq, hd = q.shape
    hkv = k.shape[2]
    grp = hq // hkv
    nb = L // BLOCK
    qb = jnp.moveaxis(q.reshape(bsz, nb, BLOCK, hkv, grp, hd), 1, 0)

    def one_block(qblk):
        s = jnp.einsum('bqkgd,bskd->bkgqs', qblk, k).astype(jnp.float32) * hd ** -0.5
        p = jax.nn.softmax(s, axis=-1) if sink is None else sink_softmax(s, sink.reshape(hkv, grp))
        return jnp.einsum('bkgqs,bskd->bqkgd', p.astype(v.dtype), v)

    out = lax.map(one_block, qb)
    return jnp.moveaxis(out, 0, 1).reshape(bsz, L, hq, hd)


def banded_attention(q, k, v, kc, vc, sink):
    bsz, N, hq, hd = q.shape
    hkv = k.shape[2]
    grp = hq // hkv
    nb = N // BLOCK
    qb = q.reshape(bsz, nb, BLOCK, hkv, grp, hd)

    def band(a):
        ap = jnp.pad(a, ((0, 0), (BLOCK, BLOCK), (0, 0), (0, 0))).reshape(bsz, nb + 2, BLOCK, hkv, hd)
        return jnp.concatenate([ap[:, :-2], ap[:, 1:-1], ap[:, 2:]], axis=2)

    kb, vb = band(k), band(v)
    qi = jnp.arange(nb)[:, None] * BLOCK + jnp.arange(BLOCK)[None, :]
    kj = (jnp.arange(nb)[:, None] - 1) * BLOCK + jnp.arange(3 * BLOCK)[None, :]
    mask = ((jnp.abs(qi[:, :, None] - kj[:, None, :]) <= WINDOW)
            & (kj[:, None, :] >= 0) & (kj[:, None, :] < N))
    scale = hd ** -0.5
    s_loc = jnp.einsum('bnqkgd,bnskd->bnkgqs', qb, kb).astype(jnp.float32) * scale
    s_loc = jnp.where(mask[None, :, None, None], s_loc, -jnp.inf)
    s_ctx = jnp.einsum('bnqkgd,bskd->bnkgqs', qb, kc).astype(jnp.float32) * scale
    p = sink_softmax(jnp.concatenate([s_loc, s_ctx], axis=-1), sink.reshape(hkv, grp))
    p_loc, p_ctx = p[..., :3 * BLOCK], p[..., 3 * BLOCK:]
    out = (jnp.einsum('bnkgqs,bnskd->bnqkgd', p_loc.astype(v.dtype), vb)
           + jnp.einsum('bnkgqs,bskd->bnqkgd', p_ctx.astype(vc.dtype), vc))
    return out.reshape(bsz, N, hq, hd)


def retention_scan(q, k, v, log_gamma, s0):
    bsz, L, H, _ = q.shape
    dv = v.shape[-1]
    nc = L // BLOCK
    pos = jnp.arange(BLOCK, dtype=jnp.float32)
    diff = pos[:, None] - pos[None, :]
    decay = jnp.exp(jnp.where(diff[None] >= 0, diff[None] * log_gamma[:, None, None], -jnp.inf))
    q_decay = jnp.exp((pos[:, None] + 1.0) * log_gamma[None, :])
    k_decay = jnp.exp((BLOCK - 1.0 - pos[:, None]) * log_gamma[None, :])
    chunk_decay = jnp.exp(BLOCK * log_gamma)

    def to_chunks(a):
        return jnp.moveaxis(a.reshape((bsz, nc, BLOCK) + a.shape[2:]), 1, 0)

    def step(state, blk):
        qc, kc, vc = blk
        attn = jnp.einsum('bqhd,bkhd->bhqk', qc, kc) * decay
        out = (jnp.einsum('bhqk,bkhe->bqhe', attn, vc)
               + jnp.einsum('bqhd,bhde->bqhe', qc, state) * q_decay[None, :, :, None])
        state = (state * chunk_decay[None, :, None, None]
                 + jnp.einsum('bkhd,bkhe->bhde', kc * k_decay[None, :, :, None], vc))
        return state, out

    state, out = lax.scan(step, s0, (to_chunks(q), to_chunks(k), to_chunks(v)))
    return jnp.moveaxis(out, 0, 1).reshape(bsz, L, H, dv), state


def ssd_scan(x, dt, a, bm, cm, h0):
    bsz, L, H, P = x.shape
    nc = L // BLOCK
    lower = jnp.tril(jnp.ones((BLOCK, BLOCK), dtype=bool))

    def to_chunks(t):
        return jnp.moveaxis(t.reshape((bsz, nc, BLOCK) + t.shape[2:]), 1, 0)

    def step(h, blk):
        xc, dtc, bc, cc = blk
        cum = jnp.cumsum(dtc * a, axis=1)
        seg = jnp.exp(jnp.where(lower[None, :, :, None],
                                cum[:, :, None, :] - cum[:, None, :, :], -jnp.inf))
        xdt = xc * dtc[..., None]
        scores = jnp.einsum('bqhn,bkhn->bqkh', cc, bc) * seg
        y = (jnp.einsum('bqkh,bkhp->bqhp', scores, xdt)
             + jnp.einsum('bqhn,bhpn->bqhp', cc, h) * jnp.exp(cum)[..., None])
        to_end = jnp.exp(cum[:, -1:, :] - cum)
        h = (h * jnp.exp(cum[:, -1])[:, :, None, None]
             + jnp.einsum('bkhn,bkhp->bhpn', bc * to_end[..., None], xdt))
        return h, y

    h, y = lax.scan(step, h0, (to_chunks(x), to_chunks(dt), to_chunks(bm), to_chunks(cm)))
    return jnp.moveaxis(y, 0, 1).reshape(bsz, L, H, P), h


def centred_depthwise_conv(x, w, b):
    width, ch = w.shape
    y = lax.conv_general_dilated(x, w[:, None, :], window_strides=(1,),
                                 padding=[((width - 1) // 2, width // 2)],
                                 dimension_numbers=('NWC', 'WIO', 'NWC'), feature_group_count=ch)
    return y + b


def token_mixers(h, lp, ctx):
    f32 = jnp.float32
    bsz, L, _ = h.shape
    latent = ctx is not None
    (rq, rk, rv, rg, gq, gk, gv, sz, sxbc, sdt, wq, wk, wv) = split_columns(h @ lp['w_in'])

    def heads(t, n):
        return t.reshape(bsz, L, n, -1)

    def flip(t):
        return jnp.flip(t, axis=1)

    q_r = heads(rq, RET_HEADS).astype(f32)
    k_r = heads(rk, RET_HEADS).astype(f32) * HEAD_DIM ** -0.5
    v_r = heads(rv, RET_HEADS).astype(f32)
    log_gamma = jnp.log1p(-jnp.exp(lp['ret_decay'].astype(f32)))
    r0 = ctx['ret'].astype(f32) if latent else jnp.zeros((bsz, 2, RET_HEADS, HEAD_DIM, HEAD_DIM), f32)
    o_f, s_f = retention_scan(q_r, k_r, v_r, log_gamma[0], r0[:, 0])
    o_b, s_b = retention_scan(flip(q_r), flip(k_r), flip(v_r), log_gamma[1], r0[:, 1])
    y_ret = (head_group_norm(o_f + flip(o_b), lp['ret_gn_g'])
             * jax.nn.silu(rg.astype(f32))).astype(h.dtype)

    q_g = rms_norm(heads(gq, GQA_HEADS), lp['gqa_q_norm'])
    k_g = rms_norm(heads(gk, GQA_KV_HEADS), lp['gqa_k_norm'])
    v_g = heads(gv, GQA_KV_HEADS)
    if latent:
        y_gqa = blocked_attention(axial_rope(q_g),
                                  jnp.concatenate([axial_rope(k_g), ctx['gqa_k']], axis=1),
                                  jnp.concatenate([v_g, ctx['gqa_v']], axis=1))
    else:
        y_gqa = blocked_attention(q_g, k_g, v_g)
    y_gqa = y_gqa.reshape(bsz, L, GQA_WIDTH)

    xbc = jax.nn.silu(centred_depthwise_conv(sxbc, lp['ssd_conv_w'], lp['ssd_conv_b']))
    xs_, bs_, cs_ = jnp.split(xbc, [SSD_INNER, SSD_INNER + SSD_GROUPS * SSD_STATE], axis=-1)
    rep = SSD_HEADS // SSD_GROUPS
    x_s = xs_.reshape(bsz, L, SSD_HEADS, SSD_HEAD_DIM).astype(f32)
    b_s = jnp.repeat(bs_.reshape(bsz, L, SSD_GROUPS, SSD_STATE), rep, axis=2).astype(f32)
    c_s = jnp.repeat(cs_.reshape(bsz, L, SSD_GROUPS, SSD_STATE), rep, axis=2).astype(f32)
    dt = jax.nn.softplus(sdt.reshape(bsz, L, 2, SSD_HEADS).astype(f32) + lp['ssd_dt_bias'].astype(f32))
    a = -jnp.exp(lp['ssd_a_log'].astype(f32))
    h0 = ctx['ssd'].astype(f32) if latent else jnp.zeros((bsz, 2, SSD_HEADS, SSD_HEAD_DIM, SSD_STATE), f32)
    y_f, h_f = ssd_scan(x_s, dt[:, :, 0], a[0], b_s, c_s, h0[:, 0])
    y_b, h_b = ssd_scan(flip(x_s), flip(dt[:, :, 1]), a[1], flip(b_s), flip(c_s), h0[:, 1])
    y_s = y_f + flip(y_b) + x_s * lp['ssd_d'].astype(f32)[:, None]
    y_s = y_s.reshape(bsz, L, SSD_INNER) * jax.nn.silu(sz.astype(f32))
    y_ssd = rms_norm(y_s, lp['ssd_norm_g']).astype(h.dtype)

    q_w = heads(wq, SWA_HEADS)
    k_w = heads(wk, SWA_KV_HEADS)
    v_w = heads(wv, SWA_KV_HEADS)
    if latent:
        y_swa = banded_attention(axial_rope(q_w), axial_rope(k_w), v_w,
                                 ctx['swa_k'], ctx['swa_v'], lp['swa_sink'])
    else:
        y_swa = blocked_attention(q_w, k_w, v_w, lp['swa_sink'])
    y_swa = y_swa.reshape(bsz, L, SWA_WIDTH)

    out = jnp.concatenate([y_ret, y_gqa, y_ssd, y_swa], axis=-1) @ lp['w_out']
    if latent:
        return out, None
    ret_state = jnp.stack([s_f, s_b], axis=1).astype(h.dtype)
    ssd_state = jnp.stack([h_f, h_b], axis=1).astype(h.dtype)
    return out, (k_g, v_g, k_w, v_w, ret_state, ssd_state)


def swiglu(h, w_in, w_out):
    gate, up = jnp.split(h @ w_in, 2, axis=-1)
    return (jax.nn.silu(gate) * up) @ w_out


def trunk_layer(x, cond, lp, ctx):
    mod = (jax.nn.silu(cond) @ lp['ada_w'] + lp['ada_b'])[:, None, :]
    shift1, scale1, gate1, shift2, scale2, gate2 = jnp.split(mod, 6, axis=-1)
    mix, ctx_out = token_mixers(x * (1 + scale1) + shift1, lp, ctx)
    x = layer_norm(ALPHA * x + gate1 * mix, lp['ln1_g'], lp['ln1_b'])
    ffn = swiglu(x * (1 + scale2) + shift2, lp['w_ffn_in'], lp['w_ffn_out'])
    x = layer_norm(ALPHA * x + gate2 * ffn, lp['ln2_g'], lp['ln2_b'])
    return x, ctx_out


def setup_inputs(seed: int = 0) -> dict:
    key = jax.random.key(seed)
    keys = iter(jax.random.split(key, 48))

    def normal(shape, scale=1.0):
        return scale * jax.random.normal(next(keys), shape, jnp.float32)

    def gain(shape):
        return 1.0 + normal(shape, 0.02)

    gqa_cache = (DEC_BATCH, DEPTH, PAST_LEN, GQA_KV_HEADS, HEAD_DIM)
    swa_cache = (DEC_BATCH, DEPTH, PAST_LEN, SWA_KV_HEADS, HEAD_DIM)
    head_idx = jnp.arange(RET_HEADS, dtype=jnp.float32)
    dir_offset = jnp.array([0.0, 0.5], jnp.float32)
    ret_decay = (-(5.0 + head_idx[None, None, :] + dir_offset[None, :, None]) * math.log(2.0)
                 + normal((DEPTH, 2, RET_HEADS), 0.01))
    dt0 = jnp.exp(jax.random.uniform(next(keys), (DEPTH, 2, SSD_HEADS), jnp.float32,
                                     math.log(1e-3), math.log(1e-1)))
    ssd_dt_bias = dt0 + jnp.log(-jnp.expm1(-dt0))
    ssd_a_log = jnp.log(jax.random.uniform(next(keys), (DEPTH, 2, SSD_HEADS), jnp.float32, 1.0, 16.0))
    return {
        'x_prompt': normal((BATCH, SEQ, D_MODEL)),
        'x_sample': normal((DEC_BATCH, DEC_SEQ, D_MODEL)),
        'cache_gqa_k': normal(gqa_cache),
        'cache_gqa_v': normal(gqa_cache),
        'cache_swa_k': normal(swa_cache),
        'cache_swa_v': normal(swa_cache),
        'state_ret': normal((DEC_BATCH, DEPTH, 2, RET_HEADS, HEAD_DIM, HEAD_DIM), 0.5),
        'state_ssd': normal((DEC_BATCH, DEPTH, 2, SSD_HEADS, SSD_HEAD_DIM, SSD_STATE), 0.5),
        'c': normal((DEC_BATCH, D_MODEL)),
        'c_ctx': normal((D_MODEL,)),
        'ada_w': normal((DEPTH, D_MODEL, 6 * D_MODEL), 0.5 * D_MODEL ** -0.5),
        'ada_b': normal((DEPTH, 6 * D_MODEL), 0.02),
        'w_in': normal((DEPTH, D_MODEL, IN_WIDTH), D_MODEL ** -0.5),
        'ret_decay': ret_decay,
        'ret_gn_g': gain((DEPTH, RET_WIDTH)),
        'gqa_q_norm': gain((DEPTH, HEAD_DIM)),
        'gqa_k_norm': gain((DEPTH, HEAD_DIM)),
        'ssd_conv_w': normal((DEPTH, SSD_CONV, SSD_CONV_DIM), SSD_CONV ** -0.5),
        'ssd_conv_b': normal((DEPTH, SSD_CONV_DIM), 0.02),
        'ssd_dt_bias': ssd_dt_bias,
        'ssd_a_log': ssd_a_log,
        'ssd_d': gain((DEPTH, SSD_HEADS)),
        'ssd_norm_g': gain((DEPTH, SSD_INNER)),
        'swa_sink': normal((DEPTH, SWA_HEADS), 0.5),
        'w_out': normal((DEPTH, MIX_WIDTH, D_MODEL), BETA * MIX_WIDTH ** -0.5),
        'ln1_g': gain((DEPTH, D_MODEL)),
        'ln1_b': normal((DEPTH, D_MODEL), 0.02),
        'w_ffn_in': normal((DEPTH, D_MODEL, 2 * D_FF), D_MODEL ** -0.5),
        'w_ffn_out': normal((DEPTH, D_FF, D_MODEL), BETA * D_FF ** -0.5),
        'ln2_g': gain((DEPTH, D_MODEL)),
        'ln2_b': normal((DEPTH, D_MODEL), 0.02),
    }


def reference(x_prompt, x_sample, cache_gqa_k, cache_gqa_v, cache_swa_k, cache_swa_v, state_ret, state_ssd,
              c, c_ctx, ada_w, ada_b, w_in, ret_decay, ret_gn_g, gqa_q_norm, gqa_k_norm,
              ssd_conv_w, ssd_conv_b, ssd_dt_bias, ssd_a_log, ssd_d, ssd_norm_g, swa_sink, w_out,
              ln1_g, ln1_b, w_ffn_in, w_ffn_out, ln2_g, ln2_b):
    def layer_params(l):
        return {'ada_w': ada_w[l], 'ada_b': ada_b[l], 'w_in': w_in[l], 'ret_decay': ret_decay[l],
                'ret_gn_g': ret_gn_g[l], 'gqa_q_norm': gqa_q_norm[l], 'gqa_k_norm': gqa_k_norm[l],
                'ssd_conv_w': ssd_conv_w[l], 'ssd_conv_b': ssd_conv_b[l], 'ssd_dt_bias': ssd_dt_bias[l],
                'ssd_a_log': ssd_a_log[l], 'ssd_d': ssd_d[l], 'ssd_norm_g': ssd_norm_g[l],
                'swa_sink': swa_sink[l], 'w_out': w_out[l], 'ln1_g': ln1_g[l], 'ln1_b': ln1_b[l],
                'w_ffn_in': w_ffn_in[l], 'w_ffn_out': w_ffn_out[l], 'ln2_g': ln2_g[l], 'ln2_b': ln2_b[l]}

    xp = x_prompt
    cond_ctx = c_ctx[None, :]
    gk_l, gv_l, wk_l, wv_l, rs_l, ss_l = [], [], [], [], [], []
    for l in range(DEPTH):
        xp, (k_g, v_g, k_w, v_w, r_s, s_s) = trunk_layer(xp, cond_ctx, layer_params(l), None)
        gk_l.append(k_g)
        gv_l.append(v_g)
        wk_l.append(k_w)
        wv_l.append(v_w)
        rs_l.append(r_s)
        ss_l.append(s_s)
    new_gqa_k = jnp.stack(gk_l, axis=1)
    new_gqa_v = jnp.stack(gv_l, axis=1)
    new_swa_k = jnp.stack(wk_l, axis=1)
    new_swa_v = jnp.stack(wv_l, axis=1)
    new_state_ret = jnp.stack(rs_l, axis=1)
    new_state_ssd = jnp.stack(ss_l, axis=1)

    xs = x_sample
    for l in range(DEPTH):
        ctx = {'gqa_k': cache_gqa_k[:, l], 'gqa_v': cache_gqa_v[:, l],
               'swa_k': cache_swa_k[:, l], 'swa_v': cache_swa_v[:, l],
               'ret': state_ret[:, l], 'ssd': state_ssd[:, l]}
        xs, _ = trunk_layer(xs, c, layer_params(l), ctx)

    return (xp, xs, new_gqa_k, new_gqa_v, new_swa_k, new_swa_v, new_state_ret, new_state_ssd)
```

```python
import functools

import jax
import jax.numpy as jnp
from jax import lax
from jax.experimental import pallas as pl
from jax.experimental.pallas import tpu as pltpu

F32 = jnp.float32
BF16 = jnp.bfloat16

D_MODEL = 1024
DEPTH = 4
HEAD_DIM = 64
N_HEADS = 4
GRID_W = 64
ROPE_THETA = 10000.0
SSD_STATE = 128
D_FF = 2816
WINDOW = 128
PAST_LEN = 256
ALPHA = (2.0 * DEPTH) ** 0.25
CHUNK = 256
TOKEN_TILE = 256
VMEM_LIMIT = 56 * 1024 * 1024

C_RQ, C_RK, C_RV, C_RG = 0, 256, 512, 768
C_GQ, C_GK, C_GV = 1024, 1280, 1408
C_XBC, C_SZ = 1536, 2304
C_WQ, C_WK, C_WV = 2560, 2816, 2944
C_DT = 3072
PROJ_W = 3200


def _silu(x):
    return x / (1.0 + jnp.exp(-x))


def _dot(a, b):
    return jnp.dot(a, b, preferred_element_type=F32)


def _dot_nt(a, b):
    return lax.dot_general(a, b, (((1,), (1,)), ((), ())), preferred_element_type=F32)


def _dot_tn(a, b):
    return lax.dot_general(a, b, (((0,), (0,)), ((), ())), preferred_element_type=F32)


def _seg_sum(x, seg):
    hi = x.astype(BF16)
    lo = (x - hi.astype(F32)).astype(BF16)
    return _dot(hi, seg) + _dot(lo, seg)


def _params(sem):
    return pltpu.CompilerParams(dimension_semantics=sem, vmem_limit_bytes=VMEM_LIMIT)


def _ada_kernel(c_ref, w_ref, b_ref, o_ref):
    s = _silu(c_ref[...])
    o_ref[...] = _dot(s.astype(BF16), w_ref[...].astype(BF16)) + b_ref[...]


def _ada_call(cond8, ada_w, ada_b):
    tn = D_MODEL
    nt = 6
    out = pl.pallas_call(
        _ada_kernel,
        out_shape=jax.ShapeDtypeStruct((DEPTH, nt, 8, tn), F32),
        grid=(DEPTH, nt),
        in_specs=[pl.BlockSpec((8, D_MODEL), lambda l, j: (0, 0)),
                  pl.BlockSpec((None, D_MODEL, tn), lambda l, j: (l, 0, j)),
                  pl.BlockSpec((None, None, 1, tn), lambda l, j: (l, j, 0, 0))],
        out_specs=pl.BlockSpec((None, None, 8, tn), lambda l, j: (l, j, 0, 0)),
        compiler_params=_params(("arbitrary", "arbitrary")),
        name="ada_mod",
    )(cond8, ada_w, ada_b.reshape(DEPTH, nt, 1, tn))
    return jnp.transpose(out, (0, 2, 1, 3)).reshape(DEPTH, 8, nt, 1, tn)


def _mod_spec(layer, row0, tiles_per_row, k):
    return pl.BlockSpec((None, None, None, 1, D_MODEL),
                        lambda i: (layer, row0 + i // tiles_per_row, k, 0, 0))


def _inproj_kernel(x_ref, sh_ref, sc_ref, w_ref, o_ref):
    h = x_ref[...] * (1.0 + sc_ref[...]) + sh_ref[...]
    o_ref[...] = _dot(h.astype(BF16), w_ref[...])


def _inproj_call(x, mod5, w_in_p, layer, row0, tiles_per_row):
    m = x.shape[0]
    tm = TOKEN_TILE
    return pl.pallas_call(
        _inproj_kernel,
        out_shape=jax.ShapeDtypeStruct((m, PROJ_W), F32),
        grid=(m // tm,),
        in_specs=[pl.BlockSpec((tm, D_MODEL), lambda i: (i, 0)),
                  _mod_spec(layer, row0, tiles_per_row, 0),
                  _mod_spec(layer, row0, tiles_per_row, 1),
                  pl.BlockSpec((None, D_MODEL, PROJ_W), lambda i: (layer, 0, 0))],
        out_specs=pl.BlockSpec((tm, PROJ_W), lambda i: (i, 0)),
        compiler_params=_params(("arbitrary",)),
        name="in_proj",
    )(x, mod5, mod5, w_in_p)


def _rope(x, cos, sin_signed):
    w = x.shape[-1]
    lane = lax.broadcasted_iota(jnp.int32, x.shape, 1)
    partner = jnp.where((lane % 32) < 16, pltpu.roll(x, w - 16, 1), pltpu.roll(x, 16, 1))
    return x * cos + partner * sin_signed


def _expand_q(q):
    lane = lax.broadcasted_iota(jnp.int32, (q.shape[0], 128), 1)
    low = lane < HEAD_DIM
    blocks = []
    for j in range(2):
        pair = q[:, 128 * j:128 * (j + 1)]
        blocks.append(jnp.where(low, pair, 0.0))
        blocks.append(jnp.where(low, 0.0, pair))
    return jnp.concatenate(blocks, axis=1).astype(BF16)


def _dup_kv(k):
    lane = lax.broadcasted_iota(jnp.int32, k.shape, 1)
    low = lane < HEAD_DIM
    sw = pltpu.roll(k, HEAD_DIM, 1)
    return jnp.concatenate([jnp.where(low, k, sw), jnp.where(low, sw, k)], axis=1).astype(BF16)


def _prep_kernel(*refs, rope):
    gq, gk, gv, wq, wk, wv, qn, kn, seg = refs[:9]
    i = 9
    if rope:
        cos_ref, sin_ref = refs[9:11]
        i = 11
    qg_o, kg_o, vg_o, qw_o, kw_o, vw_o = refs[i:i + 6]
    seg_m = seg[...]

    def rms(x, g, s):
        ms = _seg_sum(x * x, s) * (1.0 / HEAD_DIM)
        return x * lax.rsqrt(ms + 1e-6) * g

    q_g = rms(gq[...], qn[...], seg_m)
    k_g = rms(gk[...], kn[...][:, :128], seg_m[:128, :128])
    q_w = wq[...]
    k_w = wk[...]
    if rope:
        cos = cos_ref[...]
        sin = sin_ref[...]
        q_g = _rope(q_g, cos, sin)
        q_w = _rope(q_w, cos, sin)
        k_g = _rope(k_g, cos[:, :128], sin[:, :128])
        k_w = _rope(k_w, cos[:, :128], sin[:, :128])
    else:
        refs[i + 6][...] = k_g
    scale = HEAD_DIM ** -0.5
    qg_o[...] = _expand_q(q_g * scale)
    qw_o[...] = _expand_q(q_w * scale)
    kg_o[...] = _dup_kv(k_g)
    kw_o[...] = _dup_kv(k_w)
    vg_o[...] = _dup_kv(gv[...])
    vw_o[...] = _dup_kv(wv[...])


def _prep_call(p, qn256, kn256, seg, rope_tabs, seq_len):
    m = p.shape[0]
    tm = TOKEN_TILE
    rope = rope_tabs is not None

    def col(width, start):
        return pl.BlockSpec((tm, width), lambda i: (i, start // width))

    const = lambda shape: pl.BlockSpec(shape, lambda i: (0, 0))
    in_specs = [col(256, C_GQ), col(128, C_GK), col(128, C_GV),
                col(256, C_WQ), col(128, C_WK), col(128, C_WV),
                const((1, 256)), const((1, 256)), const((256, 256))]
    args = [p, p, p, p, p, p, qn256, kn256, seg]
    if rope:
        tpb = seq_len // tm
        tab = pl.BlockSpec((tm, 256), lambda i: (i % tpb, 0))
        in_specs += [tab, tab]
        args += list(rope_tabs)
    outs = [jax.ShapeDtypeStruct((m, 512), BF16), jax.ShapeDtypeStruct((m, 256), BF16),
            jax.ShapeDtypeStruct((m, 256), BF16), jax.ShapeDtypeStruct((m, 512), BF16),
            jax.ShapeDtypeStruct((m, 256), BF16), jax.ShapeDtypeStruct((m, 256), BF16)]
    row = lambda w: pl.BlockSpec((tm, w), lambda i: (i, 0))
    out_specs = [row(512), row(256), row(256), row(512), row(256), row(256)]
    if not rope:
        outs.append(jax.ShapeDtypeStruct((m, 128), F32))
        out_specs.append(row(128))
    return pl.pallas_call(
        functools.partial(_prep_kernel, rope=rope),
        out_shape=outs, grid=(m // tm,), in_specs=in_specs, out_specs=out_specs,
        compiler_params=_params(("arbitrary",)), name="attn_prep",
    )(*args)


def _online(m, l, acc, s, v):
    m_new = jnp.maximum(m, jnp.max(s, axis=-1, keepdims=True))
    p = jnp.exp(s - m_new)
    a = jnp.exp(m - m_new)
    l = a * l + jnp.sum(p, axis=-1, keepdims=True)
    acc = a * acc + _dot(p.astype(BF16), v)
    return m_new, l, acc


def _attn_kernel(*refs, seq_len, tq, kv_chunk, window, use_cache, use_sink):
    q_ref, k_ref, v_ref = refs[:3]
    i = 3
    if use_cache:
        kc_ref, vc_ref = refs[3:5]
        i = 5
    if use_sink:
        sink_ref = refs[i]
        i += 1
    o_ref = refs[i]
    qi = pl.program_id(1)
    low = lax.broadcasted_iota(jnp.int32, (tq, 128), 1) < HEAD_DIM
    span = tq + 2 * WINDOW
    for j in range(2):
        lanes = slice(128 * j, 128 * (j + 1))
        heads = []
        for r in range(2):
            h = 2 * j + r
            q = q_ref[:, 128 * h:128 * (h + 1)]
            carry = (jnp.full((tq, 1), -jnp.inf, F32), jnp.zeros((tq, 1), F32),
                     jnp.zeros((tq, 128), F32))
            if window:
                start = jnp.clip(qi * tq - WINDOW, 0, seq_len - span)
                start = pl.multiple_of(start, 128)
                s = _dot_nt(q, k_ref[pl.ds(start, span), lanes])
                qpos = qi * tq + lax.broadcasted_iota(jnp.int32, (tq, span), 0)
                kpos = start + lax.broadcasted_iota(jnp.int32, (tq, span), 1)
                s = jnp.where(jnp.abs(qpos - kpos) <= WINDOW, s, -jnp.inf)
                carry = _online(*carry, s, v_ref[pl.ds(start, span), lanes])
            else:
                def body(c, cr, q=q, lanes=lanes):
                    st = pl.multiple_of(c * kv_chunk, kv_chunk)
                    s = _dot_nt(q, k_ref[pl.ds(st, kv_chunk), lanes])
                    return _online(*cr, s, v_ref[pl.ds(st, kv_chunk), lanes])
                carry = lax.fori_loop(0, seq_len // kv_chunk, body, carry)
            if use_cache:
                s = _dot_nt(q, kc_ref[:, lanes])
                carry = _online(*carry, s, vc_ref[:, lanes])
            m, l, acc = carry
            if use_sink:
                sk = sink_ref[h][:, 0:1]
                m2 = jnp.maximum(m, sk)
                a = jnp.exp(m - m2)
                l = a * l + jnp.exp(sk - m2)
                acc = a * acc
            heads.append(acc / l)
        o_ref[:, lanes] = jnp.where(low, heads[0], heads[1]).astype(o_ref.dtype)


def _attn_call(q, k, v, nb, seq_len, cache=None, sink=None, window=False):
    tq = TOKEN_TILE
    nq = seq_len // tq
    kv_chunk = min(512, seq_len)
    in_specs = [pl.BlockSpec((tq, 512), lambda b, i: (b * nq + i, 0)),
                pl.BlockSpec((seq_len, 256), lambda b, i: (b, 0)),
                pl.BlockSpec((seq_len, 256), lambda b, i: (b, 0))]
    args = [q, k, v]
    if cache is not None:
        cspec = pl.BlockSpec((None, PAST_LEN, 256), lambda b, i: (b, 0, 0))
        in_specs += [cspec, cspec]
        args += list(cache)
    if sink is not None:
        in_specs.append(pl.BlockSpec((N_HEADS, 1, 128), lambda b, i: (0, 0, 0)))
        args.append(sink)
    return pl.pallas_call(
        functools.partial(_attn_kernel, seq_len=seq_len, tq=tq, kv_chunk=kv_chunk, window=window,
                          use_cache=cache is not None, use_sink=sink is not None),
        out_shape=jax.ShapeDtypeStruct((nb * seq_len, 256), BF16),
        grid=(nb, nq), in_specs=in_specs,
        out_specs=pl.BlockSpec((tq, 256), lambda b, i: (b * nq + i, 0)),
        compiler_params=_params(("arbitrary", "arbitrary")), name="attention",
    )(*args)


def _chunk_of(s, nc):
    return jnp.where(s < nc, nc - 1 - s, s - nc)


def _log_gamma(rd):
    return jnp.log1p(-jnp.exp(rd))


def _ret_kernel(*refs, nc, has_init, emit_state):
    q_ref, k_ref, v_ref, g_ref, rdl_ref, rdh_ref, gn_ref, seg_ref = refs[:8]
    i = 8
    if has_init:
        s0_ref = refs[i]
        i += 1
    y_ref = refs[i]
    i += 1
    if emit_state:
        st_ref = refs[i]
        i += 1
    m_sc, dec_sc, qk_sc, sf_sc, sb_sc, sball_sc = refs[i:i + 6]
    c_len = CHUNK
    b = pl.program_id(0)
    s = pl.program_id(1)

    @pl.when(jnp.logical_and(b == 0, s == 0))
    def _():
        ri = lax.broadcasted_iota(jnp.int32, (c_len, c_len), 0)
        ci = lax.broadcasted_iota(jnp.int32, (c_len, c_len), 1)
        d = (ri - ci).astype(F32)
        for h in range(N_HEADS):
            lgf = _log_gamma(jnp.broadcast_to(rdh_ref[0, h], (c_len, c_len)))
            lgb = _log_gamma(jnp.broadcast_to(rdh_ref[1, h], (c_len, c_len)))
            m_sc[h] = jnp.where(d > 0, jnp.exp(d * lgf), jnp.where(d < 0, jnp.exp(-d * lgb), 2.0))
        for dr in range(2):
            rows = [jnp.exp(c_len * _log_gamma(jnp.broadcast_to(rdh_ref[dr, h], (HEAD_DIM, c_len))))
                    for h in range(N_HEADS)]
            dec_sc[dr] = jnp.concatenate(rows, axis=0)
        a = lax.broadcasted_iota(jnp.int32, (c_len, 256), 0).astype(F32)
        lgf = _log_gamma(rdl_ref[0])
        lgb = _log_gamma(rdl_ref[1])
        qk_sc[0] = jnp.exp((a + 1.0) * lgf)
        qk_sc[1] = jnp.exp((c_len - 1.0 - a) * lgf)
        qk_sc[2] = jnp.exp((c_len - a) * lgb)
        qk_sc[3] = jnp.exp(a * lgb)

    @pl.when(s == 0)
    def _():
        sf_sc[...] = jnp.zeros_like(sf_sc)
        sb_sc[...] = jnp.zeros_like(sb_sc)
        if has_init:
            for h in range(N_HEADS):
                blk = slice(HEAD_DIM * h, HEAD_DIM * (h + 1))
                sf_sc[blk, blk] = s0_ref[0, h]
                sb_sc[blk, blk] = s0_ref[1, h]

    ri = lax.broadcasted_iota(jnp.int32, (256, 256), 0) // HEAD_DIM
    ci = lax.broadcasted_iota(jnp.int32, (256, 256), 1) // HEAD_DIM
    diag = ri == ci
    ks = k_ref[...] * (HEAD_DIM ** -0.5)
    vb = v_ref[...].astype(BF16)

    @pl.when(s < nc)
    def _():
        c = nc - 1 - s
        sball_sc[c] = sb_sc[...]
        upd = _dot_tn((ks * qk_sc[3]).astype(BF16), vb)
        sb_sc[...] = sb_sc[...] * dec_sc[1] + jnp.where(diag, upd, 0.0)
        if emit_state:
            @pl.when(s == nc - 1)
            def _():
                for h in range(N_HEADS):
                    blk = slice(HEAD_DIM * h, HEAD_DIM * (h + 1))
                    st_ref[1, h] = sb_sc[blk, blk]

    @pl.when(s >= nc)
    def _():
        c = s - nc
        q = q_ref[...]
        qb = q.astype(BF16)
        kb = ks.astype(BF16)
        lane_head = lax.broadcasted_iota(jnp.int32, (c_len, 256), 1) // HEAD_DIM
        o = (_dot(qb, sf_sc[...].astype(BF16)) * qk_sc[0]
             + _dot(qb, sball_sc[c].astype(BF16)) * qk_sc[2])
        for h in range(N_HEADS):
            mine = lane_head == h
            sc = _dot_nt(jnp.where(mine, q, 0.0).astype(BF16), kb)
            pv = _dot((sc * m_sc[h]).astype(BF16), vb)
            o = o + jnp.where(mine, pv, 0.0)
        seg = seg_ref[...]
        mu = _seg_sum(o, seg) * (1.0 / HEAD_DIM)
        dlt = o - mu
        var = _seg_sum(dlt * dlt, seg) * (1.0 / HEAD_DIM)
        on = dlt * lax.rsqrt(var + 1e-5) * gn_ref[...]
        y_ref[...] = (on * _silu(g_ref[...])).astype(y_ref.dtype)
        upd = _dot_tn((ks * qk_sc[1]).astype(BF16), vb)
        sf_sc[...] = sf_sc[...] * dec_sc[0] + jnp.where(diag, upd, 0.0)
        if emit_state:
            @pl.when(s == 2 * nc - 1)
            def _():
                for h in range(N_HEADS):
                    blk = slice(HEAD_DIM * h, HEAD_DIM * (h + 1))
                    st_ref[0, h] = sf_sc[blk, blk]


def _ret_call(p, nb, seq_len, rd_lane, rd_head, gn_g, seg, layer, state0=None, emit_state=False):
    c_len = CHUNK
    nc = seq_len // c_len

    def col(start):
        return pl.BlockSpec((c_len, 256), lambda b, s: (b * nc + _chunk_of(s, nc), start // 256))

    in_specs = [col(C_RQ), col(C_RK), col(C_RV), col(C_RG),
                pl.BlockSpec((2, 1, 256), lambda b, s: (0, 0, 0)),
                pl.BlockSpec((2, N_HEADS, 1, c_len), lambda b, s: (0, 0, 0, 0)),
                pl.BlockSpec((1, 256), lambda b, s: (0, 0)),
                pl.BlockSpec((256, 256), lambda b, s: (0, 0))]
    args = [p, p, p, p, rd_lane, rd_head, gn_g, seg]
    if state0 is not None:
        in_specs.append(pl.BlockSpec((None, None, 2, N_HEADS, HEAD_DIM, HEAD_DIM),
                                     lambda b, s: (b, layer, 0, 0, 0, 0)))
        args.append(state0)
    outs = [jax.ShapeDtypeStruct((nb * seq_len, 256), BF16)]
    out_specs = [pl.BlockSpec((c_len, 256), lambda b, s: (b * nc + jnp.maximum(s - nc, 0), 0))]
    if emit_state:
        outs.append(jax.ShapeDtypeStruct((nb, 2, N_HEADS, HEAD_DIM, HEAD_DIM), F32))
        out_specs.append(pl.BlockSpec((None, 2, N_HEADS, HEAD_DIM, HEAD_DIM),
                                      lambda b, s: (b, 0, 0, 0, 0)))
    scratch = [pltpu.VMEM((N_HEADS, c_len, c_len), F32), pltpu.VMEM((2, 256, 256), F32),
               pltpu.VMEM((4, c_len, 256), F32), pltpu.VMEM((256, 256), F32),
               pltpu.VMEM((256, 256), F32), pltpu.VMEM((nc, 256, 256), F32)]
    return pl.pallas_call(
        functools.partial(_ret_kernel, nc=nc, has_init=state0 is not None, emit_state=emit_state),
        out_shape=outs, grid=(nb, 2 * nc), in_specs=in_specs, out_specs=out_specs,
        scratch_shapes=scratch, compiler_params=_params(("arbitrary", "arbitrary")),
        name="retention",
    )(*args)


def _cumsum_rows(x, reverse):
    n = x.shape[0]
    row = lax.broadcasted_iota(jnp.int32, x.shape, 0)
    sh = 1
    while sh < n:
        if reverse:
            x = x + jnp.where(row < n - sh, pltpu.roll(x, n - sh, 0), 0.0)
        else:
            x = x + jnp.where(row >= sh, pltpu.roll(x, sh, 0), 0.0)
        sh *= 2
    return x


def _ssd_kernel(*refs, nc, has_init, emit_state):
    xbc_ref, prev_ref, next_ref, z_ref, dt_ref, cw_ref, cb_ref, dtb_ref, al_ref, dl_ref, ng_ref = refs[:11]
    i = 11
    if has_init:
        h0_ref = refs[i]
        i += 1
    y_ref = refs[i]
    i += 1
    if emit_state:
        st_ref = refs[i]
        i += 1
    hf_sc, hb_sc, hball_sc = refs[i:i + 3]
    c_len = CHUNK
    s = pl.program_id(1)
    c = _chunk_of(s, nc)

    @pl.when(s == 0)
    def _():
        if has_init:
            hf_sc[...] = h0_ref[0]
            hb_sc[...] = h0_ref[1]
        else:
            hf_sc[...] = jnp.zeros_like(hf_sc)
            hb_sc[...] = jnp.zeros_like(hb_sc)

    xbc = xbc_ref[...]
    row = lax.broadcasted_iota(jnp.int32, xbc.shape, 0)
    before = jnp.where(c > 0, prev_ref[7:8, :], 0.0)
    after = jnp.where(c < nc - 1, next_ref[0:1, :], 0.0)
    x_prev = jnp.where(row == 0, before, pltpu.roll(xbc, 1, 0))
    x_next = jnp.where(row == c_len - 1, after, pltpu.roll(xbc, c_len - 1, 0))
    conv = _silu(cw_ref[0:1, :] * x_prev + cw_ref[1:2, :] * xbc + cw_ref[2:3, :] * x_next + cb_ref[...])
    xs = conv[:, 0:256]
    bm = conv[:, 256:512].astype(BF16)
    cm = conv[:, 512:768].astype(BF16)

    pre = dt_ref[...] + dtb_ref[...]
    dt = jnp.maximum(pre, 0.0) + jnp.log1p(jnp.exp(-jnp.abs(pre)))
    dta = dt * (-jnp.exp(al_ref[...]))
    cum = _cumsum_rows(dta, reverse=False)
    rev = _cumsum_rows(dta, reverse=True)
    low = lax.broadcasted_iota(jnp.int32, (c_len, 128), 1) < HEAD_DIM
    low_rows = lax.broadcasted_iota(jnp.int32, (128, 128), 0) < HEAD_DIM

    def col(a, lane):
        return jnp.broadcast_to(a[:, lane:lane + 1], (c_len, c_len))

    def pair(a0, a1):
        return jnp.where(low, a0[:, :128], a1[:, :128])

    def state_update(h_sc, g, cols, tot_row, dt_lanes):
        w = [jnp.exp(tot_row[r] - cols[r]) * col(dt, dt_lanes[r]) for r in range(2)]
        xw = (xs[:, 128 * g:128 * (g + 1)] * pair(w[0], w[1])).astype(BF16)
        dec = jnp.where(low_rows, jnp.broadcast_to(jnp.exp(tot_row[0][:, :128]), (128, 128)),
                        jnp.broadcast_to(jnp.exp(tot_row[1][:, :128]), (128, 128)))
        h_sc[g] = h_sc[g] * dec + _dot_tn(xw, bm[:, 128 * g:128 * (g + 1)])

    @pl.when(s < nc)
    def _():
        hball_sc[c] = hb_sc[...]
        for g in range(2):
            cols = [col(rev, 4 + 2 * g + r) for r in range(2)]
            tot = [cl[0:1, :] for cl in cols]
            state_update(hb_sc, g, cols, tot, [4 + 2 * g, 5 + 2 * g])
        if emit_state:
            @pl.when(s == nc - 1)
            def _():
                st_ref[1] = hb_sc[...]

    @pl.when(s >= nc)
    def _():
        cum_t = cum.T
        rev_t = rev.T
        dt_t = dt.T
        ri = lax.broadcasted_iota(jnp.int32, (c_len, c_len), 0)
        ci = lax.broadcasted_iota(jnp.int32, (c_len, c_len), 1)
        lane_head = lax.broadcasted_iota(jnp.int32, (c_len, 256), 1) // HEAD_DIM
        xs_b = xs.astype(BF16)
        y = jnp.zeros((c_len, 256), F32)
        inter = []
        for g in range(2):
            gl = slice(128 * g, 128 * (g + 1))
            gmat = _dot_nt(cm[:, gl], bm[:, gl])
            cols_f, cols_b = [], []
            for r in range(2):
                h = 2 * g + r
                cf = col(cum, h)
                cb = col(rev, 4 + h)
                cols_f.append(cf)
                cols_b.append(cb)
                wf = jnp.where(ri >= ci, jnp.exp(jnp.minimum(cf - cum_t[h:h + 1, :], 0.0)) * dt_t[h:h + 1, :], 0.0)
                wb = jnp.where(ci >= ri, jnp.exp(jnp.minimum(cb - rev_t[4 + h:5 + h, :], 0.0))
                               * dt_t[4 + h:5 + h, :], 0.0)
                yh = _dot((gmat * (wf + wb)).astype(BF16), xs_b)
                y = y + jnp.where(lane_head == h, yh, 0.0)
            yf = _dot_nt(cm[:, gl], hf_sc[g].astype(BF16))
            yb = _dot_nt(cm[:, gl], hball_sc[c, g].astype(BF16))
            inter.append(yf * pair(jnp.exp(cols_f[0]), jnp.exp(cols_f[1]))
                         + yb * pair(jnp.exp(cols_b[0]), jnp.exp(cols_b[1])))
            tot = [cl[c_len - 1:c_len, :] for cl in cols_f]
            state_update(hf_sc, g, cols_f, tot, [2 * g, 2 * g + 1])
        y = y + jnp.concatenate(inter, axis=1) + xs * dl_ref[...]
        y = y * _silu(z_ref[...])
        ms = jnp.mean(y * y, axis=-1, keepdims=True)
        y_ref[...] = (y * lax.rsqrt(ms + 1e-6) * ng_ref[...]).astype(y_ref.dtype)
        if emit_state:
            @pl.when(s == 2 * nc - 1)
            def _():
                st_ref[0] = hf_sc[...]


def _ssd_call(p, nb, seq_len, conv_w, conv_b, dt_bias, a_log, d_lane, norm_g, layer,
              state0=None, emit_state=False):
    c_len = CHUNK
    nc = seq_len // c_len
    m = nb * seq_len
    r8 = c_len // 8
    chunk = lambda b, s: b * nc + _chunk_of(s, nc)
    in_specs = [pl.BlockSpec((c_len, 768), lambda b, s: (chunk(b, s), C_XBC // 768)),
                pl.BlockSpec((8, 768), lambda b, s: (jnp.maximum(chunk(b, s) * r8 - 1, 0), C_XBC // 768)),
                pl.BlockSpec((8, 768), lambda b, s: (jnp.minimum((chunk(b, s) + 1) * r8, m // 8 - 1),
                                                     C_XBC // 768)),
                pl.BlockSpec((c_len, 256), lambda b, s: (chunk(b, s), C_SZ // 256)),
                pl.BlockSpec((c_len, 128), lambda b, s: (chunk(b, s), C_DT // 128)),
                pl.BlockSpec((None, 3, 768), lambda b, s: (layer, 0, 0)),
                pl.BlockSpec((1, 768), lambda b, s: (0, 0)),
                pl.BlockSpec((1, 128), lambda b, s: (0, 0)),
                pl.BlockSpec((1, 128), lambda b, s: (0, 0)),
                pl.BlockSpec((1, 256), lambda b, s: (0, 0)),
                pl.BlockSpec((1, 256), lambda b, s: (0, 0))]
    args = [p, p, p, p, p, conv_w, conv_b, dt_bias, a_log, d_lane, norm_g]
    if state0 is not None:
        in_specs.append(pl.BlockSpec((None, None, 2, 2, 128, SSD_STATE),
                                     lambda b, s: (b, layer, 0, 0, 0, 0)))
        args.append(state0)
    outs = [jax.ShapeDtypeStruct((m, 256), BF16)]
    out_specs = [pl.BlockSpec((c_len, 256), lambda b, s: (b * nc + jnp.maximum(s - nc, 0), 0))]
    if emit_state:
        outs.append(jax.ShapeDtypeStruct((nb, 2, 2, 128, SSD_STATE), F32))
        out_specs.append(pl.BlockSpec((None, 2, 2, 128, SSD_STATE), lambda b, s: (b, 0, 0, 0, 0)))
    scratch = [pltpu.VMEM((2, 128, SSD_STATE), F32), pltpu.VMEM((2, 128, SSD_STATE), F32),
               pltpu.VMEM((nc, 2, 128, SSD_STATE), F32)]
    return pl.pallas_call(
        functools.partial(_ssd_kernel, nc=nc, has_init=state0 is not None, emit_state=emit_state),
        out_shape=outs, grid=(nb, 2 * nc), in_specs=in_specs, out_specs=out_specs,
        scratch_shapes=scratch, compiler_params=_params(("arbitrary", "arbitrary")), name="ssd",
    )(*args)


def _layer_norm(x, g, b):
    mu = jnp.mean(x, axis=-1, keepdims=True)
    d = x - mu
    var = jnp.mean(d * d, axis=-1, keepdims=True)
    return d * lax.rsqrt(var + 1e-5) * g + b


FFN_CHUNK = 256


def _outffn_kernel(x_ref, yr_ref, yg_ref, ys_ref, yw_ref, g1_ref, sh2_ref, sc2_ref, g2_ref,
                   wo_ref, wi_ref, wf_ref, l1g_ref, l1b_ref, l2g_ref, l2b_ref, o_ref):
    mix = (_dot(yr_ref[...], wo_ref[0:256, :]) + _dot(yg_ref[...], wo_ref[256:512, :])
           + _dot(ys_ref[...], wo_ref[512:768, :]) + _dot(yw_ref[...], wo_ref[768:1024, :]))
    x1 = _layer_norm(ALPHA * x_ref[...] + g1_ref[...] * mix, l1g_ref[...], l1b_ref[...])
    h2 = (x1 * (1.0 + sc2_ref[...]) + sh2_ref[...]).astype(BF16)
    acc = jnp.zeros(x1.shape, F32)
    for j in range(D_FF // FFN_CHUNK):
        cols = slice(FFN_CHUNK * j, FFN_CHUNK * (j + 1))
        gate = _dot(h2, wi_ref[:, cols])
        up = _dot(h2, wi_ref[:, D_FF + FFN_CHUNK * j:D_FF + FFN_CHUNK * (j + 1)])
        acc = acc + _dot((_silu(gate) * up).astype(BF16), wf_ref[cols, :])
    o_ref[...] = _layer_norm(ALPHA * x1 + g2_ref[...] * acc, l2g_ref[...], l2b_ref[...])


def _outffn_call(x, ys, mod5, w_out_b, w_ffn_in_b, w_ffn_out_b, ln, layer, row0, tiles_per_row):
    m = x.shape[0]
    tm = TOKEN_TILE
    row = lambda w: pl.BlockSpec((tm, w), lambda i: (i, 0))
    once = pl.Buffered(1)
    vec = pl.BlockSpec((None, 1, D_MODEL), lambda i: (layer, 0, 0))
    in_specs = [row(D_MODEL), row(256), row(256), row(256), row(256),
                _mod_spec(layer, row0, tiles_per_row, 2), _mod_spec(layer, row0, tiles_per_row, 3),
                _mod_spec(layer, row0, tiles_per_row, 4), _mod_spec(layer, row0, tiles_per_row, 5),
                pl.BlockSpec((None, D_MODEL, D_MODEL), lambda i: (layer, 0, 0), pipeline_mode=once),
                pl.BlockSpec((None, D_MODEL, 2 * D_FF), lambda i: (layer, 0, 0), pipeline_mode=once),
                pl.BlockSpec((None, D_FF, D_MODEL), lambda i: (layer, 0, 0), pipeline_mode=once),
                vec, vec, vec, vec]
    return pl.pallas_call(
        _outffn_kernel,
        out_shape=jax.ShapeDtypeStruct((m, D_MODEL), F32),
        grid=(m // tm,), in_specs=in_specs, out_specs=row(D_MODEL),
        compiler_params=_params(("arbitrary",)), name="out_ffn",
    )(x, *ys, mod5, mod5, mod5, mod5, w_out_b, w_ffn_in_b, w_ffn_out_b, *ln)


def _rope_tables(seq_len):
    t = jnp.arange(seq_len)
    rowp = (t // GRID_W).astype(F32)
    colp = (t % GRID_W).astype(F32)
    quarter = HEAD_DIM // 4
    inv_freq = ROPE_THETA ** (-jnp.arange(quarter, dtype=F32) / quarter)
    ang_r = rowp[:, None] * inv_freq[None, :]
    ang_c = colp[:, None] * inv_freq[None, :]
    cos = jnp.concatenate([jnp.cos(ang_r)] * 2 + [jnp.cos(ang_c)] * 2, axis=-1)
    sin = jnp.concatenate([-jnp.sin(ang_r), jnp.sin(ang_r), -jnp.sin(ang_c), jnp.sin(ang_c)], axis=-1)
    return jnp.tile(cos, (1, N_HEADS)), jnp.tile(sin, (1, N_HEADS))


def _dup_cache(c):
    return jnp.concatenate([c[:, :, 0], c[:, :, 0], c[:, :, 1], c[:, :, 1]], axis=-1).astype(BF16)


def _reorder_w_in(w):
    pad = jnp.zeros(w.shape[:-1] + (PROJ_W - C_DT - 8,), w.dtype)
    return jnp.concatenate([w[..., :1536], w[..., 1792:2560], w[..., 1536:1792],
                            w[..., 2568:3080], w[..., 2560:2568], pad], axis=-1).astype(BF16)


def kernel(x_prompt, x_sample, cache_gqa_k, cache_gqa_v, cache_swa_k, cache_swa_v, state_ret, state_ssd,
           c, c_ctx, ada_w, ada_b, w_in, ret_decay, ret_gn_g, gqa_q_norm, gqa_k_norm,
           ssd_conv_w, ssd_conv_b, ssd_dt_bias, ssd_a_log, ssd_d, ssd_norm_g, swa_sink, w_out,
           ln1_g, ln1_b, w_ffn_in, w_ffn_out, ln2_g, ln2_b):
    nb_c, len_c, _ = x_prompt.shape
    nb_l, len_l, _ = x_sample.shape
    depth = ada_w.shape[0]

    cond8 = jnp.zeros((8, D_MODEL), F32).at[0].set(c_ctx).at[1:1 + nb_l].set(c)
    mod5 = _ada_call(cond8, ada_w, ada_b)

    w_in_p = _reorder_w_in(w_in)
    w_out_b = w_out.astype(BF16)
    w_ffn_in_b = w_ffn_in.astype(BF16)
    w_ffn_out_b = w_ffn_out.astype(BF16)
    lane_head = jnp.arange(256) // HEAD_DIM
    seg = (lane_head[:, None] == lane_head[None, :]).astype(BF16)
    rope_tabs = _rope_tables(len_l)
    caches_g = (_dup_cache_layers(cache_gqa_k), _dup_cache_layers(cache_gqa_v))
    caches_w = (_dup_cache_layers(cache_swa_k), _dup_cache_layers(cache_swa_v))
    state_ssd_g = state_ssd.reshape(nb_l, depth, 2, 2, 128, SSD_STATE)

    def layer_consts(l):
        rd = ret_decay[l]
        return dict(
            rd_lane=jnp.repeat(rd, HEAD_DIM, axis=-1).reshape(2, 1, 256),
            rd_head=jnp.broadcast_to(rd[:, :, None, None], (2, N_HEADS, 1, CHUNK)),
            gn_g=ret_gn_g[l].reshape(1, 256),
            qn=jnp.tile(gqa_q_norm[l], N_HEADS).reshape(1, 256),
            kn=jnp.tile(gqa_k_norm[l], N_HEADS).reshape(1, 256),
            conv_b=ssd_conv_b[l].reshape(1, 768),
            dt_bias=jnp.zeros((1, 128), F32).at[0, :8].set(ssd_dt_bias[l].reshape(8)),
            a_log=jnp.zeros((1, 128), F32).at[0, :8].set(ssd_a_log[l].reshape(8)),
            d_lane=jnp.repeat(ssd_d[l], HEAD_DIM).reshape(1, 256),
            norm_g=ssd_norm_g[l].reshape(1, 256),
            sink=jnp.broadcast_to(swa_sink[l][:, None, None], (N_HEADS, 1, 128)),
        )

    ln = tuple(a.reshape(depth, 1, D_MODEL) for a in (ln1_g, ln1_b, ln2_g, ln2_b))

    def run_group(x, nb, seq_len, row0, latent):
        m = nb * seq_len
        tpr = (seq_len // TOKEN_TILE) if latent else (m // TOKEN_TILE)
        x = x.reshape(m, D_MODEL)
        extras = []
        for l in range(depth):
            k = layer_consts(l)
            p = _inproj_call(x, mod5, w_in_p, l, row0, tpr)
            prep = _prep_call(p, k["qn"], k["kn"], seg, rope_tabs if latent else None, seq_len)
            qg, kg, vg, qw, kw, vw = prep[:6]
            if latent:
                y_ret = _ret_call(p, nb, seq_len, k["rd_lane"], k["rd_head"], k["gn_g"], seg, l,
                                  state0=state_ret)[0]
                y_ssd = _ssd_call(p, nb, seq_len, ssd_conv_w, k["conv_b"], k["dt_bias"], k["a_log"],
                                  k["d_lane"], k["norm_g"], l, state0=state_ssd_g)[0]
                y_gqa = _attn_call(qg, kg, vg, nb, seq_len,
                                   cache=(caches_g[0][l], caches_g[1][l]))
                y_swa = _attn_call(qw, kw, vw, nb, seq_len,
                                   cache=(caches_w[0][l], caches_w[1][l]), sink=k["sink"], window=True)
            else:
                y_ret, st_ret = _ret_call(p, nb, seq_len, k["rd_lane"], k["rd_head"], k["gn_g"], seg, l,
                                          emit_state=True)
                y_ssd, st_ssd = _ssd_call(p, nb, seq_len, ssd_conv_w, k["conv_b"], k["dt_bias"],
                                          k["a_log"], k["d_lane"], k["norm_g"], l, emit_state=True)
                y_gqa = _attn_call(qg, kg, vg, nb, seq_len)
                y_swa = _attn_call(qw, kw, vw, nb, seq_len, sink=k["sink"])
                heads = lambda a: a.reshape(nb, seq_len, 2, HEAD_DIM)
                extras.append((heads(prep[6]), heads(p[:, C_GV:C_GV + 128]),
                               heads(p[:, C_WK:C_WK + 128]), heads(p[:, C_WV:C_WV + 128]),
                               st_ret, st_ssd.reshape(nb, 2, N_HEADS, HEAD_DIM, SSD_STATE)))
            x = _outffn_call(x, (y_ret, y_gqa, y_ssd, y_swa), mod5, w_out_b, w_ffn_in_b, w_ffn_out_b,
                             ln, l, row0, tpr)
        return x.reshape(nb, seq_len, D_MODEL), extras

    y_prompt, extras = run_group(x_prompt, nb_c, len_c, 0, False)
    y_sample, _ = run_group(x_sample, nb_l, len_l, 1, True)
    stacked = tuple(jnp.stack([e[i] for e in extras], axis=1) for i in range(6))
    return (y_prompt, y_sample) + stacked


def _dup_cache_layers(cache):
    return [_dup_cache(cache[:, l]) for l in range(cache.shape[1])]
```

```python
import functools
import math

import jax
import jax.numpy as jnp
from jax import lax
from jax.experimental import pallas as pl
from jax.experimental.pallas import tpu as pltpu

F32 = jnp.float32
BF16 = jnp.bfloat16

D_MODEL = 1024
DEPTH = 4
HEAD_DIM = 64
N_HEADS = 4
GRID_W = 64
ROPE_THETA = 10000.0
SSD_STATE = 128
D_FF = 2816
WINDOW = 128
PAST_LEN = 256
ALPHA = (2.0 * DEPTH) ** 0.25
CHUNK = 256
TOKEN_TILE = 256
VMEM_LIMIT = 56 * 1024 * 1024
LOG2E = math.log2(math.e)
QK_SCALE = HEAD_DIM ** -0.5 * LOG2E

C_RQ, C_RK, C_RV, C_RG = 0, 256, 512, 768
C_GQ, C_GK, C_GV = 1024, 1280, 1408
C_XBC, C_SZ = 1536, 2304
C_WQ, C_WK, C_WV = 2560, 2816, 2944
C_DT = 3072
PROJ_W = 3200


def _silu(x):
    return x / (1.0 + jnp.exp(-x))


def _dot(a, b):
    return jnp.dot(a, b, preferred_element_type=F32)


def _dot_nt(a, b):
    return lax.dot_general(a, b, (((1,), (1,)), ((), ())), preferred_element_type=F32)


def _dot_tn(a, b):
    return lax.dot_general(a, b, (((0,), (0,)), ((), ())), preferred_element_type=F32)


def _seg_sum(x, seg):
    hi = x.astype(BF16)
    lo = (x - hi.astype(F32)).astype(BF16)
    return _dot(hi, seg) + _dot(lo, seg)


def _params(sem):
    return pltpu.CompilerParams(dimension_semantics=sem, vmem_limit_bytes=VMEM_LIMIT)


def _ada_kernel(c_ref, w_ref, b_ref, o_ref):
    s = _silu(c_ref[...])
    o_ref[...] = _dot(s.astype(BF16), w_ref[...].astype(BF16)) + b_ref[...]


def _ada_call(cond8, ada_w, ada_b):
    tn = D_MODEL
    nt = 6
    out = pl.pallas_call(
        _ada_kernel,
        out_shape=jax.ShapeDtypeStruct((DEPTH, nt, 8, tn), F32),
        grid=(DEPTH, nt),
        in_specs=[pl.BlockSpec((8, D_MODEL), lambda l, j: (0, 0)),
                  pl.BlockSpec((None, D_MODEL, tn), lambda l, j: (l, 0, j)),
                  pl.BlockSpec((None, None, 1, tn), lambda l, j: (l, j, 0, 0))],
        out_specs=pl.BlockSpec((None, None, 8, tn), lambda l, j: (l, j, 0, 0)),
        compiler_params=_params(("arbitrary", "arbitrary")),
        name="ada_mod",
    )(cond8, ada_w, ada_b.reshape(DEPTH, nt, 1, tn))
    return jnp.transpose(out, (0, 2, 1, 3)).reshape(DEPTH, 8, nt, 1, tn)


def _mod_spec(layer, row0, tiles_per_row, k):
    return pl.BlockSpec((None, None, None, 1, D_MODEL),
                        lambda i: (layer, row0 + i // tiles_per_row, k, 0, 0))


def _inproj_kernel(x_ref, sh_ref, sc_ref, w_ref, o_ref):
    h = x_ref[...] * (1.0 + sc_ref[...]) + sh_ref[...]
    o_ref[...] = _dot(h.astype(BF16), w_ref[...])


def _inproj_call(x, mod5, w_in_p, layer, row0, tiles_per_row):
    m = x.shape[0]
    tm = TOKEN_TILE
    return pl.pallas_call(
        _inproj_kernel,
        out_shape=jax.ShapeDtypeStruct((m, PROJ_W), F32),
        grid=(m // tm,),
        in_specs=[pl.BlockSpec((tm, D_MODEL), lambda i: (i, 0)),
                  _mod_spec(layer, row0, tiles_per_row, 0),
                  _mod_spec(layer, row0, tiles_per_row, 1),
                  pl.BlockSpec((None, D_MODEL, PROJ_W), lambda i: (layer, 0, 0))],
        out_specs=pl.BlockSpec((tm, PROJ_W), lambda i: (i, 0)),
        compiler_params=_params(("arbitrary",)),
        name="in_proj",
    )(x, mod5, mod5, w_in_p)


def _rope(x, cos, sin_signed):
    w = x.shape[-1]
    lane = lax.broadcasted_iota(jnp.int32, x.shape, 1)
    partner = jnp.where((lane % 32) < 16, pltpu.roll(x, w - 16, 1), pltpu.roll(x, 16, 1))
    return x * cos + partner * sin_signed


def _expand_q(q):
    lane = lax.broadcasted_iota(jnp.int32, (q.shape[0], 128), 1)
    low = lane < HEAD_DIM
    blocks = []
    for j in range(2):
        pair = q[:, 128 * j:128 * (j + 1)]
        blocks.append(jnp.where(low, pair, 0.0))
        blocks.append(jnp.where(low, pltpu.roll(pair, HEAD_DIM, 1), 0.0))
    return jnp.concatenate(blocks, axis=1).astype(BF16)


def _expand_kv(k, fill):
    lane = lax.broadcasted_iota(jnp.int32, k.shape, 1)
    low = lane < HEAD_DIM
    return jnp.concatenate([jnp.where(low, k, fill),
                            jnp.where(low, pltpu.roll(k, HEAD_DIM, 1), fill)], axis=1).astype(BF16)


def _prep_kernel(*refs, rope):
    gq, gk, gv, wq, wk, wv, qn, kn, seg = refs[:9]
    i = 9
    if rope:
        cos_ref, sin_ref = refs[9:11]
        i = 11
    qg_o, kg_o, vg_o, qw_o, kw_o, vw_o = refs[i:i + 6]
    seg_m = seg[...]

    def rms(x, g, s):
        ms = _seg_sum(x * x, s) * (1.0 / HEAD_DIM)
        return x * lax.rsqrt(ms + 1e-6) * g

    q_g = rms(gq[...], qn[...], seg_m)
    k_g = rms(gk[...], kn[...][:, :128], seg_m[:128, :128])
    q_w = wq[...]
    k_w = wk[...]
    if rope:
        cos = cos_ref[...]
        sin = sin_ref[...]
        q_g = _rope(q_g, cos, sin)
        q_w = _rope(q_w, cos, sin)
        k_g = _rope(k_g, cos[:, :128], sin[:, :128])
        k_w = _rope(k_w, cos[:, :128], sin[:, :128])
    else:
        refs[i + 6][...] = k_g
    qg_o[...] = _expand_q(q_g * QK_SCALE)
    qw_o[...] = _expand_q(q_w * QK_SCALE)
    kg_o[...] = _expand_kv(k_g, 0.0)
    kw_o[...] = _expand_kv(k_w, 0.0)
    vg_o[...] = _expand_kv(gv[...], 1.0)
    vw_o[...] = _expand_kv(wv[...], 1.0)


def _prep_call(p, qn256, kn256, seg, rope_tabs, seq_len):
    m = p.shape[0]
    tm = TOKEN_TILE
    rope = rope_tabs is not None

    def col(width, start):
        return pl.BlockSpec((tm, width), lambda i: (i, start // width))

    const = lambda shape: pl.BlockSpec(shape, lambda i: (0, 0))
    in_specs = [col(256, C_GQ), col(128, C_GK), col(128, C_GV),
                col(256, C_WQ), col(128, C_WK), col(128, C_WV),
                const((1, 256)), const((1, 256)), const((256, 256))]
    args = [p, p, p, p, p, p, qn256, kn256, seg]
    if rope:
        tpb = seq_len // tm
        tab = pl.BlockSpec((tm, 256), lambda i: (i % tpb, 0))
        in_specs += [tab, tab]
        args += list(rope_tabs)
    outs = [jax.ShapeDtypeStruct((m, 512), BF16), jax.ShapeDtypeStruct((m, 256), BF16),
            jax.ShapeDtypeStruct((m, 256), BF16), jax.ShapeDtypeStruct((m, 512), BF16),
            jax.ShapeDtypeStruct((m, 256), BF16), jax.ShapeDtypeStruct((m, 256), BF16)]
    row = lambda w: pl.BlockSpec((tm, w), lambda i: (i, 0))
    out_specs = [row(512), row(256), row(256), row(512), row(256), row(256)]
    if not rope:
        outs.append(jax.ShapeDtypeStruct((m, 128), F32))
        out_specs.append(row(128))
    return pl.pallas_call(
        functools.partial(_prep_kernel, rope=rope),
        out_shape=outs, grid=(m // tm,), in_specs=in_specs, out_specs=out_specs,
        compiler_params=_params(("arbitrary",)), name="attn_prep",
    )(*args)


def _online(m, acc, s, v):
    m_new = jnp.maximum(m, jnp.max(s, axis=-1, keepdims=True))
    p = jnp.exp2(s - m_new)
    acc = jnp.exp2(m - m_new) * acc + _dot(p.astype(BF16), v)
    return m_new, acc


def _attn_kernel(*refs, seq_len, tq, kv_chunk, window, use_cache, use_sink):
    q_ref, k_ref, v_ref = refs[:3]
    i = 3
    if use_cache:
        kc_ref, vc_ref = refs[3:5]
        i = 5
    if use_sink:
        sink_ref = refs[i]
        i += 1
    o_ref = refs[i]
    qi = pl.program_id(1)
    rows = 2 * tq
    q2 = [jnp.concatenate([q_ref[:, 256 * j:256 * j + 128], q_ref[:, 256 * j + 128:256 * (j + 1)]], axis=0)
          for j in range(2)]

    def update(carry, k_blk, v_blk, mask=None):
        out = []
        for j in range(2):
            lanes = slice(128 * j, 128 * (j + 1))
            s = _dot_nt(q2[j], k_blk[:, lanes])
            if mask is not None:
                s = jnp.where(mask, s, -jnp.inf)
            out.append(_online(*carry[j], s, v_blk[:, lanes]))
        return tuple(out)

    carry = tuple((jnp.full((rows, 1), -jnp.inf, F32), jnp.zeros((rows, 128), F32)) for _ in range(2))
    if window:
        span = tq + 2 * WINDOW
        start = pl.multiple_of(jnp.clip(qi * tq - WINDOW, 0, seq_len - span), 128)
        qpos = qi * tq + lax.broadcasted_iota(jnp.int32, (rows, span), 0) % tq
        kpos = start + lax.broadcasted_iota(jnp.int32, (rows, span), 1)
        carry = update(carry, k_ref[pl.ds(start, span), :], v_ref[pl.ds(start, span), :],
                       jnp.abs(qpos - kpos) <= WINDOW)
    else:
        def body(c, cr):
            st = pl.multiple_of(c * kv_chunk, kv_chunk)
            return update(cr, k_ref[pl.ds(st, kv_chunk), :], v_ref[pl.ds(st, kv_chunk), :])
        carry = lax.fori_loop(0, seq_len // kv_chunk, body, carry, unroll=2 if seq_len > kv_chunk else 1)
    if use_cache:
        carry = update(carry, kc_ref[...], vc_ref[...])

    low = lax.broadcasted_iota(jnp.int32, (tq, 128), 1) < HEAD_DIM
    for j in range(2):
        m, acc = carry[j]
        if use_sink:
            top = lax.broadcasted_iota(jnp.int32, (rows, 1), 0) < tq
            sk = jnp.where(top, sink_ref[2 * j][:, 0:1], sink_ref[2 * j + 1][:, 0:1]) * LOG2E
            m2 = jnp.maximum(m, sk)
            upper = lax.broadcasted_iota(jnp.int32, (rows, 128), 1) >= HEAD_DIM
            acc = acc * jnp.exp2(m - m2) + jnp.where(upper, jnp.exp2(sk - m2), 0.0)
        o = acc / pltpu.roll(acc, HEAD_DIM, 1)
        o_ref[:, 128 * j:128 * (j + 1)] = jnp.where(low, o[:tq], pltpu.roll(o[tq:], HEAD_DIM, 1)).astype(o_ref.dtype)


def _attn_call(q, k, v, nb, seq_len, cache=None, sink=None, window=False):
    tq = TOKEN_TILE
    nq = seq_len // tq
    kv_chunk = min(512, seq_len)
    in_specs = [pl.BlockSpec((tq, 512), lambda b, i: (b * nq + i, 0)),
                pl.BlockSpec((seq_len, 256), lambda b, i: (b, 0)),
                pl.BlockSpec((seq_len, 256), lambda b, i: (b, 0))]
    args = [q, k, v]
    if cache is not None:
        cspec = pl.BlockSpec((None, PAST_LEN, 256), lambda b, i: (b, 0, 0))
        in_specs += [cspec, cspec]
        args += list(cache)
    if sink is not None:
        in_specs.append(pl.BlockSpec((N_HEADS, 1, 128), lambda b, i: (0, 0, 0)))
        args.append(sink)
    return pl.pallas_call(
        functools.partial(_attn_kernel, seq_len=seq_len, tq=tq, kv_chunk=kv_chunk, window=window,
                          use_cache=cache is not None, use_sink=sink is not None),
        out_shape=jax.ShapeDtypeStruct((nb * seq_len, 256), BF16),
        grid=(nb, nq), in_specs=in_specs,
        out_specs=pl.BlockSpec((tq, 256), lambda b, i: (b * nq + i, 0)),
        compiler_params=_params(("arbitrary", "arbitrary")), name="attention",
    )(*args)


def _chunk_of(s, nc):
    return jnp.where(s < nc, nc - 1 - s, s - nc)


def _log_gamma(rd):
    return jnp.log1p(-jnp.exp(rd))


def _ret_kernel(*refs, nc, has_init, emit_state):
    q_ref, k_ref, v_ref, g_ref, rdl_ref, rdh_ref, gn_ref, seg_ref = refs[:8]
    i = 8
    if has_init:
        s0_ref = refs[i]
        i += 1
    y_ref = refs[i]
    i += 1
    if emit_state:
        st_ref = refs[i]
        i += 1
    m_sc, dec_sc, qk_sc, sf_sc, sb_sc, sball_sc = refs[i:i + 6]
    c_len = CHUNK
    b = pl.program_id(0)
    s = pl.program_id(1)

    @pl.when(jnp.logical_and(b == 0, s == 0))
    def _():
        ri = lax.broadcasted_iota(jnp.int32, (c_len, c_len), 0)
        ci = lax.broadcasted_iota(jnp.int32, (c_len, c_len), 1)
        d = (ri - ci).astype(F32)
        for h in range(N_HEADS):
            lgf = _log_gamma(jnp.broadcast_to(rdh_ref[0, h], (c_len, c_len)))
            lgb = _log_gamma(jnp.broadcast_to(rdh_ref[1, h], (c_len, c_len)))
            m_sc[h] = jnp.where(d > 0, jnp.exp(d * lgf), jnp.where(d < 0, jnp.exp(-d * lgb), 2.0))
        for dr in range(2):
            rows = [jnp.exp(c_len * _log_gamma(jnp.broadcast_to(rdh_ref[dr, h], (HEAD_DIM, c_len))))
                    for h in range(N_HEADS)]
            dec_sc[dr] = jnp.concatenate(rows, axis=0)
        a = lax.broadcasted_iota(jnp.int32, (c_len, 256), 0).astype(F32)
        lgf = _log_gamma(rdl_ref[0])
        lgb = _log_gamma(rdl_ref[1])
        qk_sc[0] = jnp.exp((a + 1.0) * lgf)
        qk_sc[1] = jnp.exp((c_len - 1.0 - a) * lgf)
        qk_sc[2] = jnp.exp((c_len - a) * lgb)
        qk_sc[3] = jnp.exp(a * lgb)

    @pl.when(s == 0)
    def _():
        sf_sc[...] = jnp.zeros_like(sf_sc)
        sb_sc[...] = jnp.zeros_like(sb_sc)
        if has_init:
            for h in range(N_HEADS):
                blk = slice(HEAD_DIM * h, HEAD_DIM * (h + 1))
                sf_sc[blk, blk] = s0_ref[0, h]
                sb_sc[blk, blk] = s0_ref[1, h]

    ri = lax.broadcasted_iota(jnp.int32, (256, 256), 0) // HEAD_DIM
    ci = lax.broadcasted_iota(jnp.int32, (256, 256), 1) // HEAD_DIM
    diag = ri == ci
    ks = k_ref[...] * (HEAD_DIM ** -0.5)
    vb = v_ref[...].astype(BF16)

    @pl.when(s < nc)
    def _():
        c = nc - 1 - s
        sball_sc[c] = sb_sc[...]
        upd = _dot_tn((ks * qk_sc[3]).astype(BF16), vb)
        sb_sc[...] = sb_sc[...] * dec_sc[1] + jnp.where(diag, upd, 0.0)
        if emit_state:
            @pl.when(s == nc - 1)
            def _():
                for h in range(N_HEADS):
                    blk = slice(HEAD_DIM * h, HEAD_DIM * (h + 1))
                    st_ref[1, h] = sb_sc[blk, blk]

    @pl.when(s >= nc)
    def _():
        c = s - nc
        q = q_ref[...]
        qb = q.astype(BF16)
        kb = ks.astype(BF16)
        lane_head = lax.broadcasted_iota(jnp.int32, (c_len, 256), 1) // HEAD_DIM
        o = (_dot(qb, sf_sc[...].astype(BF16)) * qk_sc[0]
             + _dot(qb, sball_sc[c].astype(BF16)) * qk_sc[2])
        for h in range(N_HEADS):
            mine = lane_head == h
            sc = _dot_nt(jnp.where(mine, q, 0.0).astype(BF16), kb)
            pv = _dot((sc * m_sc[h]).astype(BF16), vb)
            o = o + jnp.where(mine, pv, 0.0)
        seg = seg_ref[...]
        mu = _seg_sum(o, seg) * (1.0 / HEAD_DIM)
        dlt = o - mu
        var = _seg_sum(dlt * dlt, seg) * (1.0 / HEAD_DIM)
        on = dlt * lax.rsqrt(var + 1e-5) * gn_ref[...]
        y_ref[...] = (on * _silu(g_ref[...])).astype(y_ref.dtype)
        upd = _dot_tn((ks * qk_sc[1]).astype(BF16), vb)
        sf_sc[...] = sf_sc[...] * dec_sc[0] + jnp.where(diag, upd, 0.0)
        if emit_state:
            @pl.when(s == 2 * nc - 1)
            def _():
                for h in range(N_HEADS):
                    blk = slice(HEAD_DIM * h, HEAD_DIM * (h + 1))
                    st_ref[0, h] = sf_sc[blk, blk]


def _ret_call(p, nb, seq_len, rd_lane, rd_head, gn_g, seg, layer, state0=None, emit_state=False):
    c_len = CHUNK
    nc = seq_len // c_len

    def col(start):
        return pl.BlockSpec((c_len, 256), lambda b, s: (b * nc + _chunk_of(s, nc), start // 256))

    in_specs = [col(C_RQ), col(C_RK), col(C_RV), col(C_RG),
                pl.BlockSpec((2, 1, 256), lambda b, s: (0, 0, 0)),
                pl.BlockSpec((2, N_HEADS, 1, c_len), lambda b, s: (0, 0, 0, 0)),
                pl.BlockSpec((1, 256), lambda b, s: (0, 0)),
                pl.BlockSpec((256, 256), lambda b, s: (0, 0))]
    args = [p, p, p, p, rd_lane, rd_head, gn_g, seg]
    if state0 is not None:
        in_specs.append(pl.BlockSpec((None, None, 2, N_HEADS, HEAD_DIM, HEAD_DIM),
                                     lambda b, s: (b, layer, 0, 0, 0, 0)))
        args.append(state0)
    outs = [jax.ShapeDtypeStruct((nb * seq_len, 256), BF16)]
    out_specs = [pl.BlockSpec((c_len, 256), lambda b, s: (b * nc + jnp.maximum(s - nc, 0), 0))]
    if emit_state:
        outs.append(jax.ShapeDtypeStruct((nb, 2, N_HEADS, HEAD_DIM, HEAD_DIM), F32))
        out_specs.append(pl.BlockSpec((None, 2, N_HEADS, HEAD_DIM, HEAD_DIM),
                                      lambda b, s: (b, 0, 0, 0, 0)))
    scratch = [pltpu.VMEM((N_HEADS, c_len, c_len), F32), pltpu.VMEM((2, 256, 256), F32),
               pltpu.VMEM((4, c_len, 256), F32), pltpu.VMEM((256, 256), F32),
               pltpu.VMEM((256, 256), F32), pltpu.VMEM((nc, 256, 256), F32)]
    return pl.pallas_call(
        functools.partial(_ret_kernel, nc=nc, has_init=state0 is not None, emit_state=emit_state),
        out_shape=outs, grid=(nb, 2 * nc), in_specs=in_specs, out_specs=out_specs,
        scratch_shapes=scratch, compiler_params=_params(("arbitrary", "arbitrary")),
        name="retention",
    )(*args)


def _cumsum_rows(x, reverse):
    n = x.shape[0]
    row = lax.broadcasted_iota(jnp.int32, x.shape, 0)
    sh = 1
    while sh < n:
        if reverse:
            x = x + jnp.where(row < n - sh, pltpu.roll(x, n - sh, 0), 0.0)
        else:
            x = x + jnp.where(row >= sh, pltpu.roll(x, sh, 0), 0.0)
        sh *= 2
    return x


def _ssd_kernel(*refs, nc, has_init, emit_state):
    xbc_ref, prev_ref, next_ref, z_ref, dt_ref, cw_ref, cb_ref, dtb_ref, al_ref, dl_ref, ng_ref = refs[:11]
    i = 11
    if has_init:
        h0_ref = refs[i]
        i += 1
    y_ref = refs[i]
    i += 1
    if emit_state:
        st_ref = refs[i]
        i += 1
    hf_sc, hb_sc, hball_sc = refs[i:i + 3]
    c_len = CHUNK
    s = pl.program_id(1)
    c = _chunk_of(s, nc)

    @pl.when(s == 0)
    def _():
        if has_init:
            hf_sc[...] = h0_ref[0]
            hb_sc[...] = h0_ref[1]
        else:
            hf_sc[...] = jnp.zeros_like(hf_sc)
            hb_sc[...] = jnp.zeros_like(hb_sc)

    xbc = xbc_ref[...]
    row = lax.broadcasted_iota(jnp.int32, xbc.shape, 0)
    before = jnp.where(c > 0, prev_ref[7:8, :], 0.0)
    after = jnp.where(c < nc - 1, next_ref[0:1, :], 0.0)
    x_prev = jnp.where(row == 0, before, pltpu.roll(xbc, 1, 0))
    x_next = jnp.where(row == c_len - 1, after, pltpu.roll(xbc, c_len - 1, 0))
    conv = _silu(cw_ref[0:1, :] * x_prev + cw_ref[1:2, :] * xbc + cw_ref[2:3, :] * x_next + cb_ref[...])
    xs = conv[:, 0:256]
    bm = conv[:, 256:512].astype(BF16)
    cm = conv[:, 512:768].astype(BF16)

    pre = dt_ref[...] + dtb_ref[...]
    dt = jnp.maximum(pre, 0.0) + jnp.log1p(jnp.exp(-jnp.abs(pre)))
    dta = dt * (-jnp.exp(al_ref[...]))
    cum = _cumsum_rows(dta, reverse=False)
    rev = _cumsum_rows(dta, reverse=True)
    low = lax.broadcasted_iota(jnp.int32, (c_len, 128), 1) < HEAD_DIM
    low_rows = lax.broadcasted_iota(jnp.int32, (128, 128), 0) < HEAD_DIM

    def col(a, lane):
        return jnp.broadcast_to(a[:, lane:lane + 1], (c_len, c_len))

    def pair(a0, a1):
        return jnp.where(low, a0[:, :128], a1[:, :128])

    def state_update(h_sc, g, cols, tot_row, dt_lanes):
        w = [jnp.exp(tot_row[r] - cols[r]) * col(dt, dt_lanes[r]) for r in range(2)]
        xw = (xs[:, 128 * g:128 * (g + 1)] * pair(w[0], w[1])).astype(BF16)
        dec = jnp.where(low_rows, jnp.broadcast_to(jnp.exp(tot_row[0][:, :128]), (128, 128)),
                        jnp.broadcast_to(jnp.exp(tot_row[1][:, :128]), (128, 128)))
        h_sc[g] = h_sc[g] * dec + _dot_tn(xw, bm[:, 128 * g:128 * (g + 1)])

    @pl.when(s < nc)
    def _():
        hball_sc[c] = hb_sc[...]
        for g in range(2):
            cols = [col(rev, 4 + 2 * g + r) for r in range(2)]
            tot = [cl[0:1, :] for cl in cols]
            state_update(hb_sc, g, cols, tot, [4 + 2 * g, 5 + 2 * g])
        if emit_state:
            @pl.when(s == nc - 1)
            def _():
                st_ref[1] = hb_sc[...]

    @pl.when(s >= nc)
    def _():
        cum_t = cum.T
        rev_t = rev.T
        dt_t = dt.T
        ri = lax.broadcasted_iota(jnp.int32, (c_len, c_len), 0)
        ci = lax.broadcasted_iota(jnp.int32, (c_len, c_len), 1)
        lane_head = lax.broadcasted_iota(jnp.int32, (c_len, 256), 1) // HEAD_DIM
        xs_b = xs.astype(BF16)
        y = jnp.zeros((c_len, 256), F32)
        inter = []
        for g in range(2):
            gl = slice(128 * g, 128 * (g + 1))
            gmat = _dot_nt(cm[:, gl], bm[:, gl])
            cols_f, cols_b = [], []
            for r in range(2):
                h = 2 * g + r
                cf = col(cum, h)
                cb = col(rev, 4 + h)
                cols_f.append(cf)
                cols_b.append(cb)
                wf = jnp.where(ri >= ci, jnp.exp(jnp.minimum(cf - cum_t[h:h + 1, :], 0.0)) * dt_t[h:h + 1, :], 0.0)
                wb = jnp.where(ci >= ri, jnp.exp(jnp.minimum(cb - rev_t[4 + h:5 + h, :], 0.0))
                               * dt_t[4 + h:5 + h, :], 0.0)
                yh = _dot((gmat * (wf + wb)).astype(BF16), xs_b)
                y = y + jnp.where(lane_head == h, yh, 0.0)
            yf = _dot_nt(cm[:, gl], hf_sc[g].astype(BF16))
            yb = _dot_nt(cm[:, gl], hball_sc[c, g].astype(BF16))
            inter.append(yf * pair(jnp.exp(cols_f[0]), jnp.exp(cols_f[1]))
                         + yb * pair(jnp.exp(cols_b[0]), jnp.exp(cols_b[1])))
            tot = [cl[c_len - 1:c_len, :] for cl in cols_f]
            state_update(hf_sc, g, cols_f, tot, [2 * g, 2 * g + 1])
        y = y + jnp.concatenate(inter, axis=1) + xs * dl_ref[...]
        y = y * _silu(z_ref[...])
        ms = jnp.mean(y * y, axis=-1, keepdims=True)
        y_ref[...] = (y * lax.rsqrt(ms + 1e-6) * ng_ref[...]).astype(y_ref.dtype)
        if emit_state:
            @pl.when(s == 2 * nc - 1)
            def _():
                st_ref[0] = hf_sc[...]


def _ssd_call(p, nb, seq_len, conv_w, conv_b, dt_bias, a_log, d_lane, norm_g, layer,
              state0=None, emit_state=False):
    c_len = CHUNK
    nc = seq_len // c_len
    m = nb * seq_len
    r8 = c_len // 8
    chunk = lambda b, s: b * nc + _chunk_of(s, nc)
    in_specs = [pl.BlockSpec((c_len, 768), lambda b, s: (chunk(b, s), C_XBC // 768)),
                pl.BlockSpec((8, 768), lambda b, s: (jnp.maximum(chunk(b, s) * r8 - 1, 0), C_XBC // 768)),
                pl.BlockSpec((8, 768), lambda b, s: (jnp.minimum((chunk(b, s) + 1) * r8, m // 8 - 1),
                                                     C_XBC // 768)),
                pl.BlockSpec((c_len, 256), lambda b, s: (chunk(b, s), C_SZ // 256)),
                pl.BlockSpec((c_len, 128), lambda b, s: (chunk(b, s), C_DT // 128)),
                pl.BlockSpec((None, 3, 768), lambda b, s: (layer, 0, 0)),
                pl.BlockSpec((1, 768), lambda b, s: (0, 0)),
                pl.BlockSpec((1, 128), lambda b, s: (0, 0)),
                pl.BlockSpec((1, 128), lambda b, s: (0, 0)),
                pl.BlockSpec((1, 256), lambda b, s: (0, 0)),
                pl.BlockSpec((1, 256), lambda b, s: (0, 0))]
    args = [p, p, p, p, p, conv_w, conv_b, dt_bias, a_log, d_lane, norm_g]
    if state0 is not None:
        in_specs.append(pl.BlockSpec((None, None, 2, 2, 128, SSD_STATE),
                                     lambda b, s: (b, layer, 0, 0, 0, 0)))
        args.append(state0)
    outs = [jax.ShapeDtypeStruct((m, 256), BF16)]
    out_specs = [pl.BlockSpec((c_len, 256), lambda b, s: (b * nc + jnp.maximum(s - nc, 0), 0))]
    if emit_state:
        outs.append(jax.ShapeDtypeStruct((nb, 2, 2, 128, SSD_STATE), F32))
        out_specs.append(pl.BlockSpec((None, 2, 2, 128, SSD_STATE), lambda b, s: (b, 0, 0, 0, 0)))
    scratch = [pltpu.VMEM((2, 128, SSD_STATE), F32), pltpu.VMEM((2, 128, SSD_STATE), F32),
               pltpu.VMEM((nc, 2, 128, SSD_STATE), F32)]
    return pl.pallas_call(
        functools.partial(_ssd_kernel, nc=nc, has_init=state0 is not None, emit_state=emit_state),
        out_shape=outs, grid=(nb, 2 * nc), in_specs=in_specs, out_specs=out_specs,
        scratch_shapes=scratch, compiler_params=_params(("arbitrary", "arbitrary")), name="ssd",
    )(*args)


def _layer_norm(x, g, b):
    mu = jnp.mean(x, axis=-1, keepdims=True)
    d = x - mu
    var = jnp.mean(d * d, axis=-1, keepdims=True)
    return d * lax.rsqrt(var + 1e-5) * g + b


FFN_CHUNK = 256
FFN_STREAMS = 1


def _outffn_kernel(x_ref, yr_ref, yg_ref, ys_ref, yw_ref, g1_ref, sh2_ref, sc2_ref, g2_ref,
                   wo_ref, wi_ref, wf_ref, l1g_ref, l1b_ref, l2g_ref, l2b_ref, o_ref):
    for r in range(FFN_STREAMS):
        rows = slice(TOKEN_TILE * r, TOKEN_TILE * (r + 1))
        mix = (_dot(yr_ref[rows, :], wo_ref[0:256, :]) + _dot(yg_ref[rows, :], wo_ref[256:512, :])
               + _dot(ys_ref[rows, :], wo_ref[512:768, :]) + _dot(yw_ref[rows, :], wo_ref[768:1024, :]))
        x1 = _layer_norm(ALPHA * x_ref[rows, :] + g1_ref[...] * mix, l1g_ref[...], l1b_ref[...])
        h2 = (x1 * (1.0 + sc2_ref[...]) + sh2_ref[...]).astype(BF16)
        acc = jnp.zeros(x1.shape, F32)
        for j in range(D_FF // FFN_CHUNK):
            cols = slice(FFN_CHUNK * j, FFN_CHUNK * (j + 1))
            gate = _dot(h2, wi_ref[:, cols])
            up = _dot(h2, wi_ref[:, D_FF + FFN_CHUNK * j:D_FF + FFN_CHUNK * (j + 1)])
            acc = acc + _dot((_silu(gate) * up).astype(BF16), wf_ref[cols, :])
        o_ref[rows, :] = _layer_norm(ALPHA * x1 + g2_ref[...] * acc, l2g_ref[...], l2b_ref[...])


def _outffn_call(x, ys, mod5, w_out_b, w_ffn_in_b, w_ffn_out_b, ln, layer, row0, tiles_per_row):
    m = x.shape[0]
    tm = TOKEN_TILE * FFN_STREAMS
    tiles_per_row = tiles_per_row // FFN_STREAMS
    row = lambda w: pl.BlockSpec((tm, w), lambda i: (i, 0))
    once = pl.Buffered(1)
    vec = pl.BlockSpec((None, 1, D_MODEL), lambda i: (layer, 0, 0))
    in_specs = [row(D_MODEL), row(256), row(256), row(256), row(256),
                _mod_spec(layer, row0, tiles_per_row, 2), _mod_spec(layer, row0, tiles_per_row, 3),
                _mod_spec(layer, row0, tiles_per_row, 4), _mod_spec(layer, row0, tiles_per_row, 5),
                pl.BlockSpec((None, D_MODEL, D_MODEL), lambda i: (layer, 0, 0), pipeline_mode=once),
                pl.BlockSpec((None, D_MODEL, 2 * D_FF), lambda i: (layer, 0, 0), pipeline_mode=once),
                pl.BlockSpec((None, D_FF, D_MODEL), lambda i: (layer, 0, 0), pipeline_mode=once),
                vec, vec, vec, vec]
    return pl.pallas_call(
        _outffn_kernel,
        out_shape=jax.ShapeDtypeStruct((m, D_MODEL), F32),
        grid=(m // tm,), in_specs=in_specs, out_specs=row(D_MODEL),
        compiler_params=_params(("arbitrary",)), name="out_ffn",
    )(x, *ys, mod5, mod5, mod5, mod5, w_out_b, w_ffn_in_b, w_ffn_out_b, *ln)


def _rope_tables(seq_len):
    t = jnp.arange(seq_len)
    rowp = (t // GRID_W).astype(F32)
    colp = (t % GRID_W).astype(F32)
    quarter = HEAD_DIM // 4
    inv_freq = ROPE_THETA ** (-jnp.arange(quarter, dtype=F32) / quarter)
    ang_r = rowp[:, None] * inv_freq[None, :]
    ang_c = colp[:, None] * inv_freq[None, :]
    cos = jnp.concatenate([jnp.cos(ang_r)] * 2 + [jnp.cos(ang_c)] * 2, axis=-1)
    sin = jnp.concatenate([-jnp.sin(ang_r), jnp.sin(ang_r), -jnp.sin(ang_c), jnp.sin(ang_c)], axis=-1)
    return jnp.tile(cos, (1, N_HEADS)), jnp.tile(sin, (1, N_HEADS))


def _expand_cache(c, fill):
    pad = jnp.full(c.shape[:2] + (HEAD_DIM,), fill, c.dtype)
    return jnp.concatenate([c[:, :, 0], pad, c[:, :, 1], pad], axis=-1).astype(BF16)


def _reorder_w_in(w):
    pad = jnp.zeros(w.shape[:-1] + (PROJ_W - C_DT - 8,), w.dtype)
    return jnp.concatenate([w[..., :1536], w[..., 1792:2560], w[..., 1536:1792],
                            w[..., 2568:3080], w[..., 2560:2568], pad], axis=-1).astype(BF16)


def kernel(x_prompt, x_sample, cache_gqa_k, cache_gqa_v, cache_swa_k, cache_swa_v, state_ret, state_ssd,
           c, c_ctx, ada_w, ada_b, w_in, ret_decay, ret_gn_g, gqa_q_norm, gqa_k_norm,
           ssd_conv_w, ssd_conv_b, ssd_dt_bias, ssd_a_log, ssd_d, ssd_norm_g, swa_sink, w_out,
           ln1_g, ln1_b, w_ffn_in, w_ffn_out, ln2_g, ln2_b):
    nb_c, len_c, _ = x_prompt.shape
    nb_l, len_l, _ = x_sample.shape
    depth = ada_w.shape[0]

    cond8 = jnp.zeros((8, D_MODEL), F32).at[0].set(c_ctx).at[1:1 + nb_l].set(c)
    mod5 = _ada_call(cond8, ada_w, ada_b)

    w_in_p = _reorder_w_in(w_in)
    w_out_b = w_out.astype(BF16)
    w_ffn_in_b = w_ffn_in.astype(BF16)
    w_ffn_out_b = w_ffn_out.astype(BF16)
    lane_head = jnp.arange(256) // HEAD_DIM
    seg = (lane_head[:, None] == lane_head[None, :]).astype(BF16)
    rope_tabs = _rope_tables(len_l)
    caches_g = (_expand_cache_layers(cache_gqa_k, 0.0), _expand_cache_layers(cache_gqa_v, 1.0))
    caches_w = (_expand_cache_layers(cache_swa_k, 0.0), _expand_cache_layers(cache_swa_v, 1.0))
    state_ssd_g = state_ssd.reshape(nb_l, depth, 2, 2, 128, SSD_STATE)

    def layer_consts(l):
        rd = ret_decay[l]
        return dict(
            rd_lane=jnp.repeat(rd, HEAD_DIM, axis=-1).reshape(2, 1, 256),
            rd_head=jnp.broadcast_to(rd[:, :, None, None], (2, N_HEADS, 1, CHUNK)),
            gn_g=ret_gn_g[l].reshape(1, 256),
            qn=jnp.tile(gqa_q_norm[l], N_HEADS).reshape(1, 256),
            kn=jnp.tile(gqa_k_norm[l], N_HEADS).reshape(1, 256),
            conv_b=ssd_conv_b[l].reshape(1, 768),
            dt_bias=jnp.zeros((1, 128), F32).at[0, :8].set(ssd_dt_bias[l].reshape(8)),
            a_log=jnp.zeros((1, 128), F32).at[0, :8].set(ssd_a_log[l].reshape(8)),
            d_lane=jnp.repeat(ssd_d[l], HEAD_DIM).reshape(1, 256),
            norm_g=ssd_norm_g[l].reshape(1, 256),
            sink=jnp.broadcast_to(swa_sink[l][:, None, None], (N_HEADS, 1, 128)),
        )

    ln = tuple(a.reshape(depth, 1, D_MODEL) for a in (ln1_g, ln1_b, ln2_g, ln2_b))

    def run_group(x, nb, seq_len, row0, latent):
        m = nb * seq_len
        tpr = (seq_len // TOKEN_TILE) if latent else (m // TOKEN_TILE)
        x = x.reshape(m, D_MODEL)
        extras = []
        for l in range(depth):
            k = layer_consts(l)
            p = _inproj_call(x, mod5, w_in_p, l, row0, tpr)
            prep = _prep_call(p, k["qn"], k["kn"], seg, rope_tabs if latent else None, seq_len)
            qg, kg, vg, qw, kw, vw = prep[:6]
            if latent:
                y_ret = _ret_call(p, nb, seq_len, k["rd_lane"], k["rd_head"], k["gn_g"], seg, l,
                                  state0=state_ret)[0]
                y_ssd = _ssd_call(p, nb, seq_len, ssd_conv_w, k["conv_b"], k["dt_bias"], k["a_log"],
                                  k["d_lane"], k["norm_g"], l, state0=state_ssd_g)[0]
                y_gqa = _attn_call(qg, kg, vg, nb, seq_len,
                                   cache=(caches_g[0][l], caches_g[1][l]))
                y_swa = _attn_call(qw, kw, vw, nb, seq_len,
                                   cache=(caches_w[0][l], caches_w[1][l]), sink=k["sink"], window=True)
            else:
                y_ret, st_ret = _ret_call(p, nb, seq_len, k["rd_lane"], k["rd_head"], k["gn_g"], seg, l,
                                          emit_state=True)
                y_ssd, st_ssd = _ssd_call(p, nb, seq_len, ssd_conv_w, k["conv_b"], k["dt_bias"],
                                          k["a_log"], k["d_lane"], k["norm_g"], l, emit_state=True)
                y_gqa = _attn_call(qg, kg, vg, nb, seq_len)
                y_swa = _attn_call(qw, kw, vw, nb, seq_len, sink=k["sink"])
                heads = lambda a: a.reshape(nb, seq_len, 2, HEAD_DIM)
                extras.append((heads(prep[6]), heads(p[:, C_GV:C_GV + 128]),
                               heads(p[:, C_WK:C_WK + 128]), heads(p[:, C_WV:C_WV + 128]),
                               st_ret, st_ssd.reshape(nb, 2, N_HEADS, HEAD_DIM, SSD_STATE)))
            x = _outffn_call(x, (y_ret, y_gqa, y_ssd, y_swa), mod5, w_out_b, w_ffn_in_b, w_ffn_out_b,
                             ln, l, row0, tpr)
        return x.reshape(nb, seq_len, D_MODEL), extras

    y_prompt, extras = run_group(x_prompt, nb_c, len_c, 0, False)
    y_sample, _ = run_group(x_sample, nb_l, len_l, 1, True)
    stacked = tuple(jnp.stack([e[i] for e in extras], axis=1) for i in range(6))
    return (y_prompt, y_sample) + stacked


def _expand_cache_layers(cache, fill):
    return [_expand_cache(cache[:, l], fill) for l in range(cache.shape[1])]
```

```python
import functools
import math

import jax
import jax.numpy as jnp
from jax import lax
from jax.experimental import pallas as pl
from jax.experimental.pallas import tpu as pltpu

F32 = jnp.float32
BF16 = jnp.bfloat16

D_MODEL = 1024
DEPTH = 4
HEAD_DIM = 64
N_HEADS = 4
GRID_W = 64
ROPE_THETA = 10000.0
SSD_STATE = 128
D_FF = 2816
WINDOW = 128
PAST_LEN = 256
ALPHA = (2.0 * DEPTH) ** 0.25
CHUNK = 256
TOKEN_TILE = 256
INPROJ_TILE = 512
VMEM_LIMIT = 56 * 1024 * 1024
LOG2E = math.log2(math.e)
QK_SCALE = HEAD_DIM ** -0.5 * LOG2E

W_RET, W_F32, W_GQA, W_SWA = 0, 768, 2176, 2688
PROJ_W = 3200
F_XBC, F_RG, F_SZ, F_DT = 0, 768, 1024, 1280
F_WIDTH = 1408


def _silu(x):
    return x / (1.0 + jnp.exp(-x))


def _dot(a, b):
    return jnp.dot(a, b, preferred_element_type=F32)


def _dot_nt(a, b):
    return lax.dot_general(a, b, (((1,), (1,)), ((), ())), preferred_element_type=F32)


def _dot_tn(a, b):
    return lax.dot_general(a, b, (((0,), (0,)), ((), ())), preferred_element_type=F32)


def _seg_sum(x, seg):
    hi = x.astype(BF16)
    lo = (x - hi.astype(F32)).astype(BF16)
    return _dot(hi, seg) + _dot(lo, seg)


def _params(sem):
    return pltpu.CompilerParams(dimension_semantics=sem, vmem_limit_bytes=VMEM_LIMIT)


def _ada_kernel(c_ref, w_ref, b_ref, o_ref):
    s = _silu(c_ref[...])
    o_ref[...] = _dot(s.astype(BF16), w_ref[...].astype(BF16)) + b_ref[...]


def _ada_call(cond8, ada_w, ada_b):
    tn = D_MODEL
    nt = 6
    out = pl.pallas_call(
        _ada_kernel,
        out_shape=jax.ShapeDtypeStruct((DEPTH, nt, 8, tn), F32),
        grid=(DEPTH, nt),
        in_specs=[pl.BlockSpec((8, D_MODEL), lambda l, j: (0, 0)),
                  pl.BlockSpec((None, D_MODEL, tn), lambda l, j: (l, 0, j)),
                  pl.BlockSpec((None, None, 1, tn), lambda l, j: (l, j, 0, 0))],
        out_specs=pl.BlockSpec((None, None, 8, tn), lambda l, j: (l, j, 0, 0)),
        compiler_params=_params(("arbitrary", "arbitrary")),
        name="ada_mod",
    )(cond8, ada_w, ada_b.reshape(DEPTH, nt, 1, tn))
    return jnp.transpose(out, (0, 2, 1, 3)).reshape(DEPTH, 8, nt, 1, tn)


def _mod_spec(layer, row0, tiles_per_row, k):
    return pl.BlockSpec((None, None, None, 1, D_MODEL),
                        lambda i: (layer, row0 + i // tiles_per_row, k, 0, 0))


def _rope(x, cos, sin_signed):
    w = x.shape[-1]
    lane = lax.broadcasted_iota(jnp.int32, x.shape, 1)
    partner = jnp.where((lane % 32) < 16, pltpu.roll(x, w - 16, 1), pltpu.roll(x, 16, 1))
    return x * cos + partner * sin_signed


def _expand_q(q):
    lane = lax.broadcasted_iota(jnp.int32, (q.shape[0], 128), 1)
    low = lane < HEAD_DIM
    blocks = []
    for j in range(2):
        pair = q[:, 128 * j:128 * (j + 1)]
        blocks.append(jnp.where(low, pair, 0.0))
        blocks.append(jnp.where(low, pltpu.roll(pair, HEAD_DIM, 1), 0.0))
    return jnp.concatenate(blocks, axis=1).astype(BF16)


def _expand_kv(k, fill):
    lane = lax.broadcasted_iota(jnp.int32, k.shape, 1)
    low = lane < HEAD_DIM
    return jnp.concatenate([jnp.where(low, k, fill),
                            jnp.where(low, pltpu.roll(k, HEAD_DIM, 1), fill)], axis=1).astype(BF16)


def _inproj_kernel(*refs, rope):
    x_ref, sh_ref, sc_ref, w_ref, qn, kn, seg = refs[:7]
    i = 7
    if rope:
        cos_ref, sin_ref = refs[7:9]
        i = 9
    f_o, r_o, qg_o, kg_o, vg_o, qw_o, kw_o, vw_o = refs[i:i + 8]
    h = (x_ref[...] * (1.0 + sc_ref[...]) + sh_ref[...]).astype(BF16)

    f_o[...] = _dot(h, w_ref[:, W_F32:W_F32 + F_WIDTH])
    ret = _dot(h, w_ref[:, W_RET:W_RET + 768])
    r_o[...] = jnp.concatenate([ret[:, 0:256], ret[:, 256:512] * HEAD_DIM ** -0.5, ret[:, 512:768]],
                               axis=1).astype(BF16)

    seg_m = seg[...]

    def rms(v, g, s):
        ms = _seg_sum(v * v, s) * (1.0 / HEAD_DIM)
        return v * lax.rsqrt(ms + 1e-6) * g

    gqa = _dot(h, w_ref[:, W_GQA:W_GQA + 512])
    swa = _dot(h, w_ref[:, W_SWA:W_SWA + 512])
    q_g = rms(gqa[:, 0:256], qn[...], seg_m)
    k_g = rms(gqa[:, 256:384], kn[...][:, :128], seg_m[:128, :128])
    q_w = swa[:, 0:256]
    k_w = swa[:, 256:384]
    if rope:
        cos = cos_ref[...]
        sin = sin_ref[...]
        q_g = _rope(q_g, cos, sin)
        q_w = _rope(q_w, cos, sin)
        k_g = _rope(k_g, cos[:, :128], sin[:, :128])
        k_w = _rope(k_w, cos[:, :128], sin[:, :128])
    else:
        refs[i + 8][...] = jnp.concatenate([k_g, gqa[:, 384:512], swa[:, 256:512]], axis=1)
    qg_o[...] = _expand_q(q_g * QK_SCALE)
    qw_o[...] = _expand_q(q_w * QK_SCALE)
    kg_o[...] = _expand_kv(k_g, 0.0)
    kw_o[...] = _expand_kv(k_w, 0.0)
    vg_o[...] = _expand_kv(gqa[:, 384:512], 1.0)
    vw_o[...] = _expand_kv(swa[:, 384:512], 1.0)


def _inproj_call(x, mod5, w_in_p, layer, row0, tiles_per_row, qn256, kn256, seg, rope_tabs, seq_len):
    m = x.shape[0]
    tm = INPROJ_TILE
    tiles_per_row = tiles_per_row * TOKEN_TILE // tm
    rope = rope_tabs is not None
    const = lambda shape: pl.BlockSpec(shape, lambda i: (0, 0))
    in_specs = [pl.BlockSpec((tm, D_MODEL), lambda i: (i, 0)),
                _mod_spec(layer, row0, tiles_per_row, 0),
                _mod_spec(layer, row0, tiles_per_row, 1),
                pl.BlockSpec((None, D_MODEL, PROJ_W), lambda i: (layer, 0, 0)),
                const((1, 256)), const((1, 256)), const((256, 256))]
    args = [x, mod5, mod5, w_in_p, qn256, kn256, seg]
    if rope:
        tpb = seq_len // tm
        tab = pl.BlockSpec((tm, 256), lambda i: (i % tpb, 0))
        in_specs += [tab, tab]
        args += list(rope_tabs)
    widths = [(F_WIDTH, F32), (768, BF16), (512, BF16), (256, BF16), (256, BF16),
              (512, BF16), (256, BF16), (256, BF16)]
    if not rope:
        widths.append((512, F32))
    return pl.pallas_call(
        functools.partial(_inproj_kernel, rope=rope),
        out_shape=[jax.ShapeDtypeStruct((m, w), dt) for w, dt in widths],
        grid=(m // tm,), in_specs=in_specs,
        out_specs=[pl.BlockSpec((tm, w), lambda i: (i, 0)) for w, _ in widths],
        compiler_params=_params(("arbitrary",)), name="in_proj",
    )(*args)


def _online(m, acc, s_blocks, v_blocks):
    m_new = m
    for s in s_blocks:
        m_new = jnp.maximum(m_new, jnp.max(s, axis=-1, keepdims=True))
    acc = jnp.exp2(m - m_new) * acc
    for s, v in zip(s_blocks, v_blocks):
        acc = acc + _dot(jnp.exp2(s - m_new).astype(BF16), v)
    return m_new, acc


def _attn_kernel(*refs, seq_len, tq, kv_chunk, window, use_cache, use_sink):
    q_ref, k_ref, v_ref = refs[:3]
    i = 3
    if use_cache:
        kc_ref, vc_ref = refs[3:5]
        i = 5
    if use_sink:
        sink_ref = refs[i]
        i += 1
    o_ref = refs[i]
    qi = pl.program_id(1)
    rows = 2 * tq
    q2 = [jnp.concatenate([q_ref[:, 256 * j:256 * j + 128], q_ref[:, 256 * j + 128:256 * (j + 1)]], axis=0)
          for j in range(2)]

    def update(carry, kv_blocks, masks):
        out = []
        for j in range(2):
            lanes = slice(128 * j, 128 * (j + 1))
            s_blocks = []
            for (k_blk, _), mask in zip(kv_blocks, masks):
                s = _dot_nt(q2[j], k_blk[:, lanes])
                s_blocks.append(s if mask is None else jnp.where(mask, s, -jnp.inf))
            out.append(_online(*carry[j], s_blocks, [v_blk[:, lanes] for _, v_blk in kv_blocks]))
        return tuple(out)

    if use_sink:
        upper = lax.broadcasted_iota(jnp.int32, (rows, 128), 1) >= HEAD_DIM
        carry = tuple((jnp.max(sink_ref[j], axis=-1, keepdims=True) * LOG2E, jnp.where(upper, 1.0, 0.0))
                      for j in range(2))
    else:
        carry = tuple((jnp.full((rows, 1), -jnp.inf, F32), jnp.zeros((rows, 128), F32)) for _ in range(2))
    if window:
        span = tq + 2 * WINDOW
        start = pl.multiple_of(jnp.clip(qi * tq - WINDOW, 0, seq_len - span), 128)
        qpos = qi * tq + lax.broadcasted_iota(jnp.int32, (rows, span), 0) % tq
        kpos = start + lax.broadcasted_iota(jnp.int32, (rows, span), 1)
        blocks = [(k_ref[pl.ds(start, span), :], v_ref[pl.ds(start, span), :])]
        masks = [jnp.abs(qpos - kpos) <= WINDOW]
        if use_cache:
            blocks.append((kc_ref[...], vc_ref[...]))
            masks.append(None)
        carry = update(carry, blocks, masks)
    else:
        for c in range(seq_len // kv_chunk):
            blk = slice(c * kv_chunk, (c + 1) * kv_chunk)
            carry = update(carry, [(k_ref[blk, :], v_ref[blk, :])], [None])
        if use_cache:
            carry = update(carry, [(kc_ref[...], vc_ref[...])], [None])

    low = lax.broadcasted_iota(jnp.int32, (tq, 128), 1) < HEAD_DIM
    for j in range(2):
        acc = carry[j][1]
        o = acc / pltpu.roll(acc, HEAD_DIM, 1)
        o_ref[:, 128 * j:128 * (j + 1)] = jnp.where(low, o[:tq], pltpu.roll(o[tq:], HEAD_DIM, 1)).astype(o_ref.dtype)


def _attn_call(q, k, v, nb, seq_len, cache=None, sink=None, window=False):
    tq = TOKEN_TILE
    nq = seq_len // tq
    kv_chunk = min(1024, seq_len)
    in_specs = [pl.BlockSpec((tq, 512), lambda b, i: (b * nq + i, 0)),
                pl.BlockSpec((seq_len, 256), lambda b, i: (b, 0)),
                pl.BlockSpec((seq_len, 256), lambda b, i: (b, 0))]
    args = [q, k, v]
    if cache is not None:
        cspec = pl.BlockSpec((None, PAST_LEN, 256), lambda b, i: (b, 0, 0))
        in_specs += [cspec, cspec]
        args += list(cache)
    if sink is not None:
        in_specs.append(pl.BlockSpec((2, 2 * tq, 128), lambda b, i: (0, 0, 0)))
        args.append(sink)
    return pl.pallas_call(
        functools.partial(_attn_kernel, seq_len=seq_len, tq=tq, kv_chunk=kv_chunk, window=window,
                          use_cache=cache is not None, use_sink=sink is not None),
        out_shape=jax.ShapeDtypeStruct((nb * seq_len, 256), BF16),
        grid=(nb, nq), in_specs=in_specs,
        out_specs=pl.BlockSpec((tq, 256), lambda b, i: (b * nq + i, 0)),
        compiler_params=_params(("arbitrary", "arbitrary")), name="attention",
    )(*args)


def _chunk_of(s, nc):
    return jnp.where(s < nc, nc - 1 - s, s - nc)


def _log_gamma(rd):
    return jnp.log1p(-jnp.exp(rd))


def _ret_kernel(*refs, nc, has_init, emit_state):
    q_ref, k_ref, v_ref, g_ref, rdl_ref, rdh_ref, gn_ref, seg_ref = refs[:8]
    i = 8
    if has_init:
        s0_ref = refs[i]
        i += 1
    y_ref = refs[i]
    i += 1
    if emit_state:
        st_ref = refs[i]
        i += 1
    m_sc, dec_sc, qk_sc, sf_sc, sb_sc, sball_sc = refs[i:i + 6]
    c_len = CHUNK
    b = pl.program_id(0)
    s = pl.program_id(1)

    @pl.when(jnp.logical_and(b == 0, s == 0))
    def _():
        ri = lax.broadcasted_iota(jnp.int32, (c_len, c_len), 0)
        ci = lax.broadcasted_iota(jnp.int32, (c_len, c_len), 1)
        d = (ri - ci).astype(F32)
        for h in range(N_HEADS):
            lgf = _log_gamma(jnp.broadcast_to(rdh_ref[0, h], (c_len, c_len)))
            lgb = _log_gamma(jnp.broadcast_to(rdh_ref[1, h], (c_len, c_len)))
            m_sc[h] = jnp.where(d > 0, jnp.exp(d * lgf), jnp.where(d < 0, jnp.exp(-d * lgb), 2.0))
        for dr in range(2):
            rows = [jnp.exp(c_len * _log_gamma(jnp.broadcast_to(rdh_ref[dr, h], (HEAD_DIM, c_len))))
                    for h in range(N_HEADS)]
            dec_sc[dr] = jnp.concatenate(rows, axis=0)
        a = lax.broadcasted_iota(jnp.int32, (c_len, 256), 0).astype(F32)
        lgf = _log_gamma(rdl_ref[0])
        lgb = _log_gamma(rdl_ref[1])
        qk_sc[0] = jnp.exp((a + 1.0) * lgf)
        qk_sc[1] = jnp.exp((c_len - 1.0 - a) * lgf)
        qk_sc[2] = jnp.exp((c_len - a) * lgb)
        qk_sc[3] = jnp.exp(a * lgb)

    @pl.when(s == 0)
    def _():
        sf_sc[...] = jnp.zeros_like(sf_sc)
        sb_sc[...] = jnp.zeros_like(sb_sc)
        if has_init:
            for h in range(N_HEADS):
                blk = slice(HEAD_DIM * h, HEAD_DIM * (h + 1))
                sf_sc[blk, blk] = s0_ref[0, h]
                sb_sc[blk, blk] = s0_ref[1, h]

    ri = lax.broadcasted_iota(jnp.int32, (256, 256), 0) // HEAD_DIM
    ci = lax.broadcasted_iota(jnp.int32, (256, 256), 1) // HEAD_DIM
    diag = ri == ci
    kb = k_ref[...]
    ks = kb.astype(F32)
    vb = v_ref[...]

    @pl.when(s < nc)
    def _():
        c = nc - 1 - s
        sball_sc[c] = sb_sc[...]
        upd = _dot_tn((ks * qk_sc[3]).astype(BF16), vb)
        sb_sc[...] = sb_sc[...] * dec_sc[1] + jnp.where(diag, upd, 0.0)
        if emit_state:
            @pl.when(s == nc - 1)
            def _():
                for h in range(N_HEADS):
                    blk = slice(HEAD_DIM * h, HEAD_DIM * (h + 1))
                    st_ref[1, h] = sb_sc[blk, blk]

    @pl.when(s >= nc)
    def _():
        c = s - nc
        qb = q_ref[...]
        q = qb.astype(F32)
        lane_head = lax.broadcasted_iota(jnp.int32, (c_len, 256), 1) // HEAD_DIM
        o = (_dot(qb, sf_sc[...].astype(BF16)) * qk_sc[0]
             + _dot(qb, sball_sc[c].astype(BF16)) * qk_sc[2])
        for h in range(N_HEADS):
            mine = lane_head == h
            sc = _dot_nt(jnp.where(mine, q, 0.0).astype(BF16), kb)
            pv = _dot((sc * m_sc[h]).astype(BF16), vb)
            o = o + jnp.where(mine, pv, 0.0)
        seg = seg_ref[...]
        mu = _seg_sum(o, seg) * (1.0 / HEAD_DIM)
        dlt = o - mu
        var = _seg_sum(dlt * dlt, seg) * (1.0 / HEAD_DIM)
        on = dlt * lax.rsqrt(var + 1e-5) * gn_ref[...]
        y_ref[...] = (on * _silu(g_ref[...])).astype(y_ref.dtype)
        upd = _dot_tn((ks * qk_sc[1]).astype(BF16), vb)
        sf_sc[...] = sf_sc[...] * dec_sc[0] + jnp.where(diag, upd, 0.0)
        if emit_state:
            @pl.when(s == 2 * nc - 1)
            def _():
                for h in range(N_HEADS):
                    blk = slice(HEAD_DIM * h, HEAD_DIM * (h + 1))
                    st_ref[0, h] = sf_sc[blk, blk]


def _ret_call(rqkv, pf, nb, seq_len, rd_lane, rd_head, gn_g, seg, layer, state0=None, emit_state=False):
    c_len = CHUNK
    nc = seq_len // c_len

    def col(start):
        return pl.BlockSpec((c_len, 256), lambda b, s: (b * nc + _chunk_of(s, nc), start // 256))

    in_specs = [col(0), col(256), col(512), col(F_RG),
                pl.BlockSpec((2, 1, 256), lambda b, s: (0, 0, 0)),
                pl.BlockSpec((2, N_HEADS, 1, c_len), lambda b, s: (0, 0, 0, 0)),
                pl.BlockSpec((1, 256), lambda b, s: (0, 0)),
                pl.BlockSpec((256, 256), lambda b, s: (0, 0))]
    args = [rqkv, rqkv, rqkv, pf, rd_lane, rd_head, gn_g, seg]
    if state0 is not None:
        in_specs.append(pl.BlockSpec((None, None, 2, N_HEADS, HEAD_DIM, HEAD_DIM),
                                     lambda b, s: (b, layer, 0, 0, 0, 0)))
        args.append(state0)
    outs = [jax.ShapeDtypeStruct((nb * seq_len, 256), BF16)]
    out_specs = [pl.BlockSpec((c_len, 256), lambda b, s: (b * nc + jnp.maximum(s - nc, 0), 0))]
    if emit_state:
        outs.append(jax.ShapeDtypeStruct((nb, 2, N_HEADS, HEAD_DIM, HEAD_DIM), F32))
        out_specs.append(pl.BlockSpec((None, 2, N_HEADS, HEAD_DIM, HEAD_DIM),
                                      lambda b, s: (b, 0, 0, 0, 0)))
    scratch = [pltpu.VMEM((N_HEADS, c_len, c_len), F32), pltpu.VMEM((2, 256, 256), F32),
               pltpu.VMEM((4, c_len, 256), F32), pltpu.VMEM((256, 256), F32),
               pltpu.VMEM((256, 256), F32), pltpu.VMEM((nc, 256, 256), F32)]
    return pl.pallas_call(
        functools.partial(_ret_kernel, nc=nc, has_init=state0 is not None, emit_state=emit_state),
        out_shape=outs, grid=(nb, 2 * nc), in_specs=in_specs, out_specs=out_specs,
        scratch_shapes=scratch, compiler_params=_params(("arbitrary", "arbitrary")),
        name="retention",
    )(*args)


def _cumsum_rows(x, reverse):
    n = x.shape[0]
    row = lax.broadcasted_iota(jnp.int32, x.shape, 0)
    sh = 1
    while sh < n:
        if reverse:
            x = x + jnp.where(row < n - sh, pltpu.roll(x, n - sh, 0), 0.0)
        else:
            x = x + jnp.where(row >= sh, pltpu.roll(x, sh, 0), 0.0)
        sh *= 2
    return x


def _ssd_kernel(*refs, nc, has_init, emit_state):
    xbc_ref, prev_ref, next_ref, z_ref, dt_ref, cw_ref, cb_ref, dtb_ref, al_ref, dl_ref, ng_ref = refs[:11]
    i = 11
    if has_init:
        h0_ref = refs[i]
        i += 1
    y_ref = refs[i]
    i += 1
    if emit_state:
        st_ref = refs[i]
        i += 1
    hf_sc, hb_sc, hball_sc, xs_sc, bm_sc, cm_sc, dt_sc, cum_sc, rev_sc = refs[i:i + 9]
    c_len = CHUNK
    s = pl.program_id(1)
    c = _chunk_of(s, nc)

    @pl.when(s == 0)
    def _():
        if has_init:
            hf_sc[...] = h0_ref[0]
            hb_sc[...] = h0_ref[1]
        else:
            hf_sc[...] = jnp.zeros_like(hf_sc)
            hb_sc[...] = jnp.zeros_like(hb_sc)

    low = lax.broadcasted_iota(jnp.int32, (c_len, 128), 1) < HEAD_DIM
    low_rows = lax.broadcasted_iota(jnp.int32, (128, 128), 0) < HEAD_DIM

    def col(a, lane):
        return jnp.broadcast_to(a[:, lane:lane + 1], (c_len, c_len))

    def pair(a0, a1):
        return jnp.where(low, a0[:, :128], a1[:, :128])

    def state_update(h_sc, g, xs, bm, dt, cols, tot_row, dt_lanes):
        w = [jnp.exp(tot_row[r] - cols[r]) * col(dt, dt_lanes[r]) for r in range(2)]
        xw = (xs[:, 128 * g:128 * (g + 1)] * pair(w[0], w[1])).astype(BF16)
        dec = jnp.where(low_rows, jnp.broadcast_to(jnp.exp(tot_row[0][:, :128]), (128, 128)),
                        jnp.broadcast_to(jnp.exp(tot_row[1][:, :128]), (128, 128)))
        h_sc[g] = h_sc[g] * dec + _dot_tn(xw, bm[:, 128 * g:128 * (g + 1)])

    @pl.when(s < nc)
    def _():
        xbc = xbc_ref[...]
        row = lax.broadcasted_iota(jnp.int32, xbc.shape, 0)
        before = jnp.where(c > 0, prev_ref[7:8, :], 0.0)
        after = jnp.where(c < nc - 1, next_ref[0:1, :], 0.0)
        x_prev = jnp.where(row == 0, before, pltpu.roll(xbc, 1, 0))
        x_next = jnp.where(row == c_len - 1, after, pltpu.roll(xbc, c_len - 1, 0))
        conv = _silu(cw_ref[0:1, :] * x_prev + cw_ref[1:2, :] * xbc + cw_ref[2:3, :] * x_next + cb_ref[...])
        xs = conv[:, 0:256]
        bm = conv[:, 256:512].astype(BF16)
        pre = dt_ref[...] + dtb_ref[...]
        dt = jnp.maximum(pre, 0.0) + jnp.log1p(jnp.exp(-jnp.abs(pre)))
        dta = dt * (-jnp.exp(al_ref[...]))
        rev = _cumsum_rows(dta, reverse=True)
        xs_sc[c] = xs
        bm_sc[c] = bm
        cm_sc[c] = conv[:, 512:768].astype(BF16)
        dt_sc[c] = dt
        cum_sc[c] = _cumsum_rows(dta, reverse=False)
        rev_sc[c] = rev
        hball_sc[c] = hb_sc[...]
        for g in range(2):
            cols = [col(rev, 4 + 2 * g + r) for r in range(2)]
            tot = [cl[0:1, :] for cl in cols]
            state_update(hb_sc, g, xs, bm, dt, cols, tot, [4 + 2 * g, 5 + 2 * g])
        if emit_state:
            @pl.when(s == nc - 1)
            def _():
                st_ref[1] = hb_sc[...]

    @pl.when(s >= nc)
    def _():
        xs = xs_sc[c]
        bm = bm_sc[c]
        cm = cm_sc[c]
        dt = dt_sc[c]
        cum = cum_sc[c]
        rev = rev_sc[c]
        cum_t = cum.T
        rev_t = rev.T
        dt_t = dt.T
        ri = lax.broadcasted_iota(jnp.int32, (c_len, c_len), 0)
        ci = lax.broadcasted_iota(jnp.int32, (c_len, c_len), 1)
        below = ri > ci
        on_diag = ri == ci
        lane_head = lax.broadcasted_iota(jnp.int32, (c_len, 256), 1) // HEAD_DIM
        xs_b = xs.astype(BF16)
        y = jnp.zeros((c_len, 256), F32)
        inter = []
        for g in range(2):
            gl = slice(128 * g, 128 * (g + 1))
            gmat = _dot_nt(cm[:, gl], bm[:, gl])
            cols_f, cols_b = [], []
            for r in range(2):
                h = 2 * g + r
                cf = col(cum, h)
                cb = col(rev, 4 + h)
                cols_f.append(cf)
                cols_b.append(cb)
                dt_f = dt_t[h:h + 1, :]
                dt_b = dt_t[4 + h:5 + h, :]
                decay = jnp.exp(jnp.where(below, cf - cum_t[h:h + 1, :], cb - rev_t[4 + h:5 + h, :]))
                w = decay * (jnp.where(below, dt_f, dt_b) + jnp.where(on_diag, dt_f, 0.0))
                yh = _dot((gmat * w).astype(BF16), xs_b)
                y = y + jnp.where(lane_head == h, yh, 0.0)
            yf = _dot_nt(cm[:, gl], hf_sc[g].astype(BF16))
            yb = _dot_nt(cm[:, gl], hball_sc[c, g].astype(BF16))
            inter.append(yf * pair(jnp.exp(cols_f[0]), jnp.exp(cols_f[1]))
                         + yb * pair(jnp.exp(cols_b[0]), jnp.exp(cols_b[1])))
            tot = [cl[c_len - 1:c_len, :] for cl in cols_f]
            state_update(hf_sc, g, xs, bm, dt, cols_f, tot, [2 * g, 2 * g + 1])
        y = y + jnp.concatenate(inter, axis=1) + xs * dl_ref[...]
        y = y * _silu(z_ref[...])
        ms = jnp.mean(y * y, axis=-1, keepdims=True)
        y_ref[...] = (y * lax.rsqrt(ms + 1e-6) * ng_ref[...]).astype(y_ref.dtype)
        if emit_state:
            @pl.when(s == 2 * nc - 1)
            def _():
                st_ref[0] = hf_sc[...]


def _ssd_call(p, nb, seq_len, conv_w, conv_b, dt_bias, a_log, d_lane, norm_g, layer,
              state0=None, emit_state=False):
    c_len = CHUNK
    nc = seq_len // c_len
    m = nb * seq_len
    r8 = c_len // 8
    chunk = lambda b, s: b * nc + jnp.maximum(nc - 1 - s, 0)
    in_specs = [pl.BlockSpec((c_len, 768), lambda b, s: (chunk(b, s), F_XBC // 768)),
                pl.BlockSpec((8, 768), lambda b, s: (jnp.maximum(chunk(b, s) * r8 - 1, 0), F_XBC // 768)),
                pl.BlockSpec((8, 768), lambda b, s: (jnp.minimum((chunk(b, s) + 1) * r8, m // 8 - 1),
                                                     F_XBC // 768)),
                pl.BlockSpec((c_len, 256), lambda b, s: (b * nc + jnp.maximum(s - nc, 0), F_SZ // 256)),
                pl.BlockSpec((c_len, 128), lambda b, s: (chunk(b, s), F_DT // 128)),
                pl.BlockSpec((None, 3, 768), lambda b, s: (layer, 0, 0)),
                pl.BlockSpec((1, 768), lambda b, s: (0, 0)),
                pl.BlockSpec((1, 128), lambda b, s: (0, 0)),
                pl.BlockSpec((1, 128), lambda b, s: (0, 0)),
                pl.BlockSpec((1, 256), lambda b, s: (0, 0)),
                pl.BlockSpec((1, 256), lambda b, s: (0, 0))]
    args = [p, p, p, p, p, conv_w, conv_b, dt_bias, a_log, d_lane, norm_g]
    if state0 is not None:
        in_specs.append(pl.BlockSpec((None, None, 2, 2, 128, SSD_STATE),
                                     lambda b, s: (b, layer, 0, 0, 0, 0)))
        args.append(state0)
    outs = [jax.ShapeDtypeStruct((m, 256), BF16)]
    out_specs = [pl.BlockSpec((c_len, 256), lambda b, s: (b * nc + jnp.maximum(s - nc, 0), 0))]
    if emit_state:
        outs.append(jax.ShapeDtypeStruct((nb, 2, 2, 128, SSD_STATE), F32))
        out_specs.append(pl.BlockSpec((None, 2, 2, 128, SSD_STATE), lambda b, s: (b, 0, 0, 0, 0)))
    scratch = [pltpu.VMEM((2, 128, SSD_STATE), F32), pltpu.VMEM((2, 128, SSD_STATE), F32),
               pltpu.VMEM((nc, 2, 128, SSD_STATE), F32),
               pltpu.VMEM((nc, c_len, 256), F32), pltpu.VMEM((nc, c_len, 256), BF16),
               pltpu.VMEM((nc, c_len, 256), BF16), pltpu.VMEM((nc, c_len, 128), F32),
               pltpu.VMEM((nc, c_len, 128), F32), pltpu.VMEM((nc, c_len, 128), F32)]
    return pl.pallas_call(
        functools.partial(_ssd_kernel, nc=nc, has_init=state0 is not None, emit_state=emit_state),
        out_shape=outs, grid=(nb, 2 * nc), in_specs=in_specs, out_specs=out_specs,
        scratch_shapes=scratch, compiler_params=_params(("arbitrary", "arbitrary")), name="ssd",
    )(*args)


def _layer_norm(x, g, b):
    mu = jnp.mean(x, axis=-1, keepdims=True)
    d = x - mu
    var = jnp.mean(d * d, axis=-1, keepdims=True)
    return d * lax.rsqrt(var + 1e-5) * g + b


FFN_CHUNK = 256
FFN_STREAMS = 1
FFN_ROWS = 512


def _outffn_kernel(x_ref, yr_ref, yg_ref, ys_ref, yw_ref, g1_ref, sh2_ref, sc2_ref, g2_ref,
                   wo_ref, wi_ref, wf_ref, l1g_ref, l1b_ref, l2g_ref, l2b_ref, o_ref):
    n_chunks = D_FF // FFN_CHUNK
    half = n_chunks // 2

    def head(r):
        rows = slice(FFN_ROWS * r, FFN_ROWS * (r + 1))
        mix = (_dot(yr_ref[rows, :], wo_ref[0:256, :]) + _dot(yg_ref[rows, :], wo_ref[256:512, :])
               + _dot(ys_ref[rows, :], wo_ref[512:768, :]) + _dot(yw_ref[rows, :], wo_ref[768:1024, :]))
        x1 = _layer_norm(ALPHA * x_ref[rows, :] + g1_ref[...] * mix, l1g_ref[...], l1b_ref[...])
        return x1, (x1 * (1.0 + sc2_ref[...]) + sh2_ref[...]).astype(BF16)

    def ffn(h2, acc, chunks):
        for j in chunks:
            cols = slice(FFN_CHUNK * j, FFN_CHUNK * (j + 1))
            gate = _dot(h2, wi_ref[:, cols])
            up = _dot(h2, wi_ref[:, D_FF + FFN_CHUNK * j:D_FF + FFN_CHUNK * (j + 1)])
            d = _dot((_silu(gate) * up).astype(BF16), wf_ref[cols, :])
            acc = d if acc is None else acc + d
        return acc

    def tail(r, x1, acc):
        rows = slice(FFN_ROWS * r, FFN_ROWS * (r + 1))
        o_ref[rows, :] = _layer_norm(ALPHA * x1 + g2_ref[...] * acc, l2g_ref[...], l2b_ref[...])

    state = [None] * FFN_STREAMS
    x1, h2 = head(0)
    state[0] = (x1, h2, ffn(h2, None, range(0, half)))
    for r in range(FFN_STREAMS):
        x1, h2, acc = state[r]
        if r + 1 < FFN_STREAMS:
            nx1, nh2 = head(r + 1)
        acc = ffn(h2, acc, range(half, n_chunks))
        if r + 1 < FFN_STREAMS:
            state[r + 1] = (nx1, nh2, ffn(nh2, None, range(0, half)))
        tail(r, x1, acc)


def _outffn_call(x, ys, mod5, w_out_b, w_ffn_in_b, w_ffn_out_b, ln, layer, row0, tiles_per_row):
    m = x.shape[0]
    tm = FFN_ROWS * FFN_STREAMS
    tiles_per_row = tiles_per_row * TOKEN_TILE // tm
    row = lambda w: pl.BlockSpec((tm, w), lambda i: (i, 0))
    once = pl.Buffered(1)
    vec = pl.BlockSpec((None, 1, D_MODEL), lambda i: (layer, 0, 0))
    in_specs = [row(D_MODEL), row(256), row(256), row(256), row(256),
                _mod_spec(layer, row0, tiles_per_row, 2), _mod_spec(layer, row0, tiles_per_row, 3),
                _mod_spec(layer, row0, tiles_per_row, 4), _mod_spec(layer, row0, tiles_per_row, 5),
                pl.BlockSpec((None, D_MODEL, D_MODEL), lambda i: (layer, 0, 0), pipeline_mode=once),
                pl.BlockSpec((None, D_MODEL, 2 * D_FF), lambda i: (layer, 0, 0), pipeline_mode=once),
                pl.BlockSpec((None, D_FF, D_MODEL), lambda i: (layer, 0, 0), pipeline_mode=once),
                vec, vec, vec, vec]
    return pl.pallas_call(
        _outffn_kernel,
        out_shape=jax.ShapeDtypeStruct((m, D_MODEL), F32),
        grid=(m // tm,), in_specs=in_specs, out_specs=row(D_MODEL),
        compiler_params=_params(("arbitrary",)), name="out_ffn",
    )(x, *ys, mod5, mod5, mod5, mod5, w_out_b, w_ffn_in_b, w_ffn_out_b, *ln)


def _rope_tables(seq_len):
    t = jnp.arange(seq_len)
    rowp = (t // GRID_W).astype(F32)
    colp = (t % GRID_W).astype(F32)
    quarter = HEAD_DIM // 4
    inv_freq = ROPE_THETA ** (-jnp.arange(quarter, dtype=F32) / quarter)
    ang_r = rowp[:, None] * inv_freq[None, :]
    ang_c = colp[:, None] * inv_freq[None, :]
    cos = jnp.concatenate([jnp.cos(ang_r)] * 2 + [jnp.cos(ang_c)] * 2, axis=-1)
    sin = jnp.concatenate([-jnp.sin(ang_r), jnp.sin(ang_r), -jnp.sin(ang_c), jnp.sin(ang_c)], axis=-1)
    return jnp.tile(cos, (1, N_HEADS)), jnp.tile(sin, (1, N_HEADS))


def _expand_cache(c, fill):
    pad = jnp.full(c.shape[:2] + (HEAD_DIM,), fill, c.dtype)
    return jnp.concatenate([c[:, :, 0], pad, c[:, :, 1], pad], axis=-1).astype(BF16)


def _reorder_w_in(w):
    pad = jnp.zeros(w.shape[:-1] + (F_WIDTH - F_DT - 8,), w.dtype)
    return jnp.concatenate([w[..., 0:768],
                            w[..., 1792:2560], w[..., 768:1024], w[..., 1536:1792], w[..., 2560:2568], pad,
                            w[..., 1024:1536], w[..., 2568:3080]], axis=-1).astype(BF16)


def kernel(x_prompt, x_sample, cache_gqa_k, cache_gqa_v, cache_swa_k, cache_swa_v, state_ret, state_ssd,
           c, c_ctx, ada_w, ada_b, w_in, ret_decay, ret_gn_g, gqa_q_norm, gqa_k_norm,
           ssd_conv_w, ssd_conv_b, ssd_dt_bias, ssd_a_log, ssd_d, ssd_norm_g, swa_sink, w_out,
           ln1_g, ln1_b, w_ffn_in, w_ffn_out, ln2_g, ln2_b):
    nb_c, len_c, _ = x_prompt.shape
    nb_l, len_l, _ = x_sample.shape
    depth = ada_w.shape[0]

    cond8 = jnp.zeros((8, D_MODEL), F32).at[0].set(c_ctx).at[1:1 + nb_l].set(c)
    mod5 = _ada_call(cond8, ada_w, ada_b)

    w_in_p = _reorder_w_in(w_in)
    w_out_b = w_out.astype(BF16)
    w_ffn_in_b = w_ffn_in.astype(BF16)
    w_ffn_out_b = w_ffn_out.astype(BF16)
    lane_head = jnp.arange(256) // HEAD_DIM
    seg = (lane_head[:, None] == lane_head[None, :]).astype(BF16)
    rope_tabs = _rope_tables(len_l)
    caches_g = (_expand_cache_layers(cache_gqa_k, 0.0), _expand_cache_layers(cache_gqa_v, 1.0))
    caches_w = (_expand_cache_layers(cache_swa_k, 0.0), _expand_cache_layers(cache_swa_v, 1.0))
    state_ssd_g = state_ssd.reshape(nb_l, depth, 2, 2, 128, SSD_STATE)

    def layer_consts(l):
        rd = ret_decay[l]
        return dict(
            rd_lane=jnp.repeat(rd, HEAD_DIM, axis=-1).reshape(2, 1, 256),
            rd_head=jnp.broadcast_to(rd[:, :, None, None], (2, N_HEADS, 1, CHUNK)),
            gn_g=ret_gn_g[l].reshape(1, 256),
            qn=jnp.tile(gqa_q_norm[l], N_HEADS).reshape(1, 256),
            kn=jnp.tile(gqa_k_norm[l], N_HEADS).reshape(1, 256),
            conv_b=ssd_conv_b[l].reshape(1, 768),
            dt_bias=jnp.zeros((1, 128), F32).at[0, :8].set(ssd_dt_bias[l].reshape(8)),
            a_log=jnp.zeros((1, 128), F32).at[0, :8].set(ssd_a_log[l].reshape(8)),
            d_lane=jnp.repeat(ssd_d[l], HEAD_DIM).reshape(1, 256),
            norm_g=ssd_norm_g[l].reshape(1, 256),
            sink=jnp.broadcast_to(swa_sink[l].reshape(2, 2, 1, 1), (2, 2, TOKEN_TILE, 128)
                                  ).reshape(2, 2 * TOKEN_TILE, 128),
        )

    ln = tuple(a.reshape(depth, 1, D_MODEL) for a in (ln1_g, ln1_b, ln2_g, ln2_b))

    def run_group(x, nb, seq_len, row0, latent):
        m = nb * seq_len
        tpr = (seq_len // TOKEN_TILE) if latent else (m // TOKEN_TILE)
        x = x.reshape(m, D_MODEL)
        extras = []
        for l in range(depth):
            k = layer_consts(l)
            proj = _inproj_call(x, mod5, w_in_p, l, row0, tpr, k["qn"], k["kn"], seg,
                                rope_tabs if latent else None, seq_len)
            pf, rqkv, qg, kg, vg, qw, kw, vw = proj[:8]
            if latent:
                y_ret = _ret_call(rqkv, pf, nb, seq_len, k["rd_lane"], k["rd_head"], k["gn_g"], seg, l,
                                  state0=state_ret)[0]
                y_ssd = _ssd_call(pf, nb, seq_len, ssd_conv_w, k["conv_b"], k["dt_bias"], k["a_log"],
                                  k["d_lane"], k["norm_g"], l, state0=state_ssd_g)[0]
                y_gqa = _attn_call(qg, kg, vg, nb, seq_len,
                                   cache=(caches_g[0][l], caches_g[1][l]))
                y_swa = _attn_call(qw, kw, vw, nb, seq_len,
                                   cache=(caches_w[0][l], caches_w[1][l]), sink=k["sink"], window=True)
            else:
                y_ret, st_ret = _ret_call(rqkv, pf, nb, seq_len, k["rd_lane"], k["rd_head"], k["gn_g"], seg, l,
                                          emit_state=True)
                y_ssd, st_ssd = _ssd_call(pf, nb, seq_len, ssd_conv_w, k["conv_b"], k["dt_bias"],
                                          k["a_log"], k["d_lane"], k["norm_g"], l, emit_state=True)
                y_gqa = _attn_call(qg, kg, vg, nb, seq_len)
                y_swa = _attn_call(qw, kw, vw, nb, seq_len, sink=k["sink"])
                new_kv = proj[8].reshape(nb, seq_len, 4, 2, HEAD_DIM)
                extras.append(tuple(new_kv[:, :, i] for i in range(4))
                              + (st_ret, st_ssd.reshape(nb, 2, N_HEADS, HEAD_DIM, SSD_STATE)))
            x = _outffn_call(x, (y_ret, y_gqa, y_ssd, y_swa), mod5, w_out_b, w_ffn_in_b, w_ffn_out_b,
                             ln, l, row0, tpr)
        return x.reshape(nb, seq_len, D_MODEL), extras

    y_prompt, extras = run_group(x_prompt, nb_c, len_c, 0, False)
    y_sample, _ = run_group(x_sample, nb_l, len_l, 1, True)
    stacked = tuple(jnp.stack([e[i] for e in extras], axis=1) for i in range(6))
    return (y_prompt, y_sample) + stacked


def _expand_cache_layers(cache, fill):
    return [_expand_cache(cache[:, l], fill) for l in range(cache.shape[1])]
```

```python
import functools
import math

import jax
import jax.numpy as jnp
from jax import lax
from jax.experimental import pallas as pl
from jax.experimental.pallas import tpu as pltpu

F32 = jnp.float32
BF16 = jnp.bfloat16

D_MODEL = 1024
DEPTH = 4
HEAD_DIM = 64
N_HEADS = 4
GRID_W = 64
ROPE_THETA = 10000.0
SSD_STATE = 128
D_FF = 2816
WINDOW = 128
PAST_LEN = 256
ALPHA = (2.0 * DEPTH) ** 0.25
CHUNK = 256
TOKEN_TILE = 256
INPROJ_TILE = 512
VMEM_LIMIT = 56 * 1024 * 1024
LOG2E = math.log2(math.e)
QK_SCALE = HEAD_DIM ** -0.5 * LOG2E

W_RET, W_F32, W_GQA, W_SWA = 0, 768, 2176, 2688
PROJ_W = 3200
F_XBC, F_RG, F_SZ, F_DT = 0, 768, 1024, 1280
F_WIDTH = 1408


def _silu(x):
    return x / (1.0 + jnp.exp(-x))


def _dot(a, b):
    return jnp.dot(a, b, preferred_element_type=F32)


def _dot_nt(a, b):
    return lax.dot_general(a, b, (((1,), (1,)), ((), ())), preferred_element_type=F32)


def _dot_tn(a, b):
    return lax.dot_general(a, b, (((0,), (0,)), ((), ())), preferred_element_type=F32)


def _seg_sum(x, seg):
    hi = x.astype(BF16)
    lo = (x - hi.astype(F32)).astype(BF16)
    return _dot(hi, seg) + _dot(lo, seg)


def _params(sem):
    return pltpu.CompilerParams(dimension_semantics=sem, vmem_limit_bytes=VMEM_LIMIT)


def _ada_kernel(c_ref, w_ref, b_ref, o_ref):
    s = _silu(c_ref[...])
    o_ref[...] = _dot(s.astype(BF16), w_ref[...].astype(BF16)) + b_ref[...]


def _ada_call(cond8, ada_w, ada_b):
    tn = D_MODEL
    nt = 6
    out = pl.pallas_call(
        _ada_kernel,
        out_shape=jax.ShapeDtypeStruct((DEPTH, nt, 8, tn), F32),
        grid=(DEPTH, nt),
        in_specs=[pl.BlockSpec((8, D_MODEL), lambda l, j: (0, 0)),
                  pl.BlockSpec((None, D_MODEL, tn), lambda l, j: (l, 0, j)),
                  pl.BlockSpec((None, None, 1, tn), lambda l, j: (l, j, 0, 0))],
        out_specs=pl.BlockSpec((None, None, 8, tn), lambda l, j: (l, j, 0, 0)),
        compiler_params=_params(("arbitrary", "arbitrary")),
        name="ada_mod",
    )(cond8, ada_w, ada_b.reshape(DEPTH, nt, 1, tn))
    return jnp.transpose(out, (0, 2, 1, 3)).reshape(DEPTH, 8, nt, 1, tn)


def _mod_spec(layer, row0, tiles_per_row, k):
    return pl.BlockSpec((None, None, None, 1, D_MODEL),
                        lambda i: (layer, row0 + i // tiles_per_row, k, 0, 0))


def _rope(x, cos, sin_signed):
    w = x.shape[-1]
    lane = lax.broadcasted_iota(jnp.int32, x.shape, 1)
    partner = jnp.where((lane % 32) < 16, pltpu.roll(x, w - 16, 1), pltpu.roll(x, 16, 1))
    return x * cos + partner * sin_signed


def _expand_q(q):
    lane = lax.broadcasted_iota(jnp.int32, (q.shape[0], 128), 1)
    low = lane < HEAD_DIM
    blocks = []
    for j in range(2):
        pair = q[:, 128 * j:128 * (j + 1)]
        blocks.append(jnp.where(low, pair, 0.0))
        blocks.append(jnp.where(low, pltpu.roll(pair, HEAD_DIM, 1), 0.0))
    return jnp.concatenate(blocks, axis=1).astype(BF16)


def _expand_kv(k, fill):
    lane = lax.broadcasted_iota(jnp.int32, k.shape, 1)
    low = lane < HEAD_DIM
    return jnp.concatenate([jnp.where(low, k, fill),
                            jnp.where(low, pltpu.roll(k, HEAD_DIM, 1), fill)], axis=1).astype(BF16)


def _inproj_kernel(*refs, rope, n_alias):
    x_ref, sh_ref, sc_ref, w_ref, qn, kn, seg = refs[:7]
    i = 7
    if rope:
        cos_ref, sin_ref = refs[7:9]
        i = 9
    i += n_alias
    f_o, r_o, qg_o, kg_o, vg_o, qw_o, kw_o, vw_o = refs[i:i + 8]
    h = (x_ref[...] * (1.0 + sc_ref[...]) + sh_ref[...]).astype(BF16)

    f_o[...] = _dot(h, w_ref[:, W_F32:W_F32 + F_WIDTH])
    ret = _dot(h, w_ref[:, W_RET:W_RET + 768])
    r_o[...] = jnp.concatenate([ret[:, 0:256], ret[:, 256:512] * HEAD_DIM ** -0.5, ret[:, 512:768]],
                               axis=1).astype(BF16)

    seg_m = seg[...]

    def rms(v, g, s):
        ms = _seg_sum(v * v, s) * (1.0 / HEAD_DIM)
        return v * lax.rsqrt(ms + 1e-6) * g

    gqa = _dot(h, w_ref[:, W_GQA:W_GQA + 512])
    swa = _dot(h, w_ref[:, W_SWA:W_SWA + 512])
    q_g = rms(gqa[:, 0:256], qn[...], seg_m)
    k_g = rms(gqa[:, 256:384], kn[...][:, :128], seg_m[:128, :128])
    q_w = swa[:, 0:256]
    k_w = swa[:, 256:384]
    if rope:
        cos = cos_ref[...]
        sin = sin_ref[...]
        q_g = _rope(q_g, cos, sin)
        q_w = _rope(q_w, cos, sin)
        k_g = _rope(k_g, cos[:, :128], sin[:, :128])
        k_w = _rope(k_w, cos[:, :128], sin[:, :128])
    else:
        for o_ref, piece in zip(refs[i + 8:i + 12], (k_g, gqa[:, 384:512], swa[:, 256:384], swa[:, 384:512])):
            o_ref[...] = piece.reshape(o_ref.shape)
    qg_o[...] = _expand_q(q_g * QK_SCALE)
    qw_o[...] = _expand_q(q_w * QK_SCALE)
    kg_o[...] = _expand_kv(k_g, 0.0)
    kw_o[...] = _expand_kv(k_w, 0.0)
    vg_o[...] = _expand_kv(gqa[:, 384:512], 1.0)
    vw_o[...] = _expand_kv(swa[:, 384:512], 1.0)


def _inproj_call(x, mod5, w_in_p, layer, row0, tiles_per_row, qn256, kn256, seg, rope_tabs, seq_len,
                 new_kv_prev=None):
    m = x.shape[0]
    tm = INPROJ_TILE
    tiles_per_row = tiles_per_row * TOKEN_TILE // tm
    rope = rope_tabs is not None
    const = lambda shape: pl.BlockSpec(shape, lambda i: (0, 0))
    in_specs = [pl.BlockSpec((tm, D_MODEL), lambda i: (i, 0)),
                _mod_spec(layer, row0, tiles_per_row, 0),
                _mod_spec(layer, row0, tiles_per_row, 1),
                pl.BlockSpec((None, D_MODEL, PROJ_W), lambda i: (layer, 0, 0)),
                const((1, 256)), const((1, 256)), const((256, 256))]
    args = [x, mod5, mod5, w_in_p, qn256, kn256, seg]
    if rope:
        tpb = seq_len // tm
        tab = pl.BlockSpec((tm, 256), lambda i: (i % tpb, 0))
        in_specs += [tab, tab]
        args += list(rope_tabs)
    widths = [(F_WIDTH, F32), (768, BF16), (512, BF16), (256, BF16), (256, BF16),
              (512, BF16), (256, BF16), (256, BF16)]
    out_shape = [jax.ShapeDtypeStruct((m, w), dt) for w, dt in widths]
    out_specs = [pl.BlockSpec((tm, w), lambda i: (i, 0)) for w, _ in widths]
    aliases = {}
    if not rope:
        bpt = tm // seq_len
        out_shape += [jax.ShapeDtypeStruct((m // seq_len, DEPTH, seq_len, 128), F32)] * 4
        out_specs += [pl.BlockSpec((bpt, None, seq_len, 128), lambda i: (i, layer, 0, 0))] * 4
        if new_kv_prev is not None:
            aliases = {len(args) + t: len(widths) + t for t in range(4)}
            in_specs += [pl.BlockSpec(memory_space=pl.ANY)] * 4
            args += list(new_kv_prev)
    return pl.pallas_call(
        functools.partial(_inproj_kernel, rope=rope, n_alias=len(aliases)),
        out_shape=out_shape, grid=(m // tm,), in_specs=in_specs, out_specs=out_specs,
        input_output_aliases=aliases,
        compiler_params=_params(("arbitrary",)), name="in_proj",
    )(*args)


def _online(m, acc, s_blocks, v_blocks):
    m_new = m
    for s in s_blocks:
        m_new = jnp.maximum(m_new, jnp.max(s, axis=-1, keepdims=True))
    acc = jnp.exp2(m - m_new) * acc
    for s, v in zip(s_blocks, v_blocks):
        acc = acc + _dot(jnp.exp2(s - m_new).astype(BF16), v)
    return m_new, acc


def _attend(q_ref, k_ref, v_ref, kc_ref, vc_ref, sink_ref, o_ref, *, qi, seq_len, tq, kv_chunk, window):
    rows = 2 * tq
    q2 = [jnp.concatenate([q_ref[:, 256 * j:256 * j + 128], q_ref[:, 256 * j + 128:256 * (j + 1)]], axis=0)
          for j in range(2)]

    def update(carry, kv_blocks, masks):
        out = []
        for j in range(2):
            lanes = slice(128 * j, 128 * (j + 1))
            s_blocks = []
            for (k_blk, _), mask in zip(kv_blocks, masks):
                s = _dot_nt(q2[j], k_blk[:, lanes])
                s_blocks.append(s if mask is None else jnp.where(mask, s, -jnp.inf))
            out.append(_online(*carry[j], s_blocks, [v_blk[:, lanes] for _, v_blk in kv_blocks]))
        return tuple(out)

    if sink_ref is not None:
        upper = lax.broadcasted_iota(jnp.int32, (rows, 128), 1) >= HEAD_DIM
        carry = tuple((jnp.max(sink_ref[j], axis=-1, keepdims=True) * LOG2E, jnp.where(upper, 1.0, 0.0))
                      for j in range(2))
    else:
        carry = tuple((jnp.full((rows, 1), -jnp.inf, F32), jnp.zeros((rows, 128), F32)) for _ in range(2))
    if window:
        span = tq + 2 * WINDOW
        start = pl.multiple_of(jnp.clip(qi * tq - WINDOW, 0, seq_len - span), 128)
        qpos = qi * tq + lax.broadcasted_iota(jnp.int32, (rows, span), 0) % tq
        kpos = start + lax.broadcasted_iota(jnp.int32, (rows, span), 1)
        blocks = [(k_ref[pl.ds(start, span), :], v_ref[pl.ds(start, span), :])]
        masks = [jnp.abs(qpos - kpos) <= WINDOW]
        if kc_ref is not None:
            blocks.append((kc_ref[...], vc_ref[...]))
            masks.append(None)
        carry = update(carry, blocks, masks)
    else:
        for c in range(seq_len // kv_chunk):
            blk = slice(c * kv_chunk, (c + 1) * kv_chunk)
            carry = update(carry, [(k_ref[blk, :], v_ref[blk, :])], [None])
        if kc_ref is not None:
            carry = update(carry, [(kc_ref[...], vc_ref[...])], [None])

    low = lax.broadcasted_iota(jnp.int32, (tq, 128), 1) < HEAD_DIM
    for j in range(2):
        acc = carry[j][1]
        o = acc / pltpu.roll(acc, HEAD_DIM, 1)
        o_ref[:, 128 * j:128 * (j + 1)] = jnp.where(low, o[:tq], pltpu.roll(o[tq:], HEAD_DIM, 1)).astype(o_ref.dtype)


def _attn_kernel(*refs, seq_len, tq, kv_chunk, mixers):
    n_in = sum(3 + 2 * c + s for c, s, _ in mixers)
    outs = refs[n_in:]
    i = 0
    for o_ref, (has_cache, has_sink, window) in zip(outs, mixers):
        q_ref, k_ref, v_ref = refs[i:i + 3]
        i += 3
        kc_ref, vc_ref = refs[i:i + 2] if has_cache else (None, None)
        i += 2 * has_cache
        sink_ref = refs[i] if has_sink else None
        i += has_sink
        _attend(q_ref, k_ref, v_ref, kc_ref, vc_ref, sink_ref, o_ref, qi=pl.program_id(1),
                seq_len=seq_len, tq=tq, kv_chunk=kv_chunk, window=window)


def _attn_call(mixers, nb, seq_len):
    tq = TOKEN_TILE
    nq = seq_len // tq
    kv_chunk = min(1024, seq_len)
    in_specs, args, static = [], [], []
    for mx in mixers:
        in_specs += [pl.BlockSpec((tq, 512), lambda b, i: (b * nq + i, 0)),
                     pl.BlockSpec((seq_len, 256), lambda b, i: (b, 0)),
                     pl.BlockSpec((seq_len, 256), lambda b, i: (b, 0))]
        args += [mx["q"], mx["k"], mx["v"]]
        if mx.get("cache") is not None:
            in_specs += [pl.BlockSpec((None, PAST_LEN, 256), lambda b, i: (b, 0, 0))] * 2
            args += list(mx["cache"])
        if mx.get("sink") is not None:
            in_specs.append(pl.BlockSpec((2, 2 * tq, 128), lambda b, i: (0, 0, 0)))
            args.append(mx["sink"])
        static.append((mx.get("cache") is not None, mx.get("sink") is not None, bool(mx.get("window"))))
    out = jax.ShapeDtypeStruct((nb * seq_len, 256), BF16)
    out_spec = pl.BlockSpec((tq, 256), lambda b, i: (b * nq + i, 0))
    return pl.pallas_call(
        functools.partial(_attn_kernel, seq_len=seq_len, tq=tq, kv_chunk=kv_chunk, mixers=tuple(static)),
        out_shape=[out] * len(mixers), grid=(nb, nq), in_specs=in_specs, out_specs=[out_spec] * len(mixers),
        compiler_params=_params(("arbitrary", "arbitrary")), name="attention",
    )(*args)


def _chunk_of(s, nc):
    return jnp.where(s < nc, nc - 1 - s, s - nc)


def _log_gamma(rd):
    return jnp.log1p(-jnp.exp(rd))


def _ret_kernel(*refs, nc, has_init, emit_state):
    q_ref, k_ref, v_ref, g_ref, rdl_ref, rdh_ref, gn_ref, seg_ref = refs[:8]
    i = 8
    if has_init:
        s0_ref = refs[i]
        i += 1
    y_ref = refs[i]
    i += 1
    if emit_state:
        st_ref = refs[i]
        i += 1
    m_sc, dec_sc, qk_sc, sf_sc, sb_sc, sball_sc = refs[i:i + 6]
    c_len = CHUNK
    b = pl.program_id(0)
    s = pl.program_id(1)

    @pl.when(jnp.logical_and(b == 0, s == 0))
    def _():
        ri = lax.broadcasted_iota(jnp.int32, (c_len, c_len), 0)
        ci = lax.broadcasted_iota(jnp.int32, (c_len, c_len), 1)
        d = (ri - ci).astype(F32)
        for h in range(N_HEADS):
            lgf = _log_gamma(jnp.broadcast_to(rdh_ref[0, h], (c_len, c_len)))
            lgb = _log_gamma(jnp.broadcast_to(rdh_ref[1, h], (c_len, c_len)))
            m_sc[h] = jnp.where(d > 0, jnp.exp(d * lgf), jnp.where(d < 0, jnp.exp(-d * lgb), 2.0))
        for dr in range(2):
            rows = [jnp.exp(c_len * _log_gamma(jnp.broadcast_to(rdh_ref[dr, h], (HEAD_DIM, c_len))))
                    for h in range(N_HEADS)]
            dec_sc[dr] = jnp.concatenate(rows, axis=0)
        a = lax.broadcasted_iota(jnp.int32, (c_len, 256), 0).astype(F32)
        lgf = _log_gamma(rdl_ref[0])
        lgb = _log_gamma(rdl_ref[1])
        qk_sc[0] = jnp.exp((a + 1.0) * lgf)
        qk_sc[1] = jnp.exp((c_len - 1.0 - a) * lgf)
        qk_sc[2] = jnp.exp((c_len - a) * lgb)
        qk_sc[3] = jnp.exp(a * lgb)

    @pl.when(s == 0)
    def _():
        sf_sc[...] = jnp.zeros_like(sf_sc)
        sb_sc[...] = jnp.zeros_like(sb_sc)
        if has_init:
            for h in range(N_HEADS):
                blk = slice(HEAD_DIM * h, HEAD_DIM * (h + 1))
                sf_sc[blk, blk] = s0_ref[0, h]
                sb_sc[blk, blk] = s0_ref[1, h]

    ri = lax.broadcasted_iota(jnp.int32, (256, 256), 0) // HEAD_DIM
    ci = lax.broadcasted_iota(jnp.int32, (256, 256), 1) // HEAD_DIM
    diag = ri == ci
    kb = k_ref[...]
    ks = kb.astype(F32)
    vb = v_ref[...]

    @pl.when(s < nc)
    def _():
        c = nc - 1 - s
        sball_sc[c] = sb_sc[...]
        upd = _dot_tn((ks * qk_sc[3]).astype(BF16), vb)
        sb_sc[...] = sb_sc[...] * dec_sc[1] + jnp.where(diag, upd, 0.0)
        if emit_state:
            @pl.when(s == nc - 1)
            def _():
                for h in range(N_HEADS):
                    blk = slice(HEAD_DIM * h, HEAD_DIM * (h + 1))
                    st_ref[1, h] = sb_sc[blk, blk]

    @pl.when(s >= nc)
    def _():
        c = s - nc
        qb = q_ref[...]
        q = qb.astype(F32)
        lane_head = lax.broadcasted_iota(jnp.int32, (c_len, 256), 1) // HEAD_DIM
        o = (_dot(qb, sf_sc[...].astype(BF16)) * qk_sc[0]
             + _dot(qb, sball_sc[c].astype(BF16)) * qk_sc[2])
        for h in range(N_HEADS):
            mine = lane_head == h
            sc = _dot_nt(jnp.where(mine, q, 0.0).astype(BF16), kb)
            pv = _dot((sc * m_sc[h]).astype(BF16), vb)
            o = o + jnp.where(mine, pv, 0.0)
        seg = seg_ref[...]
        mu = _seg_sum(o, seg) * (1.0 / HEAD_DIM)
        dlt = o - mu
        var = _seg_sum(dlt * dlt, seg) * (1.0 / HEAD_DIM)
        on = dlt * lax.rsqrt(var + 1e-5) * gn_ref[...]
        y_ref[...] = (on * _silu(g_ref[...])).astype(y_ref.dtype)
        upd = _dot_tn((ks * qk_sc[1]).astype(BF16), vb)
        sf_sc[...] = sf_sc[...] * dec_sc[0] + jnp.where(diag, upd, 0.0)
        if emit_state:
            @pl.when(s == 2 * nc - 1)
            def _():
                for h in range(N_HEADS):
                    blk = slice(HEAD_DIM * h, HEAD_DIM * (h + 1))
                    st_ref[0, h] = sf_sc[blk, blk]


def _ret_specs(rqkv, pf, nb, seq_len, rd_lane, rd_head, gn_g, seg, layer, state0=None, emit_state=False):
    c_len = CHUNK
    nc = seq_len // c_len

    def col(start):
        return pl.BlockSpec((c_len, 256), lambda b, s: (b * nc + _chunk_of(s, nc), start // 256))

    in_specs = [col(0), col(256), col(512), col(F_RG),
                pl.BlockSpec((2, 1, 256), lambda b, s: (0, 0, 0)),
                pl.BlockSpec((2, N_HEADS, 1, c_len), lambda b, s: (0, 0, 0, 0)),
                pl.BlockSpec((1, 256), lambda b, s: (0, 0)),
                pl.BlockSpec((256, 256), lambda b, s: (0, 0))]
    args = [rqkv, rqkv, rqkv, pf, rd_lane, rd_head, gn_g, seg]
    if state0 is not None:
        in_specs.append(pl.BlockSpec((None, None, 2, N_HEADS, HEAD_DIM, HEAD_DIM),
                                     lambda b, s: (b, layer, 0, 0, 0, 0)))
        args.append(state0)
    outs = [jax.ShapeDtypeStruct((nb * seq_len, 256), BF16)]
    out_specs = [pl.BlockSpec((c_len, 256), lambda b, s: (b * nc + jnp.maximum(s - nc, 0), 0))]
    if emit_state:
        outs.append(jax.ShapeDtypeStruct((nb, DEPTH, 2, N_HEADS, HEAD_DIM, HEAD_DIM), F32))
        out_specs.append(pl.BlockSpec((None, None, 2, N_HEADS, HEAD_DIM, HEAD_DIM),
                                      lambda b, s: (b, layer, 0, 0, 0, 0)))
    scratch = [pltpu.VMEM((N_HEADS, c_len, c_len), F32), pltpu.VMEM((2, 256, 256), F32),
               pltpu.VMEM((4, c_len, 256), F32), pltpu.VMEM((256, 256), F32),
               pltpu.VMEM((256, 256), F32), pltpu.VMEM((nc, 256, 256), F32)]
    return in_specs, args, outs, out_specs, scratch


def _cumsum_rows(x, reverse):
    n = x.shape[0]
    row = lax.broadcasted_iota(jnp.int32, x.shape, 0)
    sh = 1
    while sh < n:
        if reverse:
            x = x + jnp.where(row < n - sh, pltpu.roll(x, n - sh, 0), 0.0)
        else:
            x = x + jnp.where(row >= sh, pltpu.roll(x, sh, 0), 0.0)
        sh *= 2
    return x


def _ssd_kernel(*refs, nc, has_init, emit_state):
    xbc_ref, prev_ref, next_ref, z_ref, dt_ref, cw_ref, cb_ref, dtb_ref, al_ref, dl_ref, ng_ref = refs[:11]
    i = 11
    if has_init:
        h0_ref = refs[i]
        i += 1
    y_ref = refs[i]
    i += 1
    if emit_state:
        st_ref = refs[i]
        i += 1
    hf_sc, hb_sc, hball_sc, xs_sc, bm_sc, cm_sc, dt_sc, cum_sc, rev_sc = refs[i:i + 9]
    c_len = CHUNK
    s = pl.program_id(1)
    c = _chunk_of(s, nc)

    @pl.when(s == 0)
    def _():
        if has_init:
            hf_sc[...] = h0_ref[0]
            hb_sc[...] = h0_ref[1]
        else:
            hf_sc[...] = jnp.zeros_like(hf_sc)
            hb_sc[...] = jnp.zeros_like(hb_sc)

    low = lax.broadcasted_iota(jnp.int32, (c_len, 128), 1) < HEAD_DIM
    low_rows = lax.broadcasted_iota(jnp.int32, (128, 128), 0) < HEAD_DIM

    def col(a, lane):
        return jnp.broadcast_to(a[:, lane:lane + 1], (c_len, c_len))

    def pair(a0, a1):
        return jnp.where(low, a0[:, :128], a1[:, :128])

    def state_update(h_sc, g, xs, bm, dt, cols, tot_row, dt_lanes):
        w = [jnp.exp2(tot_row[r] - cols[r]) * col(dt, dt_lanes[r]) for r in range(2)]
        xw = (xs[:, 128 * g:128 * (g + 1)] * pair(w[0], w[1])).astype(BF16)
        dec = jnp.where(low_rows, jnp.broadcast_to(jnp.exp2(tot_row[0][:, :128]), (128, 128)),
                        jnp.broadcast_to(jnp.exp2(tot_row[1][:, :128]), (128, 128)))
        h_sc[g] = h_sc[g] * dec + _dot_tn(xw, bm[:, 128 * g:128 * (g + 1)])

    @pl.when(s < nc)
    def _():
        xbc = xbc_ref[...]
        row = lax.broadcasted_iota(jnp.int32, xbc.shape, 0)
        before = jnp.where(c > 0, prev_ref[7:8, :], 0.0)
        after = jnp.where(c < nc - 1, next_ref[0:1, :], 0.0)
        x_prev = jnp.where(row == 0, before, pltpu.roll(xbc, 1, 0))
        x_next = jnp.where(row == c_len - 1, after, pltpu.roll(xbc, c_len - 1, 0))
        conv = _silu(cw_ref[0:1, :] * x_prev + cw_ref[1:2, :] * xbc + cw_ref[2:3, :] * x_next + cb_ref[...])
        xs = conv[:, 0:256]
        bm = conv[:, 256:512].astype(BF16)
        pre = dt_ref[...] + dtb_ref[...]
        dt = jnp.maximum(pre, 0.0) + jnp.log1p(jnp.exp(-jnp.abs(pre)))
        dta = dt * (-jnp.exp(al_ref[...]) * LOG2E)
        rev = _cumsum_rows(dta, reverse=True)
        xs_sc[c] = xs
        bm_sc[c] = bm
        cm_sc[c] = conv[:, 512:768].astype(BF16)
        dt_sc[c] = dt
        cum_sc[c] = _cumsum_rows(dta, reverse=False)
        rev_sc[c] = rev
        hball_sc[c] = hb_sc[...]
        for g in range(2):
            cols = [col(rev, 4 + 2 * g + r) for r in range(2)]
            tot = [cl[0:1, :] for cl in cols]
            state_update(hb_sc, g, xs, bm, dt, cols, tot, [4 + 2 * g, 5 + 2 * g])
        if emit_state:
            @pl.when(s == nc - 1)
            def _():
                st_ref[1] = hb_sc[...]

    @pl.when(s >= nc)
    def _():
        xs = xs_sc[c]
        bm = bm_sc[c]
        cm = cm_sc[c]
        dt = dt_sc[c]
        cum = cum_sc[c]
        rev = rev_sc[c]
        log_dt = jnp.log2(dt)
        off_f = (log_dt - cum).T
        off_b = (log_dt - rev).T
        dt_t = dt.T
        ri = lax.broadcasted_iota(jnp.int32, (c_len, c_len), 0)
        ci = lax.broadcasted_iota(jnp.int32, (c_len, c_len), 1)
        below = ri > ci
        on_diag = ri == ci
        lane_head = lax.broadcasted_iota(jnp.int32, (c_len, 256), 1) // HEAD_DIM
        xs_b = xs.astype(BF16)
        y = jnp.zeros((c_len, 256), F32)
        inter = []
        for g in range(2):
            gl = slice(128 * g, 128 * (g + 1))
            gmat = _dot_nt(cm[:, gl], bm[:, gl])
            cols_f, cols_b = [], []
            for r in range(2):
                h = 2 * g + r
                cf = col(cum, h)
                cb = col(rev, 4 + h)
                cols_f.append(cf)
                cols_b.append(cb)
                w = jnp.exp2(jnp.where(below, cf + off_f[h:h + 1, :], cb + off_b[4 + h:5 + h, :]))
                w = w + jnp.where(on_diag, dt_t[h:h + 1, :], 0.0)
                yh = _dot((gmat * w).astype(BF16), xs_b)
                y = y + jnp.where(lane_head == h, yh, 0.0)
            yf = _dot_nt(cm[:, gl], hf_sc[g].astype(BF16))
            yb = _dot_nt(cm[:, gl], hball_sc[c, g].astype(BF16))
            inter.append(yf * pair(jnp.exp2(cols_f[0]), jnp.exp2(cols_f[1]))
                         + yb * pair(jnp.exp2(cols_b[0]), jnp.exp2(cols_b[1])))
            tot = [cl[c_len - 1:c_len, :] for cl in cols_f]
            state_update(hf_sc, g, xs, bm, dt, cols_f, tot, [2 * g, 2 * g + 1])
        y = y + jnp.concatenate(inter, axis=1) + xs * dl_ref[...]
        y = y * _silu(z_ref[...])
        ms = jnp.mean(y * y, axis=-1, keepdims=True)
        y_ref[...] = (y * lax.rsqrt(ms + 1e-6) * ng_ref[...]).astype(y_ref.dtype)
        if emit_state:
            @pl.when(s == 2 * nc - 1)
            def _():
                st_ref[0] = hf_sc[...]


def _ssd_specs(p, nb, seq_len, conv_w, conv_b, dt_bias, a_log, d_lane, norm_g, layer,
              state0=None, emit_state=False):
    c_len = CHUNK
    nc = seq_len // c_len
    m = nb * seq_len
    r8 = c_len // 8
    chunk = lambda b, s: b * nc + jnp.maximum(nc - 1 - s, 0)
    in_specs = [pl.BlockSpec((c_len, 768), lambda b, s: (chunk(b, s), F_XBC // 768)),
                pl.BlockSpec((8, 768), lambda b, s: (jnp.maximum(chunk(b, s) * r8 - 1, 0), F_XBC // 768)),
                pl.BlockSpec((8, 768), lambda b, s: (jnp.minimum((chunk(b, s) + 1) * r8, m // 8 - 1),
                                                     F_XBC // 768)),
                pl.BlockSpec((c_len, 256), lambda b, s: (b * nc + jnp.maximum(s - nc, 0), F_SZ // 256)),
                pl.BlockSpec((c_len, 128), lambda b, s: (chunk(b, s), F_DT // 128)),
                pl.BlockSpec((None, 3, 768), lambda b, s: (layer, 0, 0)),
                pl.BlockSpec((1, 768), lambda b, s: (0, 0)),
                pl.BlockSpec((1, 128), lambda b, s: (0, 0)),
                pl.BlockSpec((1, 128), lambda b, s: (0, 0)),
                pl.BlockSpec((1, 256), lambda b, s: (0, 0)),
                pl.BlockSpec((1, 256), lambda b, s: (0, 0))]
    args = [p, p, p, p, p, conv_w, conv_b, dt_bias, a_log, d_lane, norm_g]
    if state0 is not None:
        in_specs.append(pl.BlockSpec((None, None, 2, 2, 128, SSD_STATE),
                                     lambda b, s: (b, layer, 0, 0, 0, 0)))
        args.append(state0)
    outs = [jax.ShapeDtypeStruct((m, 256), BF16)]
    out_specs = [pl.BlockSpec((c_len, 256), lambda b, s: (b * nc + jnp.maximum(s - nc, 0), 0))]
    if emit_state:
        outs.append(jax.ShapeDtypeStruct((nb, DEPTH, 2, 2, 128, SSD_STATE), F32))
        out_specs.append(pl.BlockSpec((None, None, 2, 2, 128, SSD_STATE),
                                      lambda b, s: (b, layer, 0, 0, 0, 0)))
    scratch = [pltpu.VMEM((2, 128, SSD_STATE), F32), pltpu.VMEM((2, 128, SSD_STATE), F32),
               pltpu.VMEM((nc, 2, 128, SSD_STATE), F32),
               pltpu.VMEM((nc, c_len, 256), F32), pltpu.VMEM((nc, c_len, 256), BF16),
               pltpu.VMEM((nc, c_len, 256), BF16), pltpu.VMEM((nc, c_len, 128), F32),
               pltpu.VMEM((nc, c_len, 128), F32), pltpu.VMEM((nc, c_len, 128), F32)]
    return in_specs, args, outs, out_specs, scratch


def _scan_kernel(*refs, counts, **static):
    (ri, si), (ro, so), (rs, ss), n_alias = counts
    ins, rest = refs[:ri + si], refs[ri + si + n_alias:]
    outs, scr = rest[:ro + so], rest[ro + so:]
    _ret_kernel(*ins[:ri], *outs[:ro], *scr[:rs], **static)
    _ssd_kernel(*ins[ri:], *outs[ro:], *scr[rs:], **static)


def _scan_call(ret_parts, ssd_parts, nb, seq_len, has_init, emit_state, states_prev=None):
    nc = seq_len // CHUNK
    in_specs = ret_parts[0] + ssd_parts[0]
    args = ret_parts[1] + ssd_parts[1]
    aliases = {}
    if states_prev is not None:
        aliases = {len(args): 1, len(args) + 1: len(ret_parts[2]) + 1}
        in_specs = in_specs + [pl.BlockSpec(memory_space=pl.ANY)] * 2
        args = args + list(states_prev)
    counts = tuple((len(a), len(b)) for a, b in zip((ret_parts[0], ret_parts[2], ret_parts[4]),
                                                    (ssd_parts[0], ssd_parts[2], ssd_parts[4])))
    return pl.pallas_call(
        functools.partial(_scan_kernel, counts=counts + (len(aliases),), nc=nc, has_init=has_init,
                          emit_state=emit_state),
        out_shape=ret_parts[2] + ssd_parts[2], grid=(nb, 2 * nc),
        in_specs=in_specs, out_specs=ret_parts[3] + ssd_parts[3],
        scratch_shapes=ret_parts[4] + ssd_parts[4], input_output_aliases=aliases,
        compiler_params=_params(("arbitrary", "arbitrary")), name="scans",
    )(*args)


def _layer_norm(x, g, b):
    mu = jnp.mean(x, axis=-1, keepdims=True)
    d = x - mu
    var = jnp.mean(d * d, axis=-1, keepdims=True)
    return d * lax.rsqrt(var + 1e-5) * g + b


FFN_CHUNK = 256
FFN_STREAMS = 1
FFN_ROWS = 512


def _outffn_kernel(x_ref, yr_ref, yg_ref, ys_ref, yw_ref, g1_ref, sh2_ref, sc2_ref, g2_ref,
                   wo_ref, wi_ref, wf_ref, l1g_ref, l1b_ref, l2g_ref, l2b_ref, o_ref):
    n_chunks = D_FF // FFN_CHUNK
    half = n_chunks // 2

    def head(r):
        rows = slice(FFN_ROWS * r, FFN_ROWS * (r + 1))
        mix = (_dot(yr_ref[rows, :], wo_ref[0:256, :]) + _dot(yg_ref[rows, :], wo_ref[256:512, :])
               + _dot(ys_ref[rows, :], wo_ref[512:768, :]) + _dot(yw_ref[rows, :], wo_ref[768:1024, :]))
        x1 = _layer_norm(ALPHA * x_ref[rows, :] + g1_ref[...] * mix, l1g_ref[...], l1b_ref[...])
        return x1, (x1 * (1.0 + sc2_ref[...]) + sh2_ref[...]).astype(BF16)

    def ffn(h2, acc, chunks):
        for j in chunks:
            cols = slice(FFN_CHUNK * j, FFN_CHUNK * (j + 1))
            gate = _dot(h2, wi_ref[:, cols])
            up = _dot(h2, wi_ref[:, D_FF + FFN_CHUNK * j:D_FF + FFN_CHUNK * (j + 1)])
            d = _dot((_silu(gate) * up).astype(BF16), wf_ref[cols, :])
            acc = d if acc is None else acc + d
        return acc

    def tail(r, x1, acc):
        rows = slice(FFN_ROWS * r, FFN_ROWS * (r + 1))
        o_ref[rows, :] = _layer_norm(ALPHA * x1 + g2_ref[...] * acc, l2g_ref[...], l2b_ref[...])

    state = [None] * FFN_STREAMS
    x1, h2 = head(0)
    state[0] = (x1, h2, ffn(h2, None, range(0, half)))
    for r in range(FFN_STREAMS):
        x1, h2, acc = state[r]
        if r + 1 < FFN_STREAMS:
            nx1, nh2 = head(r + 1)
        acc = ffn(h2, acc, range(half, n_chunks))
        if r + 1 < FFN_STREAMS:
            state[r + 1] = (nx1, nh2, ffn(nh2, None, range(0, half)))
        tail(r, x1, acc)


def _outffn_call(x, ys, mod5, w_out_b, w_ffn_in_b, w_ffn_out_b, ln, layer, row0, tiles_per_row):
    m = x.shape[0]
    tm = FFN_ROWS * FFN_STREAMS
    tiles_per_row = tiles_per_row * TOKEN_TILE // tm
    row = lambda w: pl.BlockSpec((tm, w), lambda i: (i, 0))
    once = pl.Buffered(1)
    vec = pl.BlockSpec((None, 1, D_MODEL), lambda i: (layer, 0, 0))
    in_specs = [row(D_MODEL), row(256), row(256), row(256), row(256),
                _mod_spec(layer, row0, tiles_per_row, 2), _mod_spec(layer, row0, tiles_per_row, 3),
                _mod_spec(layer, row0, tiles_per_row, 4), _mod_spec(layer, row0, tiles_per_row, 5),
                pl.BlockSpec((None, D_MODEL, D_MODEL), lambda i: (layer, 0, 0), pipeline_mode=once),
                pl.BlockSpec((None, D_MODEL, 2 * D_FF), lambda i: (layer, 0, 0), pipeline_mode=once),
                pl.BlockSpec((None, D_FF, D_MODEL), lambda i: (layer, 0, 0), pipeline_mode=once),
                vec, vec, vec, vec]
    return pl.pallas_call(
        _outffn_kernel,
        out_shape=jax.ShapeDtypeStruct((m, D_MODEL), F32),
        grid=(m // tm,), in_specs=in_specs, out_specs=row(D_MODEL),
        compiler_params=_params(("arbitrary",)), name="out_ffn",
    )(x, *ys, mod5, mod5, mod5, mod5, w_out_b, w_ffn_in_b, w_ffn_out_b, *ln)


def _rope_tables(seq_len):
    t = jnp.arange(seq_len)
    rowp = (t // GRID_W).astype(F32)
    colp = (t % GRID_W).astype(F32)
    quarter = HEAD_DIM // 4
    inv_freq = ROPE_THETA ** (-jnp.arange(quarter, dtype=F32) / quarter)
    ang_r = rowp[:, None] * inv_freq[None, :]
    ang_c = colp[:, None] * inv_freq[None, :]
    cos = jnp.concatenate([jnp.cos(ang_r)] * 2 + [jnp.cos(ang_c)] * 2, axis=-1)
    sin = jnp.concatenate([-jnp.sin(ang_r), jnp.sin(ang_r), -jnp.sin(ang_c), jnp.sin(ang_c)], axis=-1)
    return jnp.tile(cos, (1, N_HEADS)), jnp.tile(sin, (1, N_HEADS))


def _expand_cache(c, fill):
    pad = jnp.full(c.shape[:2] + (HEAD_DIM,), fill, c.dtype)
    return jnp.concatenate([c[:, :, 0], pad, c[:, :, 1], pad], axis=-1).astype(BF16)


def _reorder_w_in(w):
    pad = jnp.zeros(w.shape[:-1] + (F_WIDTH - F_DT - 8,), w.dtype)
    return jnp.concatenate([w[..., 0:768],
                            w[..., 1792:2560], w[..., 768:1024], w[..., 1536:1792], w[..., 2560:2568], pad,
                            w[..., 1024:1536], w[..., 2568:3080]], axis=-1).astype(BF16)


def kernel(x_prompt, x_sample, cache_gqa_k, cache_gqa_v, cache_swa_k, cache_swa_v, state_ret, state_ssd,
           c, c_ctx, ada_w, ada_b, w_in, ret_decay, ret_gn_g, gqa_q_norm, gqa_k_norm,
           ssd_conv_w, ssd_conv_b, ssd_dt_bias, ssd_a_log, ssd_d, ssd_norm_g, swa_sink, w_out,
           ln1_g, ln1_b, w_ffn_in, w_ffn_out, ln2_g, ln2_b):
    nb_c, len_c, _ = x_prompt.shape
    nb_l, len_l, _ = x_sample.shape
    depth = ada_w.shape[0]

    cond8 = jnp.zeros((8, D_MODEL), F32).at[0].set(c_ctx).at[1:1 + nb_l].set(c)
    mod5 = _ada_call(cond8, ada_w, ada_b)

    w_in_p = _reorder_w_in(w_in)
    w_out_b = w_out.astype(BF16)
    w_ffn_in_b = w_ffn_in.astype(BF16)
    w_ffn_out_b = w_ffn_out.astype(BF16)
    lane_head = jnp.arange(256) // HEAD_DIM
    seg = (lane_head[:, None] == lane_head[None, :]).astype(BF16)
    rope_tabs = _rope_tables(len_l)
    caches_g = (_expand_cache_layers(cache_gqa_k, 0.0), _expand_cache_layers(cache_gqa_v, 1.0))
    caches_w = (_expand_cache_layers(cache_swa_k, 0.0), _expand_cache_layers(cache_swa_v, 1.0))
    state_ssd_g = state_ssd.reshape(nb_l, depth, 2, 2, 128, SSD_STATE)

    def layer_consts(l):
        rd = ret_decay[l]
        return dict(
            rd_lane=jnp.repeat(rd, HEAD_DIM, axis=-1).reshape(2, 1, 256),
            rd_head=jnp.broadcast_to(rd[:, :, None, None], (2, N_HEADS, 1, CHUNK)),
            gn_g=ret_gn_g[l].reshape(1, 256),
            qn=jnp.tile(gqa_q_norm[l], N_HEADS).reshape(1, 256),
            kn=jnp.tile(gqa_k_norm[l], N_HEADS).reshape(1, 256),
            conv_b=ssd_conv_b[l].reshape(1, 768),
            dt_bias=jnp.zeros((1, 128), F32).at[0, :8].set(ssd_dt_bias[l].reshape(8)),
            a_log=jnp.zeros((1, 128), F32).at[0, :8].set(ssd_a_log[l].reshape(8)),
            d_lane=jnp.repeat(ssd_d[l], HEAD_DIM).reshape(1, 256),
            norm_g=ssd_norm_g[l].reshape(1, 256),
            sink=jnp.broadcast_to(swa_sink[l].reshape(2, 2, 1, 1), (2, 2, TOKEN_TILE, 128)
                                  ).reshape(2, 2 * TOKEN_TILE, 128),
        )

    ln = tuple(a.reshape(depth, 1, D_MODEL) for a in (ln1_g, ln1_b, ln2_g, ln2_b))

    def run_group(x, nb, seq_len, row0, latent):
        m = nb * seq_len
        tpr = (seq_len // TOKEN_TILE) if latent else (m // TOKEN_TILE)
        x = x.reshape(m, D_MODEL)
        new_kv = states = None
        for l in range(depth):
            k = layer_consts(l)
            proj = _inproj_call(x, mod5, w_in_p, l, row0, tpr, k["qn"], k["kn"], seg,
                                rope_tabs if latent else None, seq_len, new_kv_prev=new_kv)
            pf, rqkv, qg, kg, vg, qw, kw, vw = proj[:8]
            ret_parts = _ret_specs(rqkv, pf, nb, seq_len, k["rd_lane"], k["rd_head"], k["gn_g"], seg, l,
                                   state0=state_ret if latent else None, emit_state=not latent)
            ssd_parts = _ssd_specs(pf, nb, seq_len, ssd_conv_w, k["conv_b"], k["dt_bias"], k["a_log"],
                                   k["d_lane"], k["norm_g"], l,
                                   state0=state_ssd_g if latent else None, emit_state=not latent)
            scans = _scan_call(ret_parts, ssd_parts, nb, seq_len, has_init=latent, emit_state=not latent,
                               states_prev=states)
            if latent:
                y_ret, y_ssd = scans
                y_gqa, y_swa = _attn_call(
                    [dict(q=qg, k=kg, v=vg, cache=(caches_g[0][l], caches_g[1][l])),
                     dict(q=qw, k=kw, v=vw, cache=(caches_w[0][l], caches_w[1][l]), sink=k["sink"],
                          window=True)], nb, seq_len)
            else:
                y_ret, st_ret, y_ssd, st_ssd = scans
                states = (st_ret, st_ssd)
                new_kv = proj[8:12]
                y_gqa, = _attn_call([dict(q=qg, k=kg, v=vg)], nb, seq_len)
                y_swa, = _attn_call([dict(q=qw, k=kw, v=vw, sink=k["sink"])], nb, seq_len)
            x = _outffn_call(x, (y_ret, y_gqa, y_ssd, y_swa), mod5, w_out_b, w_ffn_in_b, w_ffn_out_b,
                             ln, l, row0, tpr)
        extras = None
        if not latent:
            extras = tuple(a.reshape(nb, depth, seq_len, 2, HEAD_DIM) for a in new_kv) + (
                states[0], states[1].reshape(nb, depth, 2, N_HEADS, HEAD_DIM, SSD_STATE))
        return x.reshape(nb, seq_len, D_MODEL), extras

    y_prompt, extras = run_group(x_prompt, nb_c, len_c, 0, False)
    y_sample, _ = run_group(x_sample, nb_l, len_l, 1, True)
    return (y_prompt, y_sample) + extras


def _expand_cache_layers(cache, fill):
    return [_expand_cache(cache[:, l], fill) for l in range(cache.shape[1])]
```

```python
import functools
import math

import jax
import jax.numpy as jnp
from jax import lax
from jax.experimental import pallas as pl
from jax.experimental.pallas import tpu as pltpu

F32 = jnp.float32
BF16 = jnp.bfloat16

D_MODEL = 1024
DEPTH = 4
HEAD_DIM = 64
N_HEADS = 4
GRID_W = 64
ROPE_THETA = 10000.0
SSD_STATE = 128
D_FF = 2816
WINDOW = 128
PAST_LEN = 256
ALPHA = (2.0 * DEPTH) ** 0.25
CHUNK = 256
TOKEN_TILE = 256
INPROJ_TILE = 512
ATTN_TILE = 512
VMEM_LIMIT = 56 * 1024 * 1024
LOG2E = math.log2(math.e)
QK_SCALE = HEAD_DIM ** -0.5 * LOG2E

M_RET, M_RG, M_GQA, M_SZ, M_XBC, MAIN_W = 0, 768, 1024, 1536, 1792, 2560
T_SWA, T_DT, TAIL_W = 0, 512, 640
F_XBC, F_RG, F_SZ, F_DT = 0, 768, 1024, 1280
F_WIDTH = 1408


def _silu(x):
    return x / (1.0 + jnp.exp(-x))


def _dot(a, b):
    return jnp.dot(a, b, preferred_element_type=F32)


def _dot_nt(a, b):
    return lax.dot_general(a, b, (((1,), (1,)), ((), ())), preferred_element_type=F32)


def _dot_tn(a, b):
    return lax.dot_general(a, b, (((0,), (0,)), ((), ())), preferred_element_type=F32)


def _seg_sum(x, seg):
    hi = x.astype(BF16)
    lo = (x - hi.astype(F32)).astype(BF16)
    return _dot(hi, seg) + _dot(lo, seg)


def _params(sem):
    return pltpu.CompilerParams(dimension_semantics=sem, vmem_limit_bytes=VMEM_LIMIT)


def _ada_kernel(c_ref, w_ref, b_ref, o_ref):
    s = _silu(c_ref[...])
    o_ref[...] = _dot(s.astype(BF16), w_ref[...].astype(BF16)) + b_ref[...]


def _ada_call(cond8, ada_w, ada_b):
    tn = D_MODEL
    nt = 6
    out = pl.pallas_call(
        _ada_kernel,
        out_shape=jax.ShapeDtypeStruct((DEPTH, nt, 8, tn), F32),
        grid=(DEPTH, nt),
        in_specs=[pl.BlockSpec((8, D_MODEL), lambda l, j: (0, 0)),
                  pl.BlockSpec((None, D_MODEL, tn), lambda l, j: (l, 0, j)),
                  pl.BlockSpec((None, None, 1, tn), lambda l, j: (l, j, 0, 0))],
        out_specs=pl.BlockSpec((None, None, 8, tn), lambda l, j: (l, j, 0, 0)),
        compiler_params=_params(("arbitrary", "arbitrary")),
        name="ada_mod",
    )(cond8, ada_w, ada_b.reshape(DEPTH, nt, 1, tn))
    return jnp.transpose(out, (0, 2, 1, 3)).reshape(DEPTH, 8, nt, 1, tn)


def _mod_spec(layer, row0, tiles_per_row, k):
    return pl.BlockSpec((None, None, None, 1, D_MODEL),
                        lambda i: (layer, row0 + i // tiles_per_row, k, 0, 0))


def _rope(x, cos, sin_signed):
    w = x.shape[-1]
    lane = lax.broadcasted_iota(jnp.int32, x.shape, 1)
    partner = jnp.where((lane % 32) < 16, pltpu.roll(x, w - 16, 1), pltpu.roll(x, 16, 1))
    return x * cos + partner * sin_signed


def _expand_q(q):
    lane = lax.broadcasted_iota(jnp.int32, (q.shape[0], 128), 1)
    low = lane < HEAD_DIM
    blocks = []
    for j in range(2):
        pair = q[:, 128 * j:128 * (j + 1)]
        blocks.append(jnp.where(low, pair, 0.0))
        blocks.append(jnp.where(low, pltpu.roll(pair, HEAD_DIM, 1), 0.0))
    return jnp.concatenate(blocks, axis=1).astype(BF16)


def _expand_kv(k, fill):
    lane = lax.broadcasted_iota(jnp.int32, k.shape, 1)
    low = lane < HEAD_DIM
    return jnp.concatenate([jnp.where(low, k, fill),
                            jnp.where(low, pltpu.roll(k, HEAD_DIM, 1), fill)], axis=1).astype(BF16)


def _inproj_kernel(*refs, rope, n_alias):
    x_ref, sh_ref, sc_ref, w_ref, wt_ref, qn, kn, seg = refs[:8]
    i = 8
    if rope:
        cos_ref, sin_ref = refs[8:10]
        i = 10
    i += n_alias
    f_o, r_o, qg_o, kg_o, vg_o, qw_o, kw_o, vw_o = refs[i:i + 8]
    h = (x_ref[...] * (1.0 + sc_ref[...]) + sh_ref[...]).astype(BF16)

    f_o[:, F_XBC:F_XBC + 768] = _dot(h, w_ref[:, M_XBC:M_XBC + 768])
    f_o[:, F_RG:F_RG + 256] = _dot(h, w_ref[:, M_RG:M_RG + 256])
    f_o[:, F_SZ:F_SZ + 256] = _dot(h, w_ref[:, M_SZ:M_SZ + 256])
    f_o[:, F_DT:F_DT + 128] = _dot(h, wt_ref[:, T_DT:T_DT + 128])
    ret = _dot(h, w_ref[:, M_RET:M_RET + 768])
    r_o[...] = jnp.concatenate([ret[:, 0:256], ret[:, 256:512] * HEAD_DIM ** -0.5, ret[:, 512:768]],
                               axis=1).astype(BF16)

    seg_m = seg[...]

    def rms(v, g, s):
        ms = _seg_sum(v * v, s) * (1.0 / HEAD_DIM)
        return v * lax.rsqrt(ms + 1e-6) * g

    gqa = _dot(h, w_ref[:, M_GQA:M_GQA + 512])
    swa = _dot(h, wt_ref[:, T_SWA:T_SWA + 512])
    q_g = rms(gqa[:, 0:256], qn[...], seg_m)
    k_g = rms(gqa[:, 256:384], kn[...][:, :128], seg_m[:128, :128])
    q_w = swa[:, 0:256]
    k_w = swa[:, 256:384]
    if rope:
        cos = cos_ref[...]
        sin = sin_ref[...]
        q_g = _rope(q_g, cos, sin)
        q_w = _rope(q_w, cos, sin)
        k_g = _rope(k_g, cos[:, :128], sin[:, :128])
        k_w = _rope(k_w, cos[:, :128], sin[:, :128])
    else:
        for o_ref, piece in zip(refs[i + 8:i + 12], (k_g, gqa[:, 384:512], swa[:, 256:384], swa[:, 384:512])):
            o_ref[...] = piece.reshape(o_ref.shape)
    qg_o[...] = _expand_q(q_g * QK_SCALE)
    qw_o[...] = _expand_q(q_w * QK_SCALE)
    kg_o[...] = _expand_kv(k_g, 0.0)
    kw_o[...] = _expand_kv(k_w, 0.0)
    vg_o[...] = _expand_kv(gqa[:, 384:512], 1.0)
    vw_o[...] = _expand_kv(swa[:, 384:512], 1.0)


def _inproj_call(x, mod5, w_main, w_tail, layer, row0, tiles_per_row, qn256, kn256, seg, rope_tabs, seq_len,
                 new_kv_prev=None):
    m = x.shape[0]
    tm = INPROJ_TILE
    tiles_per_row = tiles_per_row * TOKEN_TILE // tm
    rope = rope_tabs is not None
    const = lambda shape: pl.BlockSpec(shape, lambda i: (0, 0))
    in_specs = [pl.BlockSpec((tm, D_MODEL), lambda i: (i, 0)),
                _mod_spec(layer, row0, tiles_per_row, 0),
                _mod_spec(layer, row0, tiles_per_row, 1),
                pl.BlockSpec((None, D_MODEL, MAIN_W), lambda i: (layer, 0, 0)),
                pl.BlockSpec((None, D_MODEL, TAIL_W), lambda i: (layer, 0, 0)),
                const((1, 256)), const((1, 256)), const((256, 256))]
    args = [x, mod5, mod5, w_main, w_tail, qn256, kn256, seg]
    if rope:
        tpb = seq_len // tm
        tab = pl.BlockSpec((tm, 256), lambda i: (i % tpb, 0))
        in_specs += [tab, tab]
        args += list(rope_tabs)
    widths = [(F_WIDTH, F32), (768, BF16), (512, BF16), (256, BF16), (256, BF16),
              (512, BF16), (256, BF16), (256, BF16)]
    out_shape = [jax.ShapeDtypeStruct((m, w), dt) for w, dt in widths]
    out_specs = [pl.BlockSpec((tm, w), lambda i: (i, 0)) for w, _ in widths]
    aliases = {}
    if not rope:
        bpt = tm // seq_len
        out_shape += [jax.ShapeDtypeStruct((m // seq_len, DEPTH, seq_len, 128), F32)] * 4
        out_specs += [pl.BlockSpec((bpt, None, seq_len, 128), lambda i: (i, layer, 0, 0))] * 4
        if new_kv_prev is not None:
            aliases = {len(args) + t: len(widths) + t for t in range(4)}
            in_specs += [pl.BlockSpec(memory_space=pl.ANY)] * 4
            args += list(new_kv_prev)
    return pl.pallas_call(
        functools.partial(_inproj_kernel, rope=rope, n_alias=len(aliases)),
        out_shape=out_shape, grid=(m // tm,), in_specs=in_specs, out_specs=out_specs,
        input_output_aliases=aliases,
        compiler_params=_params(("arbitrary",)), name="in_proj",
    )(*args)


def _online(m, acc, s_blocks, v_blocks):
    m_new = m
    for s in s_blocks:
        m_new = jnp.maximum(m_new, jnp.max(s, axis=-1, keepdims=True))
    acc = jnp.exp2(m - m_new) * acc
    for s, v in zip(s_blocks, v_blocks):
        acc = acc + _dot(jnp.exp2(s - m_new).astype(BF16), v)
    return m_new, acc


def _attend(q_ref, k_ref, v_ref, kc_ref, vc_ref, sink_ref, o_ref, *, qi, seq_len, tq, kv_chunk, window):
    rows = 2 * tq
    q2 = [jnp.concatenate([q_ref[:, 256 * j:256 * j + 128], q_ref[:, 256 * j + 128:256 * (j + 1)]], axis=0)
          for j in range(2)]

    def update(carry, kv_blocks, masks):
        out = []
        for j in range(2):
            lanes = slice(128 * j, 128 * (j + 1))
            s_blocks = []
            for (k_blk, _), mask in zip(kv_blocks, masks):
                s = _dot_nt(q2[j], k_blk[:, lanes])
                s_blocks.append(s if mask is None else jnp.where(mask, s, -jnp.inf))
            out.append(_online(*carry[j], s_blocks, [v_blk[:, lanes] for _, v_blk in kv_blocks]))
        return tuple(out)

    if sink_ref is not None:
        upper = lax.broadcasted_iota(jnp.int32, (rows, 128), 1) >= HEAD_DIM
        carry = tuple((jnp.max(sink_ref[j], axis=-1, keepdims=True) * LOG2E, jnp.where(upper, 1.0, 0.0))
                      for j in range(2))
    else:
        carry = tuple((jnp.full((rows, 1), -jnp.inf, F32), jnp.zeros((rows, 128), F32)) for _ in range(2))
    if window:
        span = tq + 2 * WINDOW
        start = pl.multiple_of(jnp.clip(qi * tq - WINDOW, 0, seq_len - span), 128)
        qpos = qi * tq + lax.broadcasted_iota(jnp.int32, (rows, span), 0) % tq
        kpos = start + lax.broadcasted_iota(jnp.int32, (rows, span), 1)
        blocks = [(k_ref[pl.ds(start, span), :], v_ref[pl.ds(start, span), :])]
        masks = [jnp.abs(qpos - kpos) <= WINDOW]
        if kc_ref is not None:
            blocks.append((kc_ref[...], vc_ref[...]))
            masks.append(None)
        carry = update(carry, blocks, masks)
    else:
        for c in range(seq_len // kv_chunk):
            blk = slice(c * kv_chunk, (c + 1) * kv_chunk)
            carry = update(carry, [(k_ref[blk, :], v_ref[blk, :])], [None])
        if kc_ref is not None:
            carry = update(carry, [(kc_ref[...], vc_ref[...])], [None])

    low = lax.broadcasted_iota(jnp.int32, (tq, 128), 1) < HEAD_DIM
    for j in range(2):
        acc = carry[j][1]
        o = acc / pltpu.roll(acc, HEAD_DIM, 1)
        o_ref[:, 128 * j:128 * (j + 1)] = jnp.where(low, o[:tq], pltpu.roll(o[tq:], HEAD_DIM, 1)).astype(o_ref.dtype)


def _attn_kernel(*refs, seq_len, tq, kv_chunk, mixers):
    n_in = sum(3 + 2 * c + s for c, s, _ in mixers)
    outs = refs[n_in:]
    i = 0
    for o_ref, (has_cache, has_sink, window) in zip(outs, mixers):
        q_ref, k_ref, v_ref = refs[i:i + 3]
        i += 3
        kc_ref, vc_ref = refs[i:i + 2] if has_cache else (None, None)
        i += 2 * has_cache
        sink_ref = refs[i] if has_sink else None
        i += has_sink
        _attend(q_ref, k_ref, v_ref, kc_ref, vc_ref, sink_ref, o_ref, qi=pl.program_id(1),
                seq_len=seq_len, tq=tq, kv_chunk=kv_chunk, window=window)


def _attn_call(mixers, nb, seq_len):
    tq = min(ATTN_TILE, seq_len)
    nq = seq_len // tq
    kv_chunk = min(1024, seq_len)
    in_specs, args, static = [], [], []
    for mx in mixers:
        in_specs += [pl.BlockSpec((tq, 512), lambda b, i: (b * nq + i, 0)),
                     pl.BlockSpec((seq_len, 256), lambda b, i: (b, 0)),
                     pl.BlockSpec((seq_len, 256), lambda b, i: (b, 0))]
        args += [mx["q"], mx["k"], mx["v"]]
        if mx.get("cache") is not None:
            in_specs += [pl.BlockSpec((None, PAST_LEN, 256), lambda b, i: (b, 0, 0))] * 2
            args += list(mx["cache"])
        if mx.get("sink") is not None:
            in_specs.append(pl.BlockSpec((2, 2 * tq, 128), lambda b, i: (0, 0, 0)))
            args.append(mx["sink"])
        static.append((mx.get("cache") is not None, mx.get("sink") is not None, bool(mx.get("window"))))
    out = jax.ShapeDtypeStruct((nb * seq_len, 256), BF16)
    out_spec = pl.BlockSpec((tq, 256), lambda b, i: (b * nq + i, 0))
    return pl.pallas_call(
        functools.partial(_attn_kernel, seq_len=seq_len, tq=tq, kv_chunk=kv_chunk, mixers=tuple(static)),
        out_shape=[out] * len(mixers), grid=(nb, nq), in_specs=in_specs, out_specs=[out_spec] * len(mixers),
        compiler_params=_params(("arbitrary", "arbitrary")), name="attention",
    )(*args)


def _chunk_of(s, nc):
    return jnp.where(s < nc, nc - 1 - s, s - nc)


def _log_gamma(rd):
    return jnp.log1p(-jnp.exp(rd))


def _ret_kernel(*refs, nc, has_init, emit_state):
    q_ref, k_ref, v_ref, g_ref, rdl_ref, rdh_ref, gn_ref, seg_ref = refs[:8]
    i = 8
    if has_init:
        s0_ref = refs[i]
        i += 1
    y_ref = refs[i]
    i += 1
    if emit_state:
        st_ref = refs[i]
        i += 1
    m_sc, dec_sc, qk_sc, sf_sc, sb_sc, sball_sc = refs[i:i + 6]
    c_len = CHUNK
    b = pl.program_id(0)
    s = pl.program_id(1)

    @pl.when(jnp.logical_and(b == 0, s == 0))
    def _():
        ri = lax.broadcasted_iota(jnp.int32, (c_len, c_len), 0)
        ci = lax.broadcasted_iota(jnp.int32, (c_len, c_len), 1)
        d = (ri - ci).astype(F32)
        for h in range(N_HEADS):
            lgf = _log_gamma(jnp.broadcast_to(rdh_ref[0, h], (c_len, c_len)))
            lgb = _log_gamma(jnp.broadcast_to(rdh_ref[1, h], (c_len, c_len)))
            m_sc[h] = jnp.where(d > 0, jnp.exp(d * lgf), jnp.where(d < 0, jnp.exp(-d * lgb), 2.0))
        for dr in range(2):
            rows = [jnp.exp(c_len * _log_gamma(jnp.broadcast_to(rdh_ref[dr, h], (HEAD_DIM, c_len))))
                    for h in range(N_HEADS)]
            dec_sc[dr] = jnp.concatenate(rows, axis=0)
        a = lax.broadcasted_iota(jnp.int32, (c_len, 256), 0).astype(F32)
        lgf = _log_gamma(rdl_ref[0])
        lgb = _log_gamma(rdl_ref[1])
        qk_sc[0] = jnp.exp((a + 1.0) * lgf)
        qk_sc[1] = jnp.exp((c_len - 1.0 - a) * lgf)
        qk_sc[2] = jnp.exp((c_len - a) * lgb)
        qk_sc[3] = jnp.exp(a * lgb)

    @pl.when(s == 0)
    def _():
        sf_sc[...] = jnp.zeros_like(sf_sc)
        sb_sc[...] = jnp.zeros_like(sb_sc)
        if has_init:
            for h in range(N_HEADS):
                blk = slice(HEAD_DIM * h, HEAD_DIM * (h + 1))
                sf_sc[blk, blk] = s0_ref[0, h]
                sb_sc[blk, blk] = s0_ref[1, h]

    ri = lax.broadcasted_iota(jnp.int32, (256, 256), 0) // HEAD_DIM
    ci = lax.broadcasted_iota(jnp.int32, (256, 256), 1) // HEAD_DIM
    diag = ri == ci
    kb = k_ref[...]
    ks = kb.astype(F32)
    vb = v_ref[...]

    @pl.when(s < nc)
    def _():
        c = nc - 1 - s
        sball_sc[c] = sb_sc[...]
        upd = _dot_tn((ks * qk_sc[3]).astype(BF16), vb)
        sb_sc[...] = sb_sc[...] * dec_sc[1] + jnp.where(diag, upd, 0.0)
        if emit_state:
            @pl.when(s == nc - 1)
            def _():
                for h in range(N_HEADS):
                    blk = slice(HEAD_DIM * h, HEAD_DIM * (h + 1))
                    st_ref[1, h] = sb_sc[blk, blk]

    @pl.when(s >= nc)
    def _():
        c = s - nc
        qb = q_ref[...]
        q = qb.astype(F32)
        lane_head = lax.broadcasted_iota(jnp.int32, (c_len, 256), 1) // HEAD_DIM
        o = (_dot(qb, sf_sc[...].astype(BF16)) * qk_sc[0]
             + _dot(qb, sball_sc[c].astype(BF16)) * qk_sc[2])
        for h in range(N_HEADS):
            mine = lane_head == h
            sc = _dot_nt(jnp.where(mine, q, 0.0).astype(BF16), kb)
            pv = _dot((sc * m_sc[h]).astype(BF16), vb)
            o = o + jnp.where(mine, pv, 0.0)
        seg = seg_ref[...]
        mu = _seg_sum(o, seg) * (1.0 / HEAD_DIM)
        dlt = o - mu
        var = _seg_sum(dlt * dlt, seg) * (1.0 / HEAD_DIM)
        on = dlt * lax.rsqrt(var + 1e-5) * gn_ref[...]
        y_ref[...] = (on * _silu(g_ref[...])).astype(y_ref.dtype)
        upd = _dot_tn((ks * qk_sc[1]).astype(BF16), vb)
        sf_sc[...] = sf_sc[...] * dec_sc[0] + jnp.where(diag, upd, 0.0)
        if emit_state:
            @pl.when(s == 2 * nc - 1)
            def _():
                for h in range(N_HEADS):
                    blk = slice(HEAD_DIM * h, HEAD_DIM * (h + 1))
                    st_ref[0, h] = sf_sc[blk, blk]


def _ret_specs(rqkv, pf, nb, seq_len, rd_lane, rd_head, gn_g, seg, layer, state0=None, emit_state=False):
    c_len = CHUNK
    nc = seq_len // c_len

    def col(start):
        return pl.BlockSpec((c_len, 256), lambda b, s: (b * nc + _chunk_of(s, nc), start // 256))

    in_specs = [col(0), col(256), col(512), col(F_RG),
                pl.BlockSpec((2, 1, 256), lambda b, s: (0, 0, 0)),
                pl.BlockSpec((2, N_HEADS, 1, c_len), lambda b, s: (0, 0, 0, 0)),
                pl.BlockSpec((1, 256), lambda b, s: (0, 0)),
                pl.BlockSpec((256, 256), lambda b, s: (0, 0))]
    args = [rqkv, rqkv, rqkv, pf, rd_lane, rd_head, gn_g, seg]
    if state0 is not None:
        in_specs.append(pl.BlockSpec((None, None, 2, N_HEADS, HEAD_DIM, HEAD_DIM),
                                     lambda b, s: (b, layer, 0, 0, 0, 0)))
        args.append(state0)
    outs = [jax.ShapeDtypeStruct((nb * seq_len, 256), BF16)]
    out_specs = [pl.BlockSpec((c_len, 256), lambda b, s: (b * nc + jnp.maximum(s - nc, 0), 0))]
    if emit_state:
        outs.append(jax.ShapeDtypeStruct((nb, DEPTH, 2, N_HEADS, HEAD_DIM, HEAD_DIM), F32))
        out_specs.append(pl.BlockSpec((None, None, 2, N_HEADS, HEAD_DIM, HEAD_DIM),
                                      lambda b, s: (b, layer, 0, 0, 0, 0)))
    scratch = [pltpu.VMEM((N_HEADS, c_len, c_len), F32), pltpu.VMEM((2, 256, 256), F32),
               pltpu.VMEM((4, c_len, 256), F32), pltpu.VMEM((256, 256), F32),
               pltpu.VMEM((256, 256), F32), pltpu.VMEM((nc, 256, 256), F32)]
    return in_specs, args, outs, out_specs, scratch


def _cumsum_rows(x, reverse):
    n = x.shape[0]
    row = lax.broadcasted_iota(jnp.int32, x.shape, 0)
    sh = 1
    while sh < n:
        if reverse:
            x = x + jnp.where(row < n - sh, pltpu.roll(x, n - sh, 0), 0.0)
        else:
            x = x + jnp.where(row >= sh, pltpu.roll(x, sh, 0), 0.0)
        sh *= 2
    return x


def _ssd_kernel(*refs, nc, has_init, emit_state):
    xbc_ref, prev_ref, next_ref, z_ref, dt_ref, cw_ref, cb_ref, dtb_ref, al_ref, dl_ref, ng_ref = refs[:11]
    i = 11
    if has_init:
        h0_ref = refs[i]
        i += 1
    y_ref = refs[i]
    i += 1
    if emit_state:
        st_ref = refs[i]
        i += 1
    hf_sc, hb_sc, hball_sc, xs_sc, bm_sc, cm_sc, dt_sc, cum_sc, rev_sc = refs[i:i + 9]
    c_len = CHUNK
    s = pl.program_id(1)
    c = _chunk_of(s, nc)

    @pl.when(s == 0)
    def _():
        if has_init:
            hf_sc[...] = h0_ref[0]
            hb_sc[...] = h0_ref[1]
        else:
            hf_sc[...] = jnp.zeros_like(hf_sc)
            hb_sc[...] = jnp.zeros_like(hb_sc)

    low = lax.broadcasted_iota(jnp.int32, (c_len, 128), 1) < HEAD_DIM
    low_rows = lax.broadcasted_iota(jnp.int32, (128, 128), 0) < HEAD_DIM

    def col(a, lane):
        return jnp.broadcast_to(a[:, lane:lane + 1], (c_len, c_len))

    def pair(a0, a1):
        return jnp.where(low, a0[:, :128], a1[:, :128])

    def state_update(h_sc, g, xs, bm, dt, cols, tot_row, dt_lanes):
        w = [jnp.exp2(tot_row[r][:, :128] - cols[r][:, :128]) * col(dt, dt_lanes[r])[:, :128] for r in range(2)]
        xw = (xs[:, 128 * g:128 * (g + 1)] * jnp.where(low, w[0], w[1])).astype(BF16)
        dec = jnp.where(low_rows, jnp.broadcast_to(jnp.exp2(tot_row[0][:, :128]), (128, 128)),
                        jnp.broadcast_to(jnp.exp2(tot_row[1][:, :128]), (128, 128)))
        h_sc[g] = h_sc[g] * dec + _dot_tn(xw, bm[:, 128 * g:128 * (g + 1)])

    @pl.when(s < nc)
    def _():
        xbc = xbc_ref[...]
        row = lax.broadcasted_iota(jnp.int32, xbc.shape, 0)
        before = jnp.where(c > 0, prev_ref[7:8, :], 0.0)
        after = jnp.where(c < nc - 1, next_ref[0:1, :], 0.0)
        x_prev = jnp.where(row == 0, before, pltpu.roll(xbc, 1, 0))
        x_next = jnp.where(row == c_len - 1, after, pltpu.roll(xbc, c_len - 1, 0))
        conv = _silu(cw_ref[0:1, :] * x_prev + cw_ref[1:2, :] * xbc + cw_ref[2:3, :] * x_next + cb_ref[...])
        xs = conv[:, 0:256]
        bm = conv[:, 256:512].astype(BF16)
        pre = dt_ref[...] + dtb_ref[...]
        dt = jnp.maximum(pre, 0.0) + jnp.log1p(jnp.exp(-jnp.abs(pre)))
        dta = dt * (-jnp.exp(al_ref[...]) * LOG2E)
        rev = _cumsum_rows(dta, reverse=True)
        xs_sc[c] = xs
        bm_sc[c] = bm
        cm_sc[c] = conv[:, 512:768].astype(BF16)
        dt_sc[c] = dt
        cum_sc[c] = _cumsum_rows(dta, reverse=False)
        rev_sc[c] = rev
        hball_sc[c] = hb_sc[...]
        for g in range(2):
            cols = [col(rev, 4 + 2 * g + r) for r in range(2)]
            tot = [cl[0:1, :] for cl in cols]
            state_update(hb_sc, g, xs, bm, dt, cols, tot, [4 + 2 * g, 5 + 2 * g])
        if emit_state:
            @pl.when(s == nc - 1)
            def _():
                st_ref[1] = hb_sc[...]

    @pl.when(s >= nc)
    def _():
        xs = xs_sc[c]
        bm = bm_sc[c]
        cm = cm_sc[c]
        dt = dt_sc[c]
        cum = cum_sc[c]
        rev = rev_sc[c]
        log_dt = jnp.log2(dt)
        off_f = (log_dt - cum).T
        off_b = (log_dt - rev).T
        dt_t = dt.T
        ri = lax.broadcasted_iota(jnp.int32, (c_len, c_len), 0)
        ci = lax.broadcasted_iota(jnp.int32, (c_len, c_len), 1)
        below = ri > ci
        on_diag = ri == ci
        lane_head = lax.broadcasted_iota(jnp.int32, (c_len, 256), 1) // HEAD_DIM
        xs_b = xs.astype(BF16)
        y = jnp.zeros((c_len, 256), F32)
        inter = []
        for g in range(2):
            gl = slice(128 * g, 128 * (g + 1))
            gmat = _dot_nt(cm[:, gl], bm[:, gl])
            cols_f, cols_b = [], []
            for r in range(2):
                h = 2 * g + r
                cf = col(cum, h)
                cb = col(rev, 4 + h)
                cols_f.append(cf)
                cols_b.append(cb)
                w = jnp.exp2(jnp.where(below, cf + off_f[h:h + 1, :], cb + off_b[4 + h:5 + h, :]))
                w = w + jnp.where(on_diag, dt_t[h:h + 1, :], 0.0)
                yh = _dot((gmat * w).astype(BF16), xs_b)
                y = y + jnp.where(lane_head == h, yh, 0.0)
            yf = _dot_nt(cm[:, gl], hf_sc[g].astype(BF16))
            yb = _dot_nt(cm[:, gl], hball_sc[c, g].astype(BF16))
            inter.append(yf * pair(jnp.exp2(cols_f[0]), jnp.exp2(cols_f[1]))
                         + yb * pair(jnp.exp2(cols_b[0]), jnp.exp2(cols_b[1])))
            tot = [cl[c_len - 1:c_len, :] for cl in cols_f]
            state_update(hf_sc, g, xs, bm, dt, cols_f, tot, [2 * g, 2 * g + 1])
        y = y + jnp.concatenate(inter, axis=1) + xs * dl_ref[...]
        y = y * _silu(z_ref[...])
        ms = jnp.mean(y * y, axis=-1, keepdims=True)
        y_ref[...] = (y * lax.rsqrt(ms + 1e-6) * ng_ref[...]).astype(y_ref.dtype)
        if emit_state:
            @pl.when(s == 2 * nc - 1)
            def _():
                st_ref[0] = hf_sc[...]


def _ssd_specs(p, nb, seq_len, conv_w, conv_b, dt_bias, a_log, d_lane, norm_g, layer,
              state0=None, emit_state=False):
    c_len = CHUNK
    nc = seq_len // c_len
    m = nb * seq_len
    r8 = c_len // 8
    chunk = lambda b, s: b * nc + jnp.maximum(nc - 1 - s, 0)
    in_specs = [pl.BlockSpec((c_len, 768), lambda b, s: (chunk(b, s), F_XBC // 768)),
                pl.BlockSpec((8, 768), lambda b, s: (jnp.maximum(chunk(b, s) * r8 - 1, 0), F_XBC // 768)),
                pl.BlockSpec((8, 768), lambda b, s: (jnp.minimum((chunk(b, s) + 1) * r8, m // 8 - 1),
                                                     F_XBC // 768)),
                pl.BlockSpec((c_len, 256), lambda b, s: (b * nc + jnp.maximum(s - nc, 0), F_SZ // 256)),
                pl.BlockSpec((c_len, 128), lambda b, s: (chunk(b, s), F_DT // 128)),
                pl.BlockSpec((None, 3, 768), lambda b, s: (layer, 0, 0)),
                pl.BlockSpec((1, 768), lambda b, s: (0, 0)),
                pl.BlockSpec((1, 128), lambda b, s: (0, 0)),
                pl.BlockSpec((1, 128), lambda b, s: (0, 0)),
                pl.BlockSpec((1, 256), lambda b, s: (0, 0)),
                pl.BlockSpec((1, 256), lambda b, s: (0, 0))]
    args = [p, p, p, p, p, conv_w, conv_b, dt_bias, a_log, d_lane, norm_g]
    if state0 is not None:
        in_specs.append(pl.BlockSpec((None, None, 2, 2, 128, SSD_STATE),
                                     lambda b, s: (b, layer, 0, 0, 0, 0)))
        args.append(state0)
    outs = [jax.ShapeDtypeStruct((m, 256), BF16)]
    out_specs = [pl.BlockSpec((c_len, 256), lambda b, s: (b * nc + jnp.maximum(s - nc, 0), 0))]
    if emit_state:
        outs.append(jax.ShapeDtypeStruct((nb, DEPTH, 2, 2, 128, SSD_STATE), F32))
        out_specs.append(pl.BlockSpec((None, None, 2, 2, 128, SSD_STATE),
                                      lambda b, s: (b, layer, 0, 0, 0, 0)))
    scratch = [pltpu.VMEM((2, 128, SSD_STATE), F32), pltpu.VMEM((2, 128, SSD_STATE), F32),
               pltpu.VMEM((nc, 2, 128, SSD_STATE), F32),
               pltpu.VMEM((nc, c_len, 256), F32), pltpu.VMEM((nc, c_len, 256), BF16),
               pltpu.VMEM((nc, c_len, 256), BF16), pltpu.VMEM((nc, c_len, 128), F32),
               pltpu.VMEM((nc, c_len, 128), F32), pltpu.VMEM((nc, c_len, 128), F32)]
    return in_specs, args, outs, out_specs, scratch


def _scan_kernel(*refs, counts, **static):
    (ri, si), (ro, so), (rs, ss), n_alias = counts
    ins, rest = refs[:ri + si], refs[ri + si + n_alias:]
    outs, scr = rest[:ro + so], rest[ro + so:]
    _ret_kernel(*ins[:ri], *outs[:ro], *scr[:rs], **static)
    _ssd_kernel(*ins[ri:], *outs[ro:], *scr[rs:], **static)


def _scan_call(ret_parts, ssd_parts, nb, seq_len, has_init, emit_state, states_prev=None):
    nc = seq_len // CHUNK
    in_specs = ret_parts[0] + ssd_parts[0]
    args = ret_parts[1] + ssd_parts[1]
    aliases = {}
    if states_prev is not None:
        aliases = {len(args): 1, len(args) + 1: len(ret_parts[2]) + 1}
        in_specs = in_specs + [pl.BlockSpec(memory_space=pl.ANY)] * 2
        args = args + list(states_prev)
    counts = tuple((len(a), len(b)) for a, b in zip((ret_parts[0], ret_parts[2], ret_parts[4]),
                                                    (ssd_parts[0], ssd_parts[2], ssd_parts[4])))
    return pl.pallas_call(
        functools.partial(_scan_kernel, counts=counts + (len(aliases),), nc=nc, has_init=has_init,
                          emit_state=emit_state),
        out_shape=ret_parts[2] + ssd_parts[2], grid=(nb, 2 * nc),
        in_specs=in_specs, out_specs=ret_parts[3] + ssd_parts[3],
        scratch_shapes=ret_parts[4] + ssd_parts[4], input_output_aliases=aliases,
        compiler_params=_params(("arbitrary", "arbitrary")), name="scans",
    )(*args)


def _layer_norm(x, g, b):
    mu = jnp.mean(x, axis=-1, keepdims=True)
    d = x - mu
    var = jnp.mean(d * d, axis=-1, keepdims=True)
    return d * lax.rsqrt(var + 1e-5) * g + b


FFN_CHUNK = 256
FFN_STREAMS = 1
FFN_ROWS = 512


def _outffn_kernel(x_ref, yr_ref, yg_ref, ys_ref, yw_ref, g1_ref, sh2_ref, sc2_ref, g2_ref,
                   wo_ref, wi_ref, wf_ref, l1g_ref, l1b_ref, l2g_ref, l2b_ref, o_ref):
    n_chunks = D_FF // FFN_CHUNK
    half = n_chunks // 2

    def head(r):
        rows = slice(FFN_ROWS * r, FFN_ROWS * (r + 1))
        mix = (_dot(yr_ref[rows, :], wo_ref[0:256, :]) + _dot(yg_ref[rows, :], wo_ref[256:512, :])
               + _dot(ys_ref[rows, :], wo_ref[512:768, :]) + _dot(yw_ref[rows, :], wo_ref[768:1024, :]))
        x1 = _layer_norm(ALPHA * x_ref[rows, :] + g1_ref[...] * mix, l1g_ref[...], l1b_ref[...])
        return x1, (x1 * (1.0 + sc2_ref[...]) + sh2_ref[...]).astype(BF16)

    def ffn(h2, acc, chunks):
        for j in chunks:
            cols = slice(FFN_CHUNK * j, FFN_CHUNK * (j + 1))
            gate = _dot(h2, wi_ref[:, cols])
            up = _dot(h2, wi_ref[:, D_FF + FFN_CHUNK * j:D_FF + FFN_CHUNK * (j + 1)])
            d = _dot((_silu(gate) * up).astype(BF16), wf_ref[cols, :])
            acc = d if acc is None else acc + d
        return acc

    def tail(r, x1, acc):
        rows = slice(FFN_ROWS * r, FFN_ROWS * (r + 1))
        o_ref[rows, :] = _layer_norm(ALPHA * x1 + g2_ref[...] * acc, l2g_ref[...], l2b_ref[...])

    state = [None] * FFN_STREAMS
    x1, h2 = head(0)
    state[0] = (x1, h2, ffn(h2, None, range(0, half)))
    for r in range(FFN_STREAMS):
        x1, h2, acc = state[r]
        if r + 1 < FFN_STREAMS:
            nx1, nh2 = head(r + 1)
        acc = ffn(h2, acc, range(half, n_chunks))
        if r + 1 < FFN_STREAMS:
            state[r + 1] = (nx1, nh2, ffn(nh2, None, range(0, half)))
        tail(r, x1, acc)


def _outffn_call(x, ys, mod5, w_out_b, w_ffn_in_b, w_ffn_out_b, ln, layer, row0, tiles_per_row):
    m = x.shape[0]
    tm = FFN_ROWS * FFN_STREAMS
    tiles_per_row = tiles_per_row * TOKEN_TILE // tm
    row = lambda w: pl.BlockSpec((tm, w), lambda i: (i, 0))
    once = pl.Buffered(1)
    vec = pl.BlockSpec((None, 1, D_MODEL), lambda i: (layer, 0, 0))
    in_specs = [row(D_MODEL), row(256), row(256), row(256), row(256),
                _mod_spec(layer, row0, tiles_per_row, 2), _mod_spec(layer, row0, tiles_per_row, 3),
                _mod_spec(layer, row0, tiles_per_row, 4), _mod_spec(layer, row0, tiles_per_row, 5),
                pl.BlockSpec((None, D_MODEL, D_MODEL), lambda i: (layer, 0, 0), pipeline_mode=once),
                pl.BlockSpec((None, D_MODEL, 2 * D_FF), lambda i: (layer, 0, 0), pipeline_mode=once),
                pl.BlockSpec((None, D_FF, D_MODEL), lambda i: (layer, 0, 0), pipeline_mode=once),
                vec, vec, vec, vec]
    return pl.pallas_call(
        _outffn_kernel,
        out_shape=jax.ShapeDtypeStruct((m, D_MODEL), F32),
        grid=(m // tm,), in_specs=in_specs, out_specs=row(D_MODEL),
        compiler_params=_params(("arbitrary",)), name="out_ffn",
    )(x, *ys, mod5, mod5, mod5, mod5, w_out_b, w_ffn_in_b, w_ffn_out_b, *ln)


def _rope_tables(seq_len):
    t = jnp.arange(seq_len)
    rowp = (t // GRID_W).astype(F32)
    colp = (t % GRID_W).astype(F32)
    quarter = HEAD_DIM // 4
    inv_freq = ROPE_THETA ** (-jnp.arange(quarter, dtype=F32) / quarter)
    ang_r = rowp[:, None] * inv_freq[None, :]
    ang_c = colp[:, None] * inv_freq[None, :]
    cos = jnp.concatenate([jnp.cos(ang_r)] * 2 + [jnp.cos(ang_c)] * 2, axis=-1)
    sin = jnp.concatenate([-jnp.sin(ang_r), jnp.sin(ang_r), -jnp.sin(ang_c), jnp.sin(ang_c)], axis=-1)
    return jnp.tile(cos, (1, N_HEADS)), jnp.tile(sin, (1, N_HEADS))


def _expand_cache(c, fill):
    pad = jnp.full(c.shape[:2] + (HEAD_DIM,), fill, c.dtype)
    return jnp.concatenate([c[:, :, 0], pad, c[:, :, 1], pad], axis=-1).astype(BF16)


def _split_w_in(w):
    pad = jnp.zeros(w.shape[:-1] + (TAIL_W - T_DT - 8,), w.dtype)
    tail = jnp.concatenate([w[..., MAIN_W + 8:], w[..., MAIN_W:MAIN_W + 8], pad], axis=-1)
    return w[..., :MAIN_W].astype(BF16), tail.astype(BF16)


def kernel(x_prompt, x_sample, cache_gqa_k, cache_gqa_v, cache_swa_k, cache_swa_v, state_ret, state_ssd,
           c, c_ctx, ada_w, ada_b, w_in, ret_decay, ret_gn_g, gqa_q_norm, gqa_k_norm,
           ssd_conv_w, ssd_conv_b, ssd_dt_bias, ssd_a_log, ssd_d, ssd_norm_g, swa_sink, w_out,
           ln1_g, ln1_b, w_ffn_in, w_ffn_out, ln2_g, ln2_b):
    nb_c, len_c, _ = x_prompt.shape
    nb_l, len_l, _ = x_sample.shape
    depth = ada_w.shape[0]

    cond8 = jnp.zeros((8, D_MODEL), F32).at[0].set(c_ctx).at[1:1 + nb_l].set(c)
    mod5 = _ada_call(cond8, ada_w, ada_b)

    w_main, w_tail = _split_w_in(w_in)
    w_out_b = w_out.astype(BF16)
    w_ffn_in_b = w_ffn_in.astype(BF16)
    w_ffn_out_b = w_ffn_out.astype(BF16)
    lane_head = jnp.arange(256) // HEAD_DIM
    seg = (lane_head[:, None] == lane_head[None, :]).astype(BF16)
    rope_tabs = _rope_tables(len_l)
    caches_g = (_expand_cache_layers(cache_gqa_k, 0.0), _expand_cache_layers(cache_gqa_v, 1.0))
    caches_w = (_expand_cache_layers(cache_swa_k, 0.0), _expand_cache_layers(cache_swa_v, 1.0))
    state_ssd_g = state_ssd.reshape(nb_l, depth, 2, 2, 128, SSD_STATE)

    def layer_consts(l):
        rd = ret_decay[l]
        return dict(
            rd_lane=jnp.repeat(rd, HEAD_DIM, axis=-1).reshape(2, 1, 256),
            rd_head=jnp.broadcast_to(rd[:, :, None, None], (2, N_HEADS, 1, CHUNK)),
            gn_g=ret_gn_g[l].reshape(1, 256),
            qn=jnp.tile(gqa_q_norm[l], N_HEADS).reshape(1, 256),
            kn=jnp.tile(gqa_k_norm[l], N_HEADS).reshape(1, 256),
            conv_b=ssd_conv_b[l].reshape(1, 768),
            dt_bias=jnp.zeros((1, 128), F32).at[0, :8].set(ssd_dt_bias[l].reshape(8)),
            a_log=jnp.zeros((1, 128), F32).at[0, :8].set(ssd_a_log[l].reshape(8)),
            d_lane=jnp.repeat(ssd_d[l], HEAD_DIM).reshape(1, 256),
            norm_g=ssd_norm_g[l].reshape(1, 256),
            sink=lambda tq: jnp.broadcast_to(swa_sink[l].reshape(2, 2, 1, 1), (2, 2, tq, 128)
                                             ).reshape(2, 2 * tq, 128),
        )

    ln = tuple(a.reshape(depth, 1, D_MODEL) for a in (ln1_g, ln1_b, ln2_g, ln2_b))

    def run_group(x, nb, seq_len, row0, latent):
        m = nb * seq_len
        tpr = (seq_len // TOKEN_TILE) if latent else (m // TOKEN_TILE)
        x = x.reshape(m, D_MODEL)
        new_kv = states = None
        for l in range(depth):
            k = layer_consts(l)
            proj = _inproj_call(x, mod5, w_main, w_tail, l, row0, tpr, k["qn"], k["kn"], seg,
                                rope_tabs if latent else None, seq_len, new_kv_prev=new_kv)
            pf, rqkv, qg, kg, vg, qw, kw, vw = proj[:8]
            ret_parts = _ret_specs(rqkv, pf, nb, seq_len, k["rd_lane"], k["rd_head"], k["gn_g"], seg, l,
                                   state0=state_ret if latent else None, emit_state=not latent)
            ssd_parts = _ssd_specs(pf, nb, seq_len, ssd_conv_w, k["conv_b"], k["dt_bias"], k["a_log"],
                                   k["d_lane"], k["norm_g"], l,
                                   state0=state_ssd_g if latent else None, emit_state=not latent)
            scans = _scan_call(ret_parts, ssd_parts, nb, seq_len, has_init=latent, emit_state=not latent,
                               states_prev=states)
            if latent:
                y_ret, y_ssd = scans
                y_gqa, y_swa = _attn_call(
                    [dict(q=qg, k=kg, v=vg, cache=(caches_g[0][l], caches_g[1][l])),
                     dict(q=qw, k=kw, v=vw, cache=(caches_w[0][l], caches_w[1][l]),
                          sink=k["sink"](min(ATTN_TILE, seq_len)), window=True)], nb, seq_len)
            else:
                y_ret, st_ret, y_ssd, st_ssd = scans
                states = (st_ret, st_ssd)
                new_kv = proj[8:12]
                y_gqa, = _attn_call([dict(q=qg, k=kg, v=vg)], nb, seq_len)
                y_swa, = _attn_call([dict(q=qw, k=kw, v=vw, sink=k["sink"](min(ATTN_TILE, seq_len)))],
                                    nb, seq_len)
            x = _outffn_call(x, (y_ret, y_gqa, y_ssd, y_swa), mod5, w_out_b, w_ffn_in_b, w_ffn_out_b,
                             ln, l, row0, tpr)
        extras = None
        if not latent:
            extras = tuple(a.reshape(nb, depth, seq_len, 2, HEAD_DIM) for a in new_kv) + (
                states[0], states[1].reshape(nb, depth, 2, N_HEADS, HEAD_DIM, SSD_STATE))
        return x.reshape(nb, seq_len, D_MODEL), extras

    y_prompt, extras = run_group(x_prompt, nb_c, len_c, 0, False)
    y_sample, _ = run_group(x_sample, nb_l, len_l, 1, True)
    return (y_prompt, y_sample) + extras


def _expand_cache_layers(cache, fill):
    return [_expand_cache(cache[:, l], fill) for l in range(cache.shape[1])]
```

```python
import functools
import math

import jax
import jax.numpy as jnp
from jax import lax
from jax.experimental import pallas as pl
from jax.experimental.pallas import tpu as pltpu

F32 = jnp.float32
BF16 = jnp.bfloat16

D_MODEL = 1024
DEPTH = 4
HEAD_DIM = 64
N_HEADS = 4
GRID_W = 64
ROPE_THETA = 10000.0
SSD_STATE = 128
D_FF = 2816
WINDOW = 128
PAST_LEN = 256
ALPHA = (2.0 * DEPTH) ** 0.25
CHUNK = 256
TOKEN_TILE = 256
INPROJ_TILE = 512
ATTN_TILE = 512
VMEM_LIMIT = 56 * 1024 * 1024
LOG2E = math.log2(math.e)
QK_SCALE = HEAD_DIM ** -0.5 * LOG2E

M_RET, M_RG, M_GQA, M_SZ, M_XBC, MAIN_W = 0, 768, 1024, 1536, 1792, 2560
T_SWA, T_DT, TAIL_W = 0, 512, 640
F_XBC, F_RG, F_SZ, F_DT = 0, 768, 1024, 1280
F_WIDTH = 1408


def _silu(x):
    return x / (1.0 + jnp.exp(-x))


def _dot(a, b):
    return jnp.dot(a, b, preferred_element_type=F32)


def _dot_nt(a, b):
    return lax.dot_general(a, b, (((1,), (1,)), ((), ())), preferred_element_type=F32)


def _dot_tn(a, b):
    return lax.dot_general(a, b, (((0,), (0,)), ((), ())), preferred_element_type=F32)


def _seg_sum(x, seg):
    hi = x.astype(BF16)
    lo = (x - hi.astype(F32)).astype(BF16)
    return _dot(hi, seg) + _dot(lo, seg)


def _params(sem):
    return pltpu.CompilerParams(dimension_semantics=sem, vmem_limit_bytes=VMEM_LIMIT)


def _ada_kernel(c_ref, w_ref, b_ref, o_ref):
    s = _silu(c_ref[...])
    o_ref[...] = _dot(s.astype(BF16), w_ref[...].astype(BF16)) + b_ref[...]


def _ada_call(cond8, ada_w, ada_b):
    tn = D_MODEL
    nt = 6
    out = pl.pallas_call(
        _ada_kernel,
        out_shape=jax.ShapeDtypeStruct((DEPTH, nt, 8, tn), F32),
        grid=(DEPTH, nt),
        in_specs=[pl.BlockSpec((8, D_MODEL), lambda l, j: (0, 0)),
                  pl.BlockSpec((None, D_MODEL, tn), lambda l, j: (l, 0, j)),
                  pl.BlockSpec((None, None, 1, tn), lambda l, j: (l, j, 0, 0))],
        out_specs=pl.BlockSpec((None, None, 8, tn), lambda l, j: (l, j, 0, 0)),
        compiler_params=_params(("arbitrary", "arbitrary")),
        name="ada_mod",
    )(cond8, ada_w, ada_b.reshape(DEPTH, nt, 1, tn))
    return jnp.transpose(out, (0, 2, 1, 3)).reshape(DEPTH, 8, nt, 1, tn)


def _mod_spec(layer, row0, tiles_per_row, k):
    return pl.BlockSpec((None, None, None, 1, D_MODEL),
                        lambda i: (layer, row0 + i // tiles_per_row, k, 0, 0))


def _rope(x, cos, sin_signed):
    w = x.shape[-1]
    lane = lax.broadcasted_iota(jnp.int32, x.shape, 1)
    partner = jnp.where((lane % 32) < 16, pltpu.roll(x, w - 16, 1), pltpu.roll(x, 16, 1))
    return x * cos + partner * sin_signed


def _expand_q(q):
    lane = lax.broadcasted_iota(jnp.int32, (q.shape[0], 128), 1)
    low = lane < HEAD_DIM
    blocks = []
    for j in range(2):
        pair = q[:, 128 * j:128 * (j + 1)]
        blocks.append(jnp.where(low, pair, 0.0))
        blocks.append(jnp.where(low, pltpu.roll(pair, HEAD_DIM, 1), 0.0))
    return jnp.concatenate(blocks, axis=1).astype(BF16)


def _expand_kv(k, fill):
    lane = lax.broadcasted_iota(jnp.int32, k.shape, 1)
    low = lane < HEAD_DIM
    return jnp.concatenate([jnp.where(low, k, fill),
                            jnp.where(low, pltpu.roll(k, HEAD_DIM, 1), fill)], axis=1).astype(BF16)


def _inproj_kernel(*refs, rope, n_alias):
    x_ref, sh_ref, sc_ref, w_ref, wt_ref, qn, kn, seg = refs[:8]
    i = 8
    if rope:
        cos_ref, sin_ref = refs[8:10]
        i = 10
    i += n_alias
    f_o, r_o, qg_o, kg_o, vg_o, qw_o, kw_o, vw_o = refs[i:i + 8]
    h = (x_ref[...] * (1.0 + sc_ref[...]) + sh_ref[...]).astype(BF16)

    f_o[:, F_XBC:F_XBC + 768] = _dot(h, w_ref[:, M_XBC:M_XBC + 768])
    f_o[:, F_RG:F_RG + 256] = _dot(h, w_ref[:, M_RG:M_RG + 256])
    f_o[:, F_SZ:F_SZ + 256] = _dot(h, w_ref[:, M_SZ:M_SZ + 256])
    f_o[:, F_DT:F_DT + 128] = _dot(h, wt_ref[:, T_DT:T_DT + 128])
    ret = _dot(h, w_ref[:, M_RET:M_RET + 768])
    r_o[...] = jnp.concatenate([ret[:, 0:256], ret[:, 256:512] * HEAD_DIM ** -0.5, ret[:, 512:768]],
                               axis=1).astype(BF16)

    seg_m = seg[...]

    def rms(v, g, s):
        ms = _seg_sum(v * v, s) * (1.0 / HEAD_DIM)
        return v * lax.rsqrt(ms + 1e-6) * g

    gqa = _dot(h, w_ref[:, M_GQA:M_GQA + 512])
    swa = _dot(h, wt_ref[:, T_SWA:T_SWA + 512])
    q_g = rms(gqa[:, 0:256], qn[...], seg_m)
    k_g = rms(gqa[:, 256:384], kn[...][:, :128], seg_m[:128, :128])
    q_w = swa[:, 0:256]
    k_w = swa[:, 256:384]
    if rope:
        cos = cos_ref[...]
        sin = sin_ref[...]
        q_g = _rope(q_g, cos, sin)
        q_w = _rope(q_w, cos, sin)
        k_g = _rope(k_g, cos[:, :128], sin[:, :128])
        k_w = _rope(k_w, cos[:, :128], sin[:, :128])
    else:
        for o_ref, piece in zip(refs[i + 8:i + 12], (k_g, gqa[:, 384:512], swa[:, 256:384], swa[:, 384:512])):
            o_ref[...] = piece.reshape(o_ref.shape)
    qg_o[...] = _expand_q(q_g * QK_SCALE)
    qw_o[...] = _expand_q(q_w * QK_SCALE)
    kg_o[...] = _expand_kv(k_g, 0.0)
    kw_o[...] = _expand_kv(k_w, 0.0)
    vg_o[...] = _expand_kv(gqa[:, 384:512], 1.0)
    vw_o[...] = _expand_kv(swa[:, 384:512], 1.0)


def _inproj_call(x, mod5, w_main, w_tail, layer, row0, tiles_per_row, qn256, kn256, seg, rope_tabs, seq_len,
                 new_kv_prev=None):
    m = x.shape[0]
    tm = INPROJ_TILE
    tiles_per_row = tiles_per_row * TOKEN_TILE // tm
    rope = rope_tabs is not None
    const = lambda shape: pl.BlockSpec(shape, lambda i: (0, 0))
    in_specs = [pl.BlockSpec((tm, D_MODEL), lambda i: (i, 0)),
                _mod_spec(layer, row0, tiles_per_row, 0),
                _mod_spec(layer, row0, tiles_per_row, 1),
                pl.BlockSpec((None, D_MODEL, MAIN_W), lambda i: (layer, 0, 0)),
                pl.BlockSpec((None, D_MODEL, TAIL_W), lambda i: (layer, 0, 0)),
                const((1, 256)), const((1, 256)), const((256, 256))]
    args = [x, mod5, mod5, w_main, w_tail, qn256, kn256, seg]
    if rope:
        tpb = seq_len // tm
        tab = pl.BlockSpec((tm, 256), lambda i: (i % tpb, 0))
        in_specs += [tab, tab]
        args += list(rope_tabs)
    widths = [(F_WIDTH, F32), (768, BF16), (512, BF16), (256, BF16), (256, BF16),
              (512, BF16), (256, BF16), (256, BF16)]
    out_shape = [jax.ShapeDtypeStruct((m, w), dt) for w, dt in widths]
    out_specs = [pl.BlockSpec((tm, w), lambda i: (i, 0)) for w, _ in widths]
    aliases = {}
    if not rope:
        bpt = tm // seq_len
        out_shape += [jax.ShapeDtypeStruct((m // seq_len, DEPTH, seq_len, 128), F32)] * 4
        out_specs += [pl.BlockSpec((bpt, None, seq_len, 128), lambda i: (i, layer, 0, 0))] * 4
        if new_kv_prev is not None:
            aliases = {len(args) + t: len(widths) + t for t in range(4)}
            in_specs += [pl.BlockSpec(memory_space=pl.ANY)] * 4
            args += list(new_kv_prev)
    return pl.pallas_call(
        functools.partial(_inproj_kernel, rope=rope, n_alias=len(aliases)),
        out_shape=out_shape, grid=(m // tm,), in_specs=in_specs, out_specs=out_specs,
        input_output_aliases=aliases,
        compiler_params=_params(("arbitrary",)), name="in_proj",
    )(*args)


def _online(m, acc, s_blocks, v_blocks):
    m_new = m
    for s in s_blocks:
        m_new = jnp.maximum(m_new, jnp.max(s, axis=-1, keepdims=True))
    acc = jnp.exp2(m - m_new) * acc
    for s, v in zip(s_blocks, v_blocks):
        acc = acc + _dot(jnp.exp2(s - m_new).astype(BF16), v)
    return m_new, acc


def _attend(q_ref, k_ref, v_ref, kc_ref, vc_ref, sink_ref, o_ref, *, qi, seq_len, tq, kv_chunk, window):
    rows = 2 * tq
    q2 = [jnp.concatenate([q_ref[:, 256 * j:256 * j + 128], q_ref[:, 256 * j + 128:256 * (j + 1)]], axis=0)
          for j in range(2)]

    def update(carry, kv_blocks, masks):
        out = []
        for j in range(2):
            lanes = slice(128 * j, 128 * (j + 1))
            s_blocks = []
            for (k_blk, _), mask in zip(kv_blocks, masks):
                s = _dot_nt(q2[j], k_blk[:, lanes])
                s_blocks.append(s if mask is None else jnp.where(mask, s, -jnp.inf))
            out.append(_online(*carry[j], s_blocks, [v_blk[:, lanes] for _, v_blk in kv_blocks]))
        return tuple(out)

    if sink_ref is not None:
        upper = lax.broadcasted_iota(jnp.int32, (rows, 128), 1) >= HEAD_DIM
        carry = tuple((jnp.max(sink_ref[j], axis=-1, keepdims=True) * LOG2E, jnp.where(upper, 1.0, 0.0))
                      for j in range(2))
    else:
        carry = tuple((jnp.full((rows, 1), -jnp.inf, F32), jnp.zeros((rows, 128), F32)) for _ in range(2))
    if window:
        span = tq + 2 * WINDOW
        start = pl.multiple_of(jnp.clip(qi * tq - WINDOW, 0, seq_len - span), 128)
        qpos = qi * tq + lax.broadcasted_iota(jnp.int32, (rows, span), 0) % tq
        kpos = start + lax.broadcasted_iota(jnp.int32, (rows, span), 1)
        blocks = [(k_ref[pl.ds(start, span), :], v_ref[pl.ds(start, span), :])]
        masks = [jnp.abs(qpos - kpos) <= WINDOW]
        if kc_ref is not None:
            blocks.append((kc_ref[...], vc_ref[...]))
            masks.append(None)
        carry = update(carry, blocks, masks)
    else:
        for c in range(seq_len // kv_chunk):
            blk = slice(c * kv_chunk, (c + 1) * kv_chunk)
            carry = update(carry, [(k_ref[blk, :], v_ref[blk, :])], [None])
        if kc_ref is not None:
            carry = update(carry, [(kc_ref[...], vc_ref[...])], [None])

    low = lax.broadcasted_iota(jnp.int32, (tq, 128), 1) < HEAD_DIM
    for j in range(2):
        acc = carry[j][1]
        o = acc / pltpu.roll(acc, HEAD_DIM, 1)
        o_ref[:, 128 * j:128 * (j + 1)] = jnp.where(low, o[:tq], pltpu.roll(o[tq:], HEAD_DIM, 1)).astype(o_ref.dtype)


def _attn_kernel(*refs, seq_len, tq, kv_chunk, mixers):
    n_in = sum(3 + 2 * c + s for c, s, _ in mixers)
    outs = refs[n_in:]
    i = 0
    for o_ref, (has_cache, has_sink, window) in zip(outs, mixers):
        q_ref, k_ref, v_ref = refs[i:i + 3]
        i += 3
        kc_ref, vc_ref = refs[i:i + 2] if has_cache else (None, None)
        i += 2 * has_cache
        sink_ref = refs[i] if has_sink else None
        i += has_sink
        _attend(q_ref, k_ref, v_ref, kc_ref, vc_ref, sink_ref, o_ref, qi=pl.program_id(1),
                seq_len=seq_len, tq=tq, kv_chunk=kv_chunk, window=window)


def _attn_call(mixers, nb, seq_len):
    tq = min(ATTN_TILE, seq_len)
    nq = seq_len // tq
    kv_chunk = min(1024, seq_len)
    in_specs, args, static = [], [], []
    for mx in mixers:
        in_specs += [pl.BlockSpec((tq, 512), lambda b, i: (b * nq + i, 0)),
                     pl.BlockSpec((seq_len, 256), lambda b, i: (b, 0)),
                     pl.BlockSpec((seq_len, 256), lambda b, i: (b, 0))]
        args += [mx["q"], mx["k"], mx["v"]]
        if mx.get("cache") is not None:
            in_specs += [pl.BlockSpec((None, PAST_LEN, 256), lambda b, i: (b, 0, 0))] * 2
            args += list(mx["cache"])
        if mx.get("sink") is not None:
            in_specs.append(pl.BlockSpec((2, 2 * tq, 128), lambda b, i: (0, 0, 0)))
            args.append(mx["sink"])
        static.append((mx.get("cache") is not None, mx.get("sink") is not None, bool(mx.get("window"))))
    out = jax.ShapeDtypeStruct((nb * seq_len, 256), BF16)
    out_spec = pl.BlockSpec((tq, 256), lambda b, i: (b * nq + i, 0))
    return pl.pallas_call(
        functools.partial(_attn_kernel, seq_len=seq_len, tq=tq, kv_chunk=kv_chunk, mixers=tuple(static)),
        out_shape=[out] * len(mixers), grid=(nb, nq), in_specs=in_specs, out_specs=[out_spec] * len(mixers),
        compiler_params=_params(("arbitrary", "arbitrary")), name="attention",
    )(*args)


def _chunk_of(s, nc):
    return jnp.where(s < nc, nc - 1 - s, s - nc)


STEP_CHUNKS = 2


def _chunks_per_step(nc):
    return STEP_CHUNKS if nc % STEP_CHUNKS == 0 else 1


def _log_gamma(rd):
    return jnp.log1p(-jnp.exp(rd))


def _ret_kernel(*refs, nc, sub, has_init, emit_state):
    q_ref, k_ref, v_ref, g_ref, rdl_ref, rdh_ref, gn_ref, seg_ref = refs[:8]
    i = 8
    if has_init:
        s0_ref = refs[i]
        i += 1
    y_ref = refs[i]
    i += 1
    if emit_state:
        st_ref = refs[i]
        i += 1
    m_sc, dec_sc, qk_sc, sf_sc, sb_sc, sball_sc = refs[i:i + 6]
    c_len = CHUNK
    b = pl.program_id(0)
    s = pl.program_id(1)

    @pl.when(jnp.logical_and(b == 0, s == 0))
    def _():
        ri = lax.broadcasted_iota(jnp.int32, (c_len, c_len), 0)
        ci = lax.broadcasted_iota(jnp.int32, (c_len, c_len), 1)
        d = (ri - ci).astype(F32)
        for h in range(N_HEADS):
            lgf = _log_gamma(jnp.broadcast_to(rdh_ref[0, h], (c_len, c_len)))
            lgb = _log_gamma(jnp.broadcast_to(rdh_ref[1, h], (c_len, c_len)))
            m_sc[h] = jnp.where(d > 0, jnp.exp(d * lgf), jnp.where(d < 0, jnp.exp(-d * lgb), 2.0))
        for dr in range(2):
            rows = [jnp.exp(c_len * _log_gamma(jnp.broadcast_to(rdh_ref[dr, h], (HEAD_DIM, c_len))))
                    for h in range(N_HEADS)]
            dec_sc[dr] = jnp.concatenate(rows, axis=0)
        a = lax.broadcasted_iota(jnp.int32, (c_len, 256), 0).astype(F32)
        lgf = _log_gamma(rdl_ref[0])
        lgb = _log_gamma(rdl_ref[1])
        qk_sc[0] = jnp.exp((a + 1.0) * lgf)
        qk_sc[1] = jnp.exp((c_len - 1.0 - a) * lgf)
        qk_sc[2] = jnp.exp((c_len - a) * lgb)
        qk_sc[3] = jnp.exp(a * lgb)

    @pl.when(s == 0)
    def _():
        sf_sc[...] = jnp.zeros_like(sf_sc)
        sb_sc[...] = jnp.zeros_like(sb_sc)
        if has_init:
            for h in range(N_HEADS):
                blk = slice(HEAD_DIM * h, HEAD_DIM * (h + 1))
                sf_sc[blk, blk] = s0_ref[0, h]
                sb_sc[blk, blk] = s0_ref[1, h]

    ri = lax.broadcasted_iota(jnp.int32, (256, 256), 0) // HEAD_DIM
    ci = lax.broadcasted_iota(jnp.int32, (256, 256), 1) // HEAD_DIM
    diag = ri == ci
    ns = nc // sub

    @pl.when(s < ns)
    def _():
        state = sb_sc[...]
        for u in reversed(range(sub)):
            rows = slice(u * c_len, (u + 1) * c_len)
            sball_sc[(ns - 1 - s) * sub + u] = state
            kd = (k_ref[rows, :].astype(F32) * qk_sc[3]).astype(BF16)
            state = state * dec_sc[1] + jnp.where(diag, _dot_tn(kd, v_ref[rows, :]), 0.0)
        sb_sc[...] = state
        if emit_state:
            @pl.when(s == ns - 1)
            def _():
                for h in range(N_HEADS):
                    blk = slice(HEAD_DIM * h, HEAD_DIM * (h + 1))
                    st_ref[1, h] = sb_sc[blk, blk]

    @pl.when(s >= ns)
    def _():
        state = sf_sc[...]
        lane_head = lax.broadcasted_iota(jnp.int32, (c_len, 256), 1) // HEAD_DIM
        seg = seg_ref[...]
        for u in range(sub):
            rows = slice(u * c_len, (u + 1) * c_len)
            qb = q_ref[rows, :]
            kb = k_ref[rows, :]
            vb = v_ref[rows, :]
            q = qb.astype(F32)
            o = (_dot(qb, state.astype(BF16)) * qk_sc[0]
                 + _dot(qb, sball_sc[(s - ns) * sub + u].astype(BF16)) * qk_sc[2])
            for h in range(N_HEADS):
                mine = lane_head == h
                sc = _dot_nt(jnp.where(mine, q, 0.0).astype(BF16), kb)
                pv = _dot((sc * m_sc[h]).astype(BF16), vb)
                o = o + jnp.where(mine, pv, 0.0)
            mu = _seg_sum(o, seg) * (1.0 / HEAD_DIM)
            dlt = o - mu
            var = _seg_sum(dlt * dlt, seg) * (1.0 / HEAD_DIM)
            on = dlt * lax.rsqrt(var + 1e-5) * gn_ref[...]
            y_ref[rows, :] = (on * _silu(g_ref[rows, :])).astype(y_ref.dtype)
            kd = (kb.astype(F32) * qk_sc[1]).astype(BF16)
            state = state * dec_sc[0] + jnp.where(diag, _dot_tn(kd, vb), 0.0)
        sf_sc[...] = state
        if emit_state:
            @pl.when(s == 2 * ns - 1)
            def _():
                for h in range(N_HEADS):
                    blk = slice(HEAD_DIM * h, HEAD_DIM * (h + 1))
                    st_ref[0, h] = sf_sc[blk, blk]


def _ret_specs(rqkv, pf, nb, seq_len, rd_lane, rd_head, gn_g, seg, layer, state0=None, emit_state=False):
    c_len = CHUNK
    nc = seq_len // c_len
    sub = _chunks_per_step(nc)
    ns = nc // sub
    rows = sub * c_len

    def col(start):
        return pl.BlockSpec((rows, 256), lambda b, s: (b * ns + _chunk_of(s, ns), start // 256))

    in_specs = [col(0), col(256), col(512), col(F_RG),
                pl.BlockSpec((2, 1, 256), lambda b, s: (0, 0, 0)),
                pl.BlockSpec((2, N_HEADS, 1, c_len), lambda b, s: (0, 0, 0, 0)),
                pl.BlockSpec((1, 256), lambda b, s: (0, 0)),
                pl.BlockSpec((256, 256), lambda b, s: (0, 0))]
    args = [rqkv, rqkv, rqkv, pf, rd_lane, rd_head, gn_g, seg]
    if state0 is not None:
        in_specs.append(pl.BlockSpec((None, None, 2, N_HEADS, HEAD_DIM, HEAD_DIM),
                                     lambda b, s: (b, layer, 0, 0, 0, 0)))
        args.append(state0)
    outs = [jax.ShapeDtypeStruct((nb * seq_len, 256), BF16)]
    out_specs = [pl.BlockSpec((rows, 256), lambda b, s: (b * ns + jnp.maximum(s - ns, 0), 0))]
    if emit_state:
        outs.append(jax.ShapeDtypeStruct((nb, DEPTH, 2, N_HEADS, HEAD_DIM, HEAD_DIM), F32))
        out_specs.append(pl.BlockSpec((None, None, 2, N_HEADS, HEAD_DIM, HEAD_DIM),
                                      lambda b, s: (b, layer, 0, 0, 0, 0)))
    scratch = [pltpu.VMEM((N_HEADS, c_len, c_len), F32), pltpu.VMEM((2, 256, 256), F32),
               pltpu.VMEM((4, c_len, 256), F32), pltpu.VMEM((256, 256), F32),
               pltpu.VMEM((256, 256), F32), pltpu.VMEM((nc, 256, 256), F32)]
    return in_specs, args, outs, out_specs, scratch


def _cumsum_rows(x, reverse):
    n = x.shape[0]
    row = lax.broadcasted_iota(jnp.int32, x.shape, 0)
    sh = 1
    while sh < n:
        if reverse:
            x = x + jnp.where(row < n - sh, pltpu.roll(x, n - sh, 0), 0.0)
        else:
            x = x + jnp.where(row >= sh, pltpu.roll(x, sh, 0), 0.0)
        sh *= 2
    return x


def _ssd_kernel(*refs, nc, sub, has_init, emit_state):
    xbc_ref, prev_ref, next_ref, z_ref, dt_ref, cw_ref, cb_ref, dtb_ref, al_ref, dl_ref, ng_ref = refs[:11]
    i = 11
    if has_init:
        h0_ref = refs[i]
        i += 1
    y_ref = refs[i]
    i += 1
    if emit_state:
        st_ref = refs[i]
        i += 1
    hf_sc, hb_sc, hball_sc, xs_sc, bm_sc, cm_sc, dt_sc, cum_sc, rev_sc = refs[i:i + 9]
    c_len = CHUNK
    ns = nc // sub
    s = pl.program_id(1)

    @pl.when(s == 0)
    def _():
        if has_init:
            hf_sc[...] = h0_ref[0]
            hb_sc[...] = h0_ref[1]
        else:
            hf_sc[...] = jnp.zeros_like(hf_sc)
            hb_sc[...] = jnp.zeros_like(hb_sc)

    low = lax.broadcasted_iota(jnp.int32, (c_len, 128), 1) < HEAD_DIM
    low_rows = lax.broadcasted_iota(jnp.int32, (128, 128), 0) < HEAD_DIM

    def col(a, lane):
        return jnp.broadcast_to(a[:, lane:lane + 1], (c_len, c_len))

    def pair(a0, a1):
        return jnp.where(low, a0[:, :128], a1[:, :128])

    def state_update(h, g, xs, bm, dt, cols, tot_row, dt_lanes):
        w = [jnp.exp2(tot_row[r][:, :128] - cols[r][:, :128]) * col(dt, dt_lanes[r])[:, :128] for r in range(2)]
        xw = (xs[:, 128 * g:128 * (g + 1)] * jnp.where(low, w[0], w[1])).astype(BF16)
        dec = jnp.where(low_rows, jnp.broadcast_to(jnp.exp2(tot_row[0][:, :128]), (128, 128)),
                        jnp.broadcast_to(jnp.exp2(tot_row[1][:, :128]), (128, 128)))
        return h * dec + _dot_tn(xw, bm[:, 128 * g:128 * (g + 1)])

    @pl.when(s < ns)
    def _():
        cg = ns - 1 - s
        n_rows = sub * c_len
        xbc = xbc_ref[...]
        row = lax.broadcasted_iota(jnp.int32, xbc.shape, 0)
        before = jnp.where(cg > 0, prev_ref[7:8, :], 0.0)
        after = jnp.where(cg < ns - 1, next_ref[0:1, :], 0.0)
        x_prev = jnp.where(row == 0, before, pltpu.roll(xbc, 1, 0))
        x_next = jnp.where(row == n_rows - 1, after, pltpu.roll(xbc, n_rows - 1, 0))
        conv = _silu(cw_ref[0:1, :] * x_prev + cw_ref[1:2, :] * xbc + cw_ref[2:3, :] * x_next + cb_ref[...])
        pre = dt_ref[...] + dtb_ref[...]
        dt_all = jnp.maximum(pre, 0.0) + jnp.log1p(jnp.exp(-jnp.abs(pre)))
        dta_all = dt_all * (-jnp.exp(al_ref[...]) * LOG2E)
        state = [hb_sc[g] for g in range(2)]
        for u in reversed(range(sub)):
            rows = slice(u * c_len, (u + 1) * c_len)
            c = cg * sub + u
            xs = conv[rows, 0:256]
            bm = conv[rows, 256:512].astype(BF16)
            dt = dt_all[rows, :]
            rev = _cumsum_rows(dta_all[rows, :], reverse=True)
            xs_sc[c] = xs
            bm_sc[c] = bm
            cm_sc[c] = conv[rows, 512:768].astype(BF16)
            dt_sc[c] = dt
            cum_sc[c] = _cumsum_rows(dta_all[rows, :], reverse=False)
            rev_sc[c] = rev
            for g in range(2):
                hball_sc[c, g] = state[g]
                cols = [col(rev, 4 + 2 * g + r) for r in range(2)]
                tot = [cl[0:1, :] for cl in cols]
                state[g] = state_update(state[g], g, xs, bm, dt, cols, tot, [4 + 2 * g, 5 + 2 * g])
        for g in range(2):
            hb_sc[g] = state[g]
        if emit_state:
            @pl.when(s == ns - 1)
            def _():
                st_ref[1] = hb_sc[...]

    @pl.when(s >= ns)
    def _():
        ri = lax.broadcasted_iota(jnp.int32, (c_len, c_len), 0)
        ci = lax.broadcasted_iota(jnp.int32, (c_len, c_len), 1)
        below = ri > ci
        on_diag = ri == ci
        lane_head = lax.broadcasted_iota(jnp.int32, (c_len, 256), 1) // HEAD_DIM
        state = [hf_sc[g] for g in range(2)]
        for u in range(sub):
            rows = slice(u * c_len, (u + 1) * c_len)
            c = (s - ns) * sub + u
            xs = xs_sc[c]
            bm = bm_sc[c]
            cm = cm_sc[c]
            dt = dt_sc[c]
            cum = cum_sc[c]
            rev = rev_sc[c]
            log_dt = jnp.log2(dt)
            off_f = (log_dt - cum).T
            off_b = (log_dt - rev).T
            dt_t = dt.T
            xs_b = xs.astype(BF16)
            y = jnp.zeros((c_len, 256), F32)
            inter = []
            for g in range(2):
                gl = slice(128 * g, 128 * (g + 1))
                gmat = _dot_nt(cm[:, gl], bm[:, gl])
                cols_f, cols_b = [], []
                for r in range(2):
                    h = 2 * g + r
                    cf = col(cum, h)
                    cb = col(rev, 4 + h)
                    cols_f.append(cf)
                    cols_b.append(cb)
                    w = jnp.exp2(jnp.where(below, cf + off_f[h:h + 1, :], cb + off_b[4 + h:5 + h, :]))
                    w = w + jnp.where(on_diag, dt_t[h:h + 1, :], 0.0)
                    yh = _dot((gmat * w).astype(BF16), xs_b)
                    y = y + jnp.where(lane_head == h, yh, 0.0)
                yf = _dot_nt(cm[:, gl], state[g].astype(BF16))
                yb = _dot_nt(cm[:, gl], hball_sc[c, g].astype(BF16))
                inter.append(yf * pair(jnp.exp2(cols_f[0]), jnp.exp2(cols_f[1]))
                             + yb * pair(jnp.exp2(cols_b[0]), jnp.exp2(cols_b[1])))
                tot = [cl[c_len - 1:c_len, :] for cl in cols_f]
                state[g] = state_update(state[g], g, xs, bm, dt, cols_f, tot, [2 * g, 2 * g + 1])
            y = y + jnp.concatenate(inter, axis=1) + xs * dl_ref[...]
            y = y * _silu(z_ref[rows, :])
            ms = jnp.mean(y * y, axis=-1, keepdims=True)
            y_ref[rows, :] = (y * lax.rsqrt(ms + 1e-6) * ng_ref[...]).astype(y_ref.dtype)
        for g in range(2):
            hf_sc[g] = state[g]
        if emit_state:
            @pl.when(s == 2 * ns - 1)
            def _():
                st_ref[0] = hf_sc[...]


def _ssd_specs(p, nb, seq_len, conv_w, conv_b, dt_bias, a_log, d_lane, norm_g, layer,
              state0=None, emit_state=False):
    c_len = CHUNK
    nc = seq_len // c_len
    sub = _chunks_per_step(nc)
    ns = nc // sub
    rows = sub * c_len
    m = nb * seq_len
    r8 = rows // 8
    chunk = lambda b, s: b * ns + jnp.maximum(ns - 1 - s, 0)
    in_specs = [pl.BlockSpec((rows, 768), lambda b, s: (chunk(b, s), F_XBC // 768)),
                pl.BlockSpec((8, 768), lambda b, s: (jnp.maximum(chunk(b, s) * r8 - 1, 0), F_XBC // 768)),
                pl.BlockSpec((8, 768), lambda b, s: (jnp.minimum((chunk(b, s) + 1) * r8, m // 8 - 1),
                                                     F_XBC // 768)),
                pl.BlockSpec((rows, 256), lambda b, s: (b * ns + jnp.maximum(s - ns, 0), F_SZ // 256)),
                pl.BlockSpec((rows, 128), lambda b, s: (chunk(b, s), F_DT // 128)),
                pl.BlockSpec((None, 3, 768), lambda b, s: (layer, 0, 0)),
                pl.BlockSpec((1, 768), lambda b, s: (0, 0)),
                pl.BlockSpec((1, 128), lambda b, s: (0, 0)),
                pl.BlockSpec((1, 128), lambda b, s: (0, 0)),
                pl.BlockSpec((1, 256), lambda b, s: (0, 0)),
                pl.BlockSpec((1, 256), lambda b, s: (0, 0))]
    args = [p, p, p, p, p, conv_w, conv_b, dt_bias, a_log, d_lane, norm_g]
    if state0 is not None:
        in_specs.append(pl.BlockSpec((None, None, 2, 2, 128, SSD_STATE),
                                     lambda b, s: (b, layer, 0, 0, 0, 0)))
        args.append(state0)
    outs = [jax.ShapeDtypeStruct((m, 256), BF16)]
    out_specs = [pl.BlockSpec((rows, 256), lambda b, s: (b * ns + jnp.maximum(s - ns, 0), 0))]
    if emit_state:
        outs.append(jax.ShapeDtypeStruct((nb, DEPTH, 2, 2, 128, SSD_STATE), F32))
        out_specs.append(pl.BlockSpec((None, None, 2, 2, 128, SSD_STATE),
                                      lambda b, s: (b, layer, 0, 0, 0, 0)))
    scratch = [pltpu.VMEM((2, 128, SSD_STATE), F32), pltpu.VMEM((2, 128, SSD_STATE), F32),
               pltpu.VMEM((nc, 2, 128, SSD_STATE), F32),
               pltpu.VMEM((nc, c_len, 256), F32), pltpu.VMEM((nc, c_len, 256), BF16),
               pltpu.VMEM((nc, c_len, 256), BF16), pltpu.VMEM((nc, c_len, 128), F32),
               pltpu.VMEM((nc, c_len, 128), F32), pltpu.VMEM((nc, c_len, 128), F32)]
    return in_specs, args, outs, out_specs, scratch


def _scan_kernel(*refs, counts, **static):
    (ri, si), (ro, so), (rs, ss), n_alias = counts
    ins, rest = refs[:ri + si], refs[ri + si + n_alias:]
    outs, scr = rest[:ro + so], rest[ro + so:]
    _ret_kernel(*ins[:ri], *outs[:ro], *scr[:rs], **static)
    _ssd_kernel(*ins[ri:], *outs[ro:], *scr[rs:], **static)


def _scan_call(ret_parts, ssd_parts, nb, seq_len, has_init, emit_state, states_prev=None):
    nc = seq_len // CHUNK
    sub = _chunks_per_step(nc)
    in_specs = ret_parts[0] + ssd_parts[0]
    args = ret_parts[1] + ssd_parts[1]
    aliases = {}
    if states_prev is not None:
        aliases = {len(args): 1, len(args) + 1: len(ret_parts[2]) + 1}
        in_specs = in_specs + [pl.BlockSpec(memory_space=pl.ANY)] * 2
        args = args + list(states_prev)
    counts = tuple((len(a), len(b)) for a, b in zip((ret_parts[0], ret_parts[2], ret_parts[4]),
                                                    (ssd_parts[0], ssd_parts[2], ssd_parts[4])))
    return pl.pallas_call(
        functools.partial(_scan_kernel, counts=counts + (len(aliases),), nc=nc, sub=sub, has_init=has_init,
                          emit_state=emit_state),
        out_shape=ret_parts[2] + ssd_parts[2], grid=(nb, 2 * nc // sub),
        in_specs=in_specs, out_specs=ret_parts[3] + ssd_parts[3],
        scratch_shapes=ret_parts[4] + ssd_parts[4], input_output_aliases=aliases,
        compiler_params=_params(("arbitrary", "arbitrary")), name="scans",
    )(*args)


def _layer_norm(x, g, b):
    mu = jnp.mean(x, axis=-1, keepdims=True)
    d = x - mu
    var = jnp.mean(d * d, axis=-1, keepdims=True)
    return d * lax.rsqrt(var + 1e-5) * g + b


FFN_CHUNK = 256
FFN_STREAMS = 1
FFN_ROWS = 512


def _outffn_kernel(x_ref, yr_ref, yg_ref, ys_ref, yw_ref, g1_ref, sh2_ref, sc2_ref, g2_ref,
                   wo_ref, wi_ref, wf_ref, l1g_ref, l1b_ref, l2g_ref, l2b_ref, o_ref):
    n_chunks = D_FF // FFN_CHUNK
    half = n_chunks // 2

    def head(r):
        rows = slice(FFN_ROWS * r, FFN_ROWS * (r + 1))
        mix = (_dot(yr_ref[rows, :], wo_ref[0:256, :]) + _dot(yg_ref[rows, :], wo_ref[256:512, :])
               + _dot(ys_ref[rows, :], wo_ref[512:768, :]) + _dot(yw_ref[rows, :], wo_ref[768:1024, :]))
        x1 = _layer_norm(ALPHA * x_ref[rows, :] + g1_ref[...] * mix, l1g_ref[...], l1b_ref[...])
        return x1, (x1 * (1.0 + sc2_ref[...]) + sh2_ref[...]).astype(BF16)

    def ffn(h2, acc, chunks):
        for j in chunks:
            cols = slice(FFN_CHUNK * j, FFN_CHUNK * (j + 1))
            gate = _dot(h2, wi_ref[:, cols])
            up = _dot(h2, wi_ref[:, D_FF + FFN_CHUNK * j:D_FF + FFN_CHUNK * (j + 1)])
            d = _dot((_silu(gate) * up).astype(BF16), wf_ref[cols, :])
            acc = d if acc is None else acc + d
        return acc

    def tail(r, x1, acc):
        rows = slice(FFN_ROWS * r, FFN_ROWS * (r + 1))
        o_ref[rows, :] = _layer_norm(ALPHA * x1 + g2_ref[...] * acc, l2g_ref[...], l2b_ref[...])

    state = [None] * FFN_STREAMS
    x1, h2 = head(0)
    state[0] = (x1, h2, ffn(h2, None, range(0, half)))
    for r in range(FFN_STREAMS):
        x1, h2, acc = state[r]
        if r + 1 < FFN_STREAMS:
            nx1, nh2 = head(r + 1)
        acc = ffn(h2, acc, range(half, n_chunks))
        if r + 1 < FFN_STREAMS:
            state[r + 1] = (nx1, nh2, ffn(nh2, None, range(0, half)))
        tail(r, x1, acc)


def _outffn_call(x, ys, mod5, w_out_b, w_ffn_in_b, w_ffn_out_b, ln, layer, row0, tiles_per_row):
    m = x.shape[0]
    tm = FFN_ROWS * FFN_STREAMS
    tiles_per_row = tiles_per_row * TOKEN_TILE // tm
    row = lambda w: pl.BlockSpec((tm, w), lambda i: (i, 0))
    once = pl.Buffered(1)
    vec = pl.BlockSpec((None, 1, D_MODEL), lambda i: (layer, 0, 0))
    in_specs = [row(D_MODEL), row(256), row(256), row(256), row(256),
                _mod_spec(layer, row0, tiles_per_row, 2), _mod_spec(layer, row0, tiles_per_row, 3),
                _mod_spec(layer, row0, tiles_per_row, 4), _mod_spec(layer, row0, tiles_per_row, 5),
                pl.BlockSpec((None, D_MODEL, D_MODEL), lambda i: (layer, 0, 0), pipeline_mode=once),
                pl.BlockSpec((None, D_MODEL, 2 * D_FF), lambda i: (layer, 0, 0), pipeline_mode=once),
                pl.BlockSpec((None, D_FF, D_MODEL), lambda i: (layer, 0, 0), pipeline_mode=once),
                vec, vec, vec, vec]
    return pl.pallas_call(
        _outffn_kernel,
        out_shape=jax.ShapeDtypeStruct((m, D_MODEL), F32),
        grid=(m // tm,), in_specs=in_specs, out_specs=row(D_MODEL),
        compiler_params=_params(("arbitrary",)), name="out_ffn",
    )(x, *ys, mod5, mod5, mod5, mod5, w_out_b, w_ffn_in_b, w_ffn_out_b, *ln)


def _rope_tables(seq_len):
    t = jnp.arange(seq_len)
    rowp = (t // GRID_W).astype(F32)
    colp = (t % GRID_W).astype(F32)
    quarter = HEAD_DIM // 4
    inv_freq = ROPE_THETA ** (-jnp.arange(quarter, dtype=F32) / quarter)
    ang_r = rowp[:, None] * inv_freq[None, :]
    ang_c = colp[:, None] * inv_freq[None, :]
    cos = jnp.concatenate([jnp.cos(ang_r)] * 2 + [jnp.cos(ang_c)] * 2, axis=-1)
    sin = jnp.concatenate([-jnp.sin(ang_r), jnp.sin(ang_r), -jnp.sin(ang_c), jnp.sin(ang_c)], axis=-1)
    return jnp.tile(cos, (1, N_HEADS)), jnp.tile(sin, (1, N_HEADS))


def _expand_cache(c, fill):
    pad = jnp.full(c.shape[:2] + (HEAD_DIM,), fill, c.dtype)
    return jnp.concatenate([c[:, :, 0], pad, c[:, :, 1], pad], axis=-1).astype(BF16)


def _split_w_in(w):
    pad = jnp.zeros(w.shape[:-1] + (TAIL_W - T_DT - 8,), w.dtype)
    tail = jnp.concatenate([w[..., MAIN_W + 8:], w[..., MAIN_W:MAIN_W + 8], pad], axis=-1)
    return w[..., :MAIN_W].astype(BF16), tail.astype(BF16)


def kernel(x_prompt, x_sample, cache_gqa_k, cache_gqa_v, cache_swa_k, cache_swa_v, state_ret, state_ssd,
           c, c_ctx, ada_w, ada_b, w_in, ret_decay, ret_gn_g, gqa_q_norm, gqa_k_norm,
           ssd_conv_w, ssd_conv_b, ssd_dt_bias, ssd_a_log, ssd_d, ssd_norm_g, swa_sink, w_out,
           ln1_g, ln1_b, w_ffn_in, w_ffn_out, ln2_g, ln2_b):
    nb_c, len_c, _ = x_prompt.shape
    nb_l, len_l, _ = x_sample.shape
    depth = ada_w.shape[0]

    cond8 = jnp.zeros((8, D_MODEL), F32).at[0].set(c_ctx).at[1:1 + nb_l].set(c)
    mod5 = _ada_call(cond8, ada_w, ada_b)

    w_main, w_tail = _split_w_in(w_in)
    w_out_b = w_out.astype(BF16)
    w_ffn_in_b = w_ffn_in.astype(BF16)
    w_ffn_out_b = w_ffn_out.astype(BF16)
    lane_head = jnp.arange(256) // HEAD_DIM
    seg = (lane_head[:, None] == lane_head[None, :]).astype(BF16)
    rope_tabs = _rope_tables(len_l)
    caches_g = (_expand_cache_layers(cache_gqa_k, 0.0), _expand_cache_layers(cache_gqa_v, 1.0))
    caches_w = (_expand_cache_layers(cache_swa_k, 0.0), _expand_cache_layers(cache_swa_v, 1.0))
    state_ssd_g = state_ssd.reshape(nb_l, depth, 2, 2, 128, SSD_STATE)

    def layer_consts(l):
        rd = ret_decay[l]
        return dict(
            rd_lane=jnp.repeat(rd, HEAD_DIM, axis=-1).reshape(2, 1, 256),
            rd_head=jnp.broadcast_to(rd[:, :, None, None], (2, N_HEADS, 1, CHUNK)),
            gn_g=ret_gn_g[l].reshape(1, 256),
            qn=jnp.tile(gqa_q_norm[l], N_HEADS).reshape(1, 256),
            kn=jnp.tile(gqa_k_norm[l], N_HEADS).reshape(1, 256),
            conv_b=ssd_conv_b[l].reshape(1, 768),
            dt_bias=jnp.zeros((1, 128), F32).at[0, :8].set(ssd_dt_bias[l].reshape(8)),
            a_log=jnp.zeros((1, 128), F32).at[0, :8].set(ssd_a_log[l].reshape(8)),
            d_lane=jnp.repeat(ssd_d[l], HEAD_DIM).reshape(1, 256),
            norm_g=ssd_norm_g[l].reshape(1, 256),
            sink=lambda tq: jnp.broadcast_to(swa_sink[l].reshape(2, 2, 1, 1), (2, 2, tq, 128)
                                             ).reshape(2, 2 * tq, 128),
        )

    ln = tuple(a.reshape(depth, 1, D_MODEL) for a in (ln1_g, ln1_b, ln2_g, ln2_b))

    def run_group(x, nb, seq_len, row0, latent):
        m = nb * seq_len
        tpr = (seq_len // TOKEN_TILE) if latent else (m // TOKEN_TILE)
        x = x.reshape(m, D_MODEL)
        new_kv = states = None
        if not latent:
            new_kv = tuple(jnp.zeros((nb, depth, seq_len, 128), F32) for _ in range(4))
            states = (jnp.zeros((nb, depth, 2, N_HEADS, HEAD_DIM, HEAD_DIM), F32),
                      jnp.zeros((nb, depth, 2, 2, 128, SSD_STATE), F32))
        for l in range(depth):
            k = layer_consts(l)
            proj = _inproj_call(x, mod5, w_main, w_tail, l, row0, tpr, k["qn"], k["kn"], seg,
                                rope_tabs if latent else None, seq_len, new_kv_prev=new_kv)
            pf, rqkv, qg, kg, vg, qw, kw, vw = proj[:8]
            ret_parts = _ret_specs(rqkv, pf, nb, seq_len, k["rd_lane"], k["rd_head"], k["gn_g"], seg, l,
                                   state0=state_ret if latent else None, emit_state=not latent)
            ssd_parts = _ssd_specs(pf, nb, seq_len, ssd_conv_w, k["conv_b"], k["dt_bias"], k["a_log"],
                                   k["d_lane"], k["norm_g"], l,
                                   state0=state_ssd_g if latent else None, emit_state=not latent)
            scans = _scan_call(ret_parts, ssd_parts, nb, seq_len, has_init=latent, emit_state=not latent,
                               states_prev=states)
            if latent:
                y_ret, y_ssd = scans
                y_gqa, y_swa = _attn_call(
                    [dict(q=qg, k=kg, v=vg, cache=(caches_g[0][l], caches_g[1][l])),
                     dict(q=qw, k=kw, v=vw, cache=(caches_w[0][l], caches_w[1][l]),
                          sink=k["sink"](min(ATTN_TILE, seq_len)), window=True)], nb, seq_len)
            else:
                y_ret, st_ret, y_ssd, st_ssd = scans
                states = (st_ret, st_ssd)
                new_kv = proj[8:12]
                y_gqa, = _attn_call([dict(q=qg, k=kg, v=vg)], nb, seq_len)
                y_swa, = _attn_call([dict(q=qw, k=kw, v=vw, sink=k["sink"](min(ATTN_TILE, seq_len)))],
                                    nb, seq_len)
            x = _outffn_call(x, (y_ret, y_gqa, y_ssd, y_swa), mod5, w_out_b, w_ffn_in_b, w_ffn_out_b,
                             ln, l, row0, tpr)
        extras = None
        if not latent:
            extras = tuple(a.reshape(nb, depth, seq_len, 2, HEAD_DIM) for a in new_kv) + (
                states[0], states[1].reshape(nb, depth, 2, N_HEADS, HEAD_DIM, SSD_STATE))
        return x.reshape(nb, seq_len, D_MODEL), extras

    y_prompt, extras = run_group(x_prompt, nb_c, len_c, 0, False)
    y_sample, _ = run_group(x_sample, nb_l, len_l, 1, True)
    return (y_prompt, y_sample) + extras


def _expand_cache_layers(cache, fill):
    return [_expand_cache(cache[:, l], fill) for l in range(cache.shape[1])]
```

```python
import functools
import math

import jax
import jax.numpy as jnp
from jax import lax
from jax.experimental import pallas as pl
from jax.experimental.pallas import tpu as pltpu

F32 = jnp.float32
BF16 = jnp.bfloat16

D_MODEL = 1024
DEPTH = 4
HEAD_DIM = 64
N_HEADS = 4
GRID_W = 64
ROPE_THETA = 10000.0
SSD_STATE = 128
D_FF = 2816
WINDOW = 128
PAST_LEN = 256
ALPHA = (2.0 * DEPTH) ** 0.25
CHUNK = 256
TOKEN_TILE = 256
INPROJ_TILE = 512
ATTN_TILE = 512
VMEM_LIMIT = 56 * 1024 * 1024
LOG2E = math.log2(math.e)
QK_SCALE = HEAD_DIM ** -0.5 * LOG2E

M_RET, M_RG, M_GQA, M_SZ, M_XBC, MAIN_W = 0, 768, 1024, 1536, 1792, 2560
T_SWA, T_DT, TAIL_W = 0, 512, 640
F_XBC, F_RG, F_SZ, F_DT = 0, 768, 1024, 1280
F_WIDTH = 1408


def _silu(x):
    return x / (1.0 + jnp.exp(-x))


def _dot(a, b):
    return jnp.dot(a, b, preferred_element_type=F32)


def _dot_nt(a, b):
    return lax.dot_general(a, b, (((1,), (1,)), ((), ())), preferred_element_type=F32)


def _dot_tn(a, b):
    return lax.dot_general(a, b, (((0,), (0,)), ((), ())), preferred_element_type=F32)


def _seg_sum(x, seg):
    hi = x.astype(BF16)
    lo = (x - hi.astype(F32)).astype(BF16)
    return _dot(hi, seg) + _dot(lo, seg)


def _params(sem):
    return pltpu.CompilerParams(dimension_semantics=sem, vmem_limit_bytes=VMEM_LIMIT)


def _ada_kernel(c_ref, w_ref, b_ref, o_ref):
    s = _silu(c_ref[...])
    o_ref[...] = _dot(s.astype(BF16), w_ref[...].astype(BF16)) + b_ref[...]


def _ada_call(cond8, ada_w, ada_b):
    tn = D_MODEL
    nt = 6
    out = pl.pallas_call(
        _ada_kernel,
        out_shape=jax.ShapeDtypeStruct((DEPTH, nt, 8, tn), F32),
        grid=(DEPTH, nt),
        in_specs=[pl.BlockSpec((8, D_MODEL), lambda l, j: (0, 0)),
                  pl.BlockSpec((None, D_MODEL, tn), lambda l, j: (l, 0, j)),
                  pl.BlockSpec((None, None, 1, tn), lambda l, j: (l, j, 0, 0))],
        out_specs=pl.BlockSpec((None, None, 8, tn), lambda l, j: (l, j, 0, 0)),
        compiler_params=_params(("arbitrary", "arbitrary")),
        name="ada_mod",
    )(cond8, ada_w, ada_b.reshape(DEPTH, nt, 1, tn))
    return jnp.transpose(out, (0, 2, 1, 3)).reshape(DEPTH, 8, nt, 1, tn)


def _mod_spec(layer, row0, tiles_per_row, k):
    return pl.BlockSpec((None, None, None, 1, D_MODEL),
                        lambda i: (layer, row0 + i // tiles_per_row, k, 0, 0))


def _rope(x, cos, sin_signed):
    w = x.shape[-1]
    lane = lax.broadcasted_iota(jnp.int32, x.shape, 1)
    partner = jnp.where((lane % 32) < 16, pltpu.roll(x, w - 16, 1), pltpu.roll(x, 16, 1))
    return x * cos + partner * sin_signed


def _expand_q(q):
    lane = lax.broadcasted_iota(jnp.int32, (q.shape[0], 128), 1)
    low = lane < HEAD_DIM
    blocks = []
    for j in range(2):
        pair = q[:, 128 * j:128 * (j + 1)]
        blocks.append(jnp.where(low, pair, 0.0))
        blocks.append(jnp.where(low, pltpu.roll(pair, HEAD_DIM, 1), 0.0))
    return jnp.concatenate(blocks, axis=1).astype(BF16)


def _expand_kv(k, fill):
    lane = lax.broadcasted_iota(jnp.int32, k.shape, 1)
    low = lane < HEAD_DIM
    return jnp.concatenate([jnp.where(low, k, fill),
                            jnp.where(low, pltpu.roll(k, HEAD_DIM, 1), fill)], axis=1).astype(BF16)


def _inproj_kernel(*refs, rope, n_alias):
    x_ref, sh_ref, sc_ref, w_ref, wt_ref, qn, kn, seg = refs[:8]
    i = 8
    if rope:
        cos_ref, sin_ref = refs[8:10]
        i = 10
    i += n_alias
    f_o, r_o, qg_o, kg_o, vg_o, qw_o, kw_o, vw_o = refs[i:i + 8]
    h = (x_ref[...] * (1.0 + sc_ref[...]) + sh_ref[...]).astype(BF16)

    f_o[:, F_XBC:F_XBC + 768] = _dot(h, w_ref[:, M_XBC:M_XBC + 768])
    f_o[:, F_RG:F_RG + 256] = _dot(h, w_ref[:, M_RG:M_RG + 256])
    f_o[:, F_SZ:F_SZ + 256] = _dot(h, w_ref[:, M_SZ:M_SZ + 256])
    f_o[:, F_DT:F_DT + 128] = _dot(h, wt_ref[:, T_DT:T_DT + 128])
    ret = _dot(h, w_ref[:, M_RET:M_RET + 768])
    r_o[...] = jnp.concatenate([ret[:, 0:256], ret[:, 256:512] * HEAD_DIM ** -0.5, ret[:, 512:768]],
                               axis=1).astype(BF16)

    seg_m = seg[...]

    def rms(v, g, s):
        ms = _seg_sum(v * v, s) * (1.0 / HEAD_DIM)
        return v * lax.rsqrt(ms + 1e-6) * g

    gqa = _dot(h, w_ref[:, M_GQA:M_GQA + 512])
    swa = _dot(h, wt_ref[:, T_SWA:T_SWA + 512])
    q_g = rms(gqa[:, 0:256], qn[...], seg_m)
    k_g = rms(gqa[:, 256:384], kn[...][:, :128], seg_m[:128, :128])
    q_w = swa[:, 0:256]
    k_w = swa[:, 256:384]
    if rope:
        cos = cos_ref[...]
        sin = sin_ref[...]
        q_g = _rope(q_g, cos, sin)
        q_w = _rope(q_w, cos, sin)
        k_g = _rope(k_g, cos[:, :128], sin[:, :128])
        k_w = _rope(k_w, cos[:, :128], sin[:, :128])
    else:
        for o_ref, piece in zip(refs[i + 8:i + 12], (k_g, gqa[:, 384:512], swa[:, 256:384], swa[:, 384:512])):
            o_ref[...] = piece.reshape(o_ref.shape)
    qg_o[...] = _expand_q(q_g * QK_SCALE)
    qw_o[...] = _expand_q(q_w * QK_SCALE)
    kg_o[...] = _expand_kv(k_g, 0.0)
    kw_o[...] = _expand_kv(k_w, 0.0)
    vg_o[...] = _expand_kv(gqa[:, 384:512], 1.0)
    vw_o[...] = _expand_kv(swa[:, 384:512], 1.0)


def _inproj_call(x, mod5, w_main, w_tail, layer, row0, tiles_per_row, qn256, kn256, seg, rope_tabs, seq_len,
                 new_kv_prev=None):
    m = x.shape[0]
    tm = INPROJ_TILE
    tiles_per_row = tiles_per_row * TOKEN_TILE // tm
    rope = rope_tabs is not None
    const = lambda shape: pl.BlockSpec(shape, lambda i: (0, 0))
    in_specs = [pl.BlockSpec((tm, D_MODEL), lambda i: (i, 0)),
                _mod_spec(layer, row0, tiles_per_row, 0),
                _mod_spec(layer, row0, tiles_per_row, 1),
                pl.BlockSpec((None, D_MODEL, MAIN_W), lambda i: (layer, 0, 0)),
                pl.BlockSpec((None, D_MODEL, TAIL_W), lambda i: (layer, 0, 0)),
                const((1, 256)), const((1, 256)), const((256, 256))]
    args = [x, mod5, mod5, w_main, w_tail, qn256, kn256, seg]
    if rope:
        tpb = seq_len // tm
        tab = pl.BlockSpec((tm, 256), lambda i: (i % tpb, 0))
        in_specs += [tab, tab]
        args += list(rope_tabs)
    widths = [(F_WIDTH, F32), (768, BF16), (512, BF16), (256, BF16), (256, BF16),
              (512, BF16), (256, BF16), (256, BF16)]
    out_shape = [jax.ShapeDtypeStruct((m, w), dt) for w, dt in widths]
    out_specs = [pl.BlockSpec((tm, w), lambda i: (i, 0)) for w, _ in widths]
    aliases = {}
    if not rope:
        bpt = tm // seq_len
        out_shape += [jax.ShapeDtypeStruct((m // seq_len, DEPTH, seq_len, 128), F32)] * 4
        out_specs += [pl.BlockSpec((bpt, None, seq_len, 128), lambda i: (i, layer, 0, 0))] * 4
        if new_kv_prev is not None:
            aliases = {len(args) + t: len(widths) + t for t in range(4)}
            in_specs += [pl.BlockSpec(memory_space=pl.ANY)] * 4
            args += list(new_kv_prev)
    return pl.pallas_call(
        functools.partial(_inproj_kernel, rope=rope, n_alias=len(aliases)),
        out_shape=out_shape, grid=(m // tm,), in_specs=in_specs, out_specs=out_specs,
        input_output_aliases=aliases,
        compiler_params=_params(("arbitrary",)), name="in_proj",
    )(*args)


def _online(m, acc, s_blocks, v_blocks):
    m_new = m
    for s in s_blocks:
        m_new = jnp.maximum(m_new, jnp.max(s, axis=-1, keepdims=True))
    acc = jnp.exp2(m - m_new) * acc
    for s, v in zip(s_blocks, v_blocks):
        acc = acc + _dot(jnp.exp2(s - m_new).astype(BF16), v)
    return m_new, acc


def _attend(q_ref, k_ref, v_ref, kc_ref, vc_ref, sink_ref, o_ref, *, qi, seq_len, tq, kv_chunk, window):
    rows = 2 * tq
    q2 = [jnp.concatenate([q_ref[:, 256 * j:256 * j + 128], q_ref[:, 256 * j + 128:256 * (j + 1)]], axis=0)
          for j in range(2)]

    def update(carry, kv_blocks, masks):
        out = []
        for j in range(2):
            lanes = slice(128 * j, 128 * (j + 1))
            s_blocks = []
            for (k_blk, _), mask in zip(kv_blocks, masks):
                s = _dot_nt(q2[j], k_blk[:, lanes])
                s_blocks.append(s if mask is None else jnp.where(mask, s, -jnp.inf))
            out.append(_online(*carry[j], s_blocks, [v_blk[:, lanes] for _, v_blk in kv_blocks]))
        return tuple(out)

    if sink_ref is not None:
        upper = lax.broadcasted_iota(jnp.int32, (rows, 128), 1) >= HEAD_DIM
        carry = tuple((jnp.max(sink_ref[j], axis=-1, keepdims=True) * LOG2E, jnp.where(upper, 1.0, 0.0))
                      for j in range(2))
    else:
        carry = tuple((jnp.full((rows, 1), -jnp.inf, F32), jnp.zeros((rows, 128), F32)) for _ in range(2))
    if window:
        span = tq + 2 * WINDOW
        start = pl.multiple_of(jnp.clip(qi * tq - WINDOW, 0, seq_len - span), 128)
        qpos = qi * tq + lax.broadcasted_iota(jnp.int32, (rows, span), 0) % tq
        kpos = start + lax.broadcasted_iota(jnp.int32, (rows, span), 1)
        blocks = [(k_ref[pl.ds(start, span), :], v_ref[pl.ds(start, span), :])]
        masks = [jnp.abs(qpos - kpos) <= WINDOW]
        if kc_ref is not None:
            blocks.append((kc_ref[...], vc_ref[...]))
            masks.append(None)
        carry = update(carry, blocks, masks)
    else:
        for c in range(seq_len // kv_chunk):
            blk = slice(c * kv_chunk, (c + 1) * kv_chunk)
            carry = update(carry, [(k_ref[blk, :], v_ref[blk, :])], [None])
        if kc_ref is not None:
            carry = update(carry, [(kc_ref[...], vc_ref[...])], [None])

    low = lax.broadcasted_iota(jnp.int32, (tq, 128), 1) < HEAD_DIM
    for j in range(2):
        acc = carry[j][1]
        o = acc / pltpu.roll(acc, HEAD_DIM, 1)
        o_ref[:, 128 * j:128 * (j + 1)] = jnp.where(low, o[:tq], pltpu.roll(o[tq:], HEAD_DIM, 1)).astype(o_ref.dtype)


def _attn_kernel(*refs, seq_len, tq, kv_chunk, mixers):
    n_in = sum(3 + 2 * c + s for c, s, _ in mixers)
    outs = refs[n_in:]
    i = 0
    for o_ref, (has_cache, has_sink, window) in zip(outs, mixers):
        q_ref, k_ref, v_ref = refs[i:i + 3]
        i += 3
        kc_ref, vc_ref = refs[i:i + 2] if has_cache else (None, None)
        i += 2 * has_cache
        sink_ref = refs[i] if has_sink else None
        i += has_sink
        _attend(q_ref, k_ref, v_ref, kc_ref, vc_ref, sink_ref, o_ref, qi=pl.program_id(1),
                seq_len=seq_len, tq=tq, kv_chunk=kv_chunk, window=window)


def _attn_call(mixers, nb, seq_len):
    tq = min(ATTN_TILE, seq_len)
    nq = seq_len // tq
    kv_chunk = min(1024, seq_len)
    in_specs, args, static = [], [], []
    for mx in mixers:
        in_specs += [pl.BlockSpec((tq, 512), lambda b, i: (b * nq + i, 0)),
                     pl.BlockSpec((seq_len, 256), lambda b, i: (b, 0)),
                     pl.BlockSpec((seq_len, 256), lambda b, i: (b, 0))]
        args += [mx["q"], mx["k"], mx["v"]]
        if mx.get("cache") is not None:
            in_specs += [pl.BlockSpec((None, PAST_LEN, 256), lambda b, i: (b, 0, 0))] * 2
            args += list(mx["cache"])
        if mx.get("sink") is not None:
            in_specs.append(pl.BlockSpec((2, 2 * tq, 128), lambda b, i: (0, 0, 0)))
            args.append(mx["sink"])
        static.append((mx.get("cache") is not None, mx.get("sink") is not None, bool(mx.get("window"))))
    out = jax.ShapeDtypeStruct((nb * seq_len, 256), BF16)
    out_spec = pl.BlockSpec((tq, 256), lambda b, i: (b * nq + i, 0))
    return pl.pallas_call(
        functools.partial(_attn_kernel, seq_len=seq_len, tq=tq, kv_chunk=kv_chunk, mixers=tuple(static)),
        out_shape=[out] * len(mixers), grid=(nb, nq), in_specs=in_specs, out_specs=[out_spec] * len(mixers),
        compiler_params=_params(("arbitrary", "arbitrary")), name="attention",
    )(*args)


def _chunk_of(s, nc):
    return jnp.where(s < nc, nc - 1 - s, s - nc)


STEP_CHUNKS = 2


def _when(always, pred):
    if always:
        return lambda fn: fn()
    return pl.when(pred)


def _run_sweeps(single, s, ns, backward, forward):
    if single:
        backward()
        forward()
    else:
        pl.when(s < ns)(backward)
        pl.when(s >= ns)(forward)


def _chunks_per_step(nc):
    return STEP_CHUNKS if nc % STEP_CHUNKS == 0 else 1


def _log_gamma(rd):
    return jnp.log1p(-jnp.exp(rd))


def _ret_kernel(*refs, nc, sub, has_init, emit_state):
    q_ref, k_ref, v_ref, g_ref, rdl_ref, rdh_ref, gn_ref, seg_ref = refs[:8]
    i = 8
    if has_init:
        s0_ref = refs[i]
        i += 1
    y_ref = refs[i]
    i += 1
    if emit_state:
        st_ref = refs[i]
        i += 1
    m_sc, dec_sc, qk_sc, sf_sc, sb_sc, sball_sc = refs[i:i + 6]
    c_len = CHUNK
    b = pl.program_id(0)
    s = pl.program_id(1)

    @pl.when(jnp.logical_and(b == 0, s == 0))
    def _():
        ri = lax.broadcasted_iota(jnp.int32, (c_len, c_len), 0)
        ci = lax.broadcasted_iota(jnp.int32, (c_len, c_len), 1)
        d = (ri - ci).astype(F32)
        for h in range(N_HEADS):
            lgf = _log_gamma(jnp.broadcast_to(rdh_ref[0, h], (c_len, c_len)))
            lgb = _log_gamma(jnp.broadcast_to(rdh_ref[1, h], (c_len, c_len)))
            m_sc[h] = jnp.where(d > 0, jnp.exp(d * lgf), jnp.where(d < 0, jnp.exp(-d * lgb), 2.0))
        for dr in range(2):
            rows = [jnp.exp(c_len * _log_gamma(jnp.broadcast_to(rdh_ref[dr, h], (HEAD_DIM, c_len))))
                    for h in range(N_HEADS)]
            dec_sc[dr] = jnp.concatenate(rows, axis=0)
        a = lax.broadcasted_iota(jnp.int32, (c_len, 256), 0).astype(F32)
        lgf = _log_gamma(rdl_ref[0])
        lgb = _log_gamma(rdl_ref[1])
        qk_sc[0] = jnp.exp((a + 1.0) * lgf)
        qk_sc[1] = jnp.exp((c_len - 1.0 - a) * lgf)
        qk_sc[2] = jnp.exp((c_len - a) * lgb)
        qk_sc[3] = jnp.exp(a * lgb)

    @pl.when(s == 0)
    def _():
        sf_sc[...] = jnp.zeros_like(sf_sc)
        sb_sc[...] = jnp.zeros_like(sb_sc)
        if has_init:
            for h in range(N_HEADS):
                blk = slice(HEAD_DIM * h, HEAD_DIM * (h + 1))
                sf_sc[blk, blk] = s0_ref[0, h]
                sb_sc[blk, blk] = s0_ref[1, h]

    ri = lax.broadcasted_iota(jnp.int32, (256, 256), 0) // HEAD_DIM
    ci = lax.broadcasted_iota(jnp.int32, (256, 256), 1) // HEAD_DIM
    diag = ri == ci
    ns = nc // sub
    single = ns == 1
    sf = 0 if single else s - ns

    def backward():
        state = sb_sc[...]
        for u in reversed(range(sub)):
            rows = slice(u * c_len, (u + 1) * c_len)
            sball_sc[(ns - 1 - s) * sub + u] = state
            kd = (k_ref[rows, :].astype(F32) * qk_sc[3]).astype(BF16)
            state = state * dec_sc[1] + jnp.where(diag, _dot_tn(kd, v_ref[rows, :]), 0.0)
        sb_sc[...] = state
        if emit_state:
            @_when(single, s == ns - 1)
            def _():
                for h in range(N_HEADS):
                    blk = slice(HEAD_DIM * h, HEAD_DIM * (h + 1))
                    st_ref[1, h] = sb_sc[blk, blk]

    def forward():
        state = sf_sc[...]
        lane_head = lax.broadcasted_iota(jnp.int32, (c_len, 256), 1) // HEAD_DIM
        seg = seg_ref[...]
        for u in range(sub):
            rows = slice(u * c_len, (u + 1) * c_len)
            qb = q_ref[rows, :]
            kb = k_ref[rows, :]
            vb = v_ref[rows, :]
            q = qb.astype(F32)
            o = (_dot(qb, state.astype(BF16)) * qk_sc[0]
                 + _dot(qb, sball_sc[sf * sub + u].astype(BF16)) * qk_sc[2])
            for h in range(N_HEADS):
                mine = lane_head == h
                sc = _dot_nt(jnp.where(mine, q, 0.0).astype(BF16), kb)
                pv = _dot((sc * m_sc[h]).astype(BF16), vb)
                o = o + jnp.where(mine, pv, 0.0)
            mu = _seg_sum(o, seg) * (1.0 / HEAD_DIM)
            dlt = o - mu
            var = _seg_sum(dlt * dlt, seg) * (1.0 / HEAD_DIM)
            on = dlt * lax.rsqrt(var + 1e-5) * gn_ref[...]
            y_ref[rows, :] = (on * _silu(g_ref[rows, :])).astype(y_ref.dtype)
            kd = (kb.astype(F32) * qk_sc[1]).astype(BF16)
            state = state * dec_sc[0] + jnp.where(diag, _dot_tn(kd, vb), 0.0)
        sf_sc[...] = state
        if emit_state:
            @_when(single, sf == ns - 1)
            def _():
                for h in range(N_HEADS):
                    blk = slice(HEAD_DIM * h, HEAD_DIM * (h + 1))
                    st_ref[0, h] = sf_sc[blk, blk]

    _run_sweeps(single, s, ns, backward, forward)


def _ret_specs(rqkv, pf, nb, seq_len, rd_lane, rd_head, gn_g, seg, layer, state0=None, emit_state=False):
    c_len = CHUNK
    nc = seq_len // c_len
    sub = _chunks_per_step(nc)
    ns = nc // sub
    rows = sub * c_len

    def col(start):
        return pl.BlockSpec((rows, 256), lambda b, s: (b * ns + _chunk_of(s, ns), start // 256))

    in_specs = [col(0), col(256), col(512), col(F_RG),
                pl.BlockSpec((2, 1, 256), lambda b, s: (0, 0, 0)),
                pl.BlockSpec((2, N_HEADS, 1, c_len), lambda b, s: (0, 0, 0, 0)),
                pl.BlockSpec((1, 256), lambda b, s: (0, 0)),
                pl.BlockSpec((256, 256), lambda b, s: (0, 0))]
    args = [rqkv, rqkv, rqkv, pf, rd_lane, rd_head, gn_g, seg]
    if state0 is not None:
        in_specs.append(pl.BlockSpec((None, None, 2, N_HEADS, HEAD_DIM, HEAD_DIM),
                                     lambda b, s: (b, layer, 0, 0, 0, 0)))
        args.append(state0)
    outs = [jax.ShapeDtypeStruct((nb * seq_len, 256), BF16)]
    out_specs = [pl.BlockSpec((rows, 256), lambda b, s: (b * ns + jnp.maximum(s - ns, 0), 0))]
    if emit_state:
        outs.append(jax.ShapeDtypeStruct((nb, DEPTH, 2, N_HEADS, HEAD_DIM, HEAD_DIM), F32))
        out_specs.append(pl.BlockSpec((None, None, 2, N_HEADS, HEAD_DIM, HEAD_DIM),
                                      lambda b, s: (b, layer, 0, 0, 0, 0)))
    scratch = [pltpu.VMEM((N_HEADS, c_len, c_len), F32), pltpu.VMEM((2, 256, 256), F32),
               pltpu.VMEM((4, c_len, 256), F32), pltpu.VMEM((256, 256), F32),
               pltpu.VMEM((256, 256), F32), pltpu.VMEM((nc, 256, 256), F32)]
    return in_specs, args, outs, out_specs, scratch


def _cumsum_rows(x):
    n = x.shape[0]
    row = lax.broadcasted_iota(jnp.int32, x.shape, 0)
    sh = 1
    while sh < n:
        x = x + jnp.where(row >= sh, pltpu.roll(x, sh, 0), 0.0)
        sh *= 2
    return x


def _ssd_kernel(*refs, nc, sub, has_init, emit_state):
    xbc_ref, prev_ref, next_ref, z_ref, dt_ref, cw_ref, cb_ref, dtb_ref, al_ref, dl_ref, ng_ref = refs[:11]
    i = 11
    if has_init:
        h0_ref = refs[i]
        i += 1
    y_ref = refs[i]
    i += 1
    if emit_state:
        st_ref = refs[i]
        i += 1
    hf_sc, hb_sc, hball_sc, xs_sc, bm_sc, cm_sc, dt_sc, cum_sc, rev_sc = refs[i:i + 9]
    c_len = CHUNK
    ns = nc // sub
    s = pl.program_id(1)
    single = ns == 1
    sf = 0 if single else s - ns

    @pl.when(s == 0)
    def _():
        if has_init:
            hf_sc[...] = h0_ref[0]
            hb_sc[...] = h0_ref[1]
        else:
            hf_sc[...] = jnp.zeros_like(hf_sc)
            hb_sc[...] = jnp.zeros_like(hb_sc)

    low = lax.broadcasted_iota(jnp.int32, (c_len, 128), 1) < HEAD_DIM
    low_rows = lax.broadcasted_iota(jnp.int32, (128, 128), 0) < HEAD_DIM

    def col(a, lane):
        return jnp.broadcast_to(a[:, lane:lane + 1], (c_len, c_len))

    def pair(a0, a1):
        return jnp.where(low, a0[:, :128], a1[:, :128])

    def state_update(h, g, xs, bm, dt, cols, tot_row, dt_lanes):
        w = [jnp.exp2(tot_row[r][:, :128] - cols[r][:, :128]) * col(dt, dt_lanes[r])[:, :128] for r in range(2)]
        xw = (xs[:, 128 * g:128 * (g + 1)] * jnp.where(low, w[0], w[1])).astype(BF16)
        dec = jnp.where(low_rows, jnp.broadcast_to(jnp.exp2(tot_row[0][:, :128]), (128, 128)),
                        jnp.broadcast_to(jnp.exp2(tot_row[1][:, :128]), (128, 128)))
        return h * dec + _dot_tn(xw, bm[:, 128 * g:128 * (g + 1)])

    def backward():
        cg = ns - 1 - s
        n_rows = sub * c_len
        xbc = xbc_ref[...]
        before = jnp.where(cg > 0, prev_ref[7:8, :], 0.0)
        after = jnp.where(cg < ns - 1, next_ref[0:1, :], 0.0)
        row = lax.broadcasted_iota(jnp.int32, xbc.shape, 0)
        x_prev = jnp.where(row == 0, before, pltpu.roll(xbc, 1, 0))
        x_next = jnp.where(row == n_rows - 1, after, pltpu.roll(xbc, n_rows - 1, 0))
        conv = _silu(cw_ref[0:1, :] * x_prev + cw_ref[1:2, :] * xbc + cw_ref[2:3, :] * x_next + cb_ref[...])
        pre = dt_ref[...] + dtb_ref[...]
        dt_all = jnp.maximum(pre, 0.0) + jnp.log1p(jnp.exp(-jnp.abs(pre)))
        dta_all = dt_all * (-jnp.exp(al_ref[...]) * LOG2E)
        state = [hb_sc[g] for g in range(2)]
        for u in reversed(range(sub)):
            rows = slice(u * c_len, (u + 1) * c_len)
            c = cg * sub + u
            xs = conv[rows, 0:256]
            bm = conv[rows, 256:512].astype(BF16)
            dt = dt_all[rows, :]
            dta = dta_all[rows, :]
            cum = _cumsum_rows(dta)
            rev = cum[c_len - 1:c_len, :] - cum + dta
            xs_sc[c] = xs
            bm_sc[c] = bm
            cm_sc[c] = conv[rows, 512:768].astype(BF16)
            dt_sc[c] = dt
            cum_sc[c] = cum
            rev_sc[c] = rev
            for g in range(2):
                hball_sc[c, g] = state[g]
                cols = [col(rev, 4 + 2 * g + r) for r in range(2)]
                tot = [cl[0:1, :] for cl in cols]
                state[g] = state_update(state[g], g, xs, bm, dt, cols, tot, [4 + 2 * g, 5 + 2 * g])
        for g in range(2):
            hb_sc[g] = state[g]
        if emit_state:
            @_when(single, s == ns - 1)
            def _():
                st_ref[1] = hb_sc[...]

    def forward():
        ri = lax.broadcasted_iota(jnp.int32, (c_len, c_len), 0)
        ci = lax.broadcasted_iota(jnp.int32, (c_len, c_len), 1)
        below = ri > ci
        on_diag = ri == ci
        lane_head = lax.broadcasted_iota(jnp.int32, (c_len, 256), 1) // HEAD_DIM
        state = [hf_sc[g] for g in range(2)]
        for u in range(sub):
            rows = slice(u * c_len, (u + 1) * c_len)
            c = sf * sub + u
            xs = xs_sc[c]
            bm = bm_sc[c]
            cm = cm_sc[c]
            dt = dt_sc[c]
            cum = cum_sc[c]
            rev = rev_sc[c]
            log_dt = jnp.log2(dt)
            off_f = (log_dt - cum).T
            off_b = (log_dt - rev).T
            dt_t = dt.T
            xs_b = xs.astype(BF16)
            y = jnp.zeros((c_len, 256), F32)
            inter = []
            for g in range(2):
                gl = slice(128 * g, 128 * (g + 1))
                gmat = _dot_nt(cm[:, gl], bm[:, gl])
                cols_f, cols_b = [], []
                for r in range(2):
                    h = 2 * g + r
                    cf = col(cum, h)
                    cb = col(rev, 4 + h)
                    cols_f.append(cf)
                    cols_b.append(cb)
                    w = jnp.exp2(jnp.where(below, cf + off_f[h:h + 1, :], cb + off_b[4 + h:5 + h, :]))
                    w = w + jnp.where(on_diag, dt_t[h:h + 1, :], 0.0)
                    yh = _dot((gmat * w).astype(BF16), xs_b)
                    y = y + jnp.where(lane_head == h, yh, 0.0)
                yf = _dot_nt(cm[:, gl], state[g].astype(BF16))
                yb = _dot_nt(cm[:, gl], hball_sc[c, g].astype(BF16))
                inter.append(yf * pair(jnp.exp2(cols_f[0]), jnp.exp2(cols_f[1]))
                             + yb * pair(jnp.exp2(cols_b[0]), jnp.exp2(cols_b[1])))
                tot = [cl[c_len - 1:c_len, :] for cl in cols_f]
                state[g] = state_update(state[g], g, xs, bm, dt, cols_f, tot, [2 * g, 2 * g + 1])
            y = y + jnp.concatenate(inter, axis=1) + xs * dl_ref[...]
            y = y * _silu(z_ref[rows, :])
            ms = jnp.mean(y * y, axis=-1, keepdims=True)
            y_ref[rows, :] = (y * lax.rsqrt(ms + 1e-6) * ng_ref[...]).astype(y_ref.dtype)
        for g in range(2):
            hf_sc[g] = state[g]
        if emit_state:
            @_when(single, sf == ns - 1)
            def _():
                st_ref[0] = hf_sc[...]

    _run_sweeps(single, s, ns, backward, forward)


def _ssd_specs(p, nb, seq_len, conv_w, conv_b, dt_bias, a_log, d_lane, norm_g, layer,
              state0=None, emit_state=False):
    c_len = CHUNK
    nc = seq_len // c_len
    sub = _chunks_per_step(nc)
    ns = nc // sub
    rows = sub * c_len
    m = nb * seq_len
    r8 = rows // 8
    chunk = lambda b, s: b * ns + jnp.maximum(ns - 1 - s, 0)
    in_specs = [pl.BlockSpec((rows, 768), lambda b, s: (chunk(b, s), F_XBC // 768)),
                pl.BlockSpec((8, 768), lambda b, s: (jnp.maximum(chunk(b, s) * r8 - 1, 0), F_XBC // 768)),
                pl.BlockSpec((8, 768), lambda b, s: (jnp.minimum((chunk(b, s) + 1) * r8, m // 8 - 1),
                                                     F_XBC // 768)),
                pl.BlockSpec((rows, 256), lambda b, s: (b * ns + jnp.maximum(s - ns, 0), F_SZ // 256)),
                pl.BlockSpec((rows, 128), lambda b, s: (chunk(b, s), F_DT // 128)),
                pl.BlockSpec((None, 3, 768), lambda b, s: (layer, 0, 0)),
                pl.BlockSpec((1, 768), lambda b, s: (0, 0)),
                pl.BlockSpec((1, 128), lambda b, s: (0, 0)),
                pl.BlockSpec((1, 128), lambda b, s: (0, 0)),
                pl.BlockSpec((1, 256), lambda b, s: (0, 0)),
                pl.BlockSpec((1, 256), lambda b, s: (0, 0))]
    args = [p, p, p, p, p, conv_w, conv_b, dt_bias, a_log, d_lane, norm_g]
    if state0 is not None:
        in_specs.append(pl.BlockSpec((None, None, 2, 2, 128, SSD_STATE),
                                     lambda b, s: (b, layer, 0, 0, 0, 0)))
        args.append(state0)
    outs = [jax.ShapeDtypeStruct((m, 256), BF16)]
    out_specs = [pl.BlockSpec((rows, 256), lambda b, s: (b * ns + jnp.maximum(s - ns, 0), 0))]
    if emit_state:
        outs.append(jax.ShapeDtypeStruct((nb, DEPTH, 2, 2, 128, SSD_STATE), F32))
        out_specs.append(pl.BlockSpec((None, None, 2, 2, 128, SSD_STATE),
                                      lambda b, s: (b, layer, 0, 0, 0, 0)))
    scratch = [pltpu.VMEM((2, 128, SSD_STATE), F32), pltpu.VMEM((2, 128, SSD_STATE), F32),
               pltpu.VMEM((nc, 2, 128, SSD_STATE), F32),
               pltpu.VMEM((nc, c_len, 256), F32), pltpu.VMEM((nc, c_len, 256), BF16),
               pltpu.VMEM((nc, c_len, 256), BF16), pltpu.VMEM((nc, c_len, 128), F32),
               pltpu.VMEM((nc, c_len, 128), F32), pltpu.VMEM((nc, c_len, 128), F32)]
    return in_specs, args, outs, out_specs, scratch


def _scan_kernel(*refs, counts, **static):
    (ri, si), (ro, so), (rs, ss), n_alias = counts
    ins, rest = refs[:ri + si], refs[ri + si + n_alias:]
    outs, scr = rest[:ro + so], rest[ro + so:]
    _ret_kernel(*ins[:ri], *outs[:ro], *scr[:rs], **static)
    _ssd_kernel(*ins[ri:], *outs[ro:], *scr[rs:], **static)


def _scan_call(ret_parts, ssd_parts, nb, seq_len, has_init, emit_state, states_prev=None):
    nc = seq_len // CHUNK
    sub = _chunks_per_step(nc)
    in_specs = ret_parts[0] + ssd_parts[0]
    args = ret_parts[1] + ssd_parts[1]
    aliases = {}
    if states_prev is not None:
        aliases = {len(args): 1, len(args) + 1: len(ret_parts[2]) + 1}
        in_specs = in_specs + [pl.BlockSpec(memory_space=pl.ANY)] * 2
        args = args + list(states_prev)
    counts = tuple((len(a), len(b)) for a, b in zip((ret_parts[0], ret_parts[2], ret_parts[4]),
                                                    (ssd_parts[0], ssd_parts[2], ssd_parts[4])))
    return pl.pallas_call(
        functools.partial(_scan_kernel, counts=counts + (len(aliases),), nc=nc, sub=sub, has_init=has_init,
                          emit_state=emit_state),
        out_shape=ret_parts[2] + ssd_parts[2], grid=(nb, 1 if nc == sub else 2 * nc // sub),
        in_specs=in_specs, out_specs=ret_parts[3] + ssd_parts[3],
        scratch_shapes=ret_parts[4] + ssd_parts[4], input_output_aliases=aliases,
        compiler_params=_params(("arbitrary", "arbitrary")), name="scans",
    )(*args)


def _layer_norm(x, g, b):
    mu = jnp.mean(x, axis=-1, keepdims=True)
    d = x - mu
    var = jnp.mean(d * d, axis=-1, keepdims=True)
    return d * lax.rsqrt(var + 1e-5) * g + b


FFN_CHUNK = 256
FFN_ROWS = 512


def _outffn_kernel(x_ref, yr_ref, yg_ref, ys_ref, yw_ref, g1_ref, sh2_ref, sc2_ref, g2_ref,
                   wo_ref, wi_ref, wf_ref, l1g_ref, l1b_ref, l2g_ref, l2b_ref, o_ref):
    mix = (_dot(yr_ref[...], wo_ref[0:256, :]) + _dot(yg_ref[...], wo_ref[256:512, :])
           + _dot(ys_ref[...], wo_ref[512:768, :]) + _dot(yw_ref[...], wo_ref[768:1024, :]))
    x1 = _layer_norm(ALPHA * x_ref[...] + g1_ref[...] * mix, l1g_ref[...], l1b_ref[...])
    h2 = (x1 * (1.0 + sc2_ref[...]) + sh2_ref[...]).astype(BF16)
    acc = None
    for j in range(D_FF // FFN_CHUNK):
        cols = slice(FFN_CHUNK * j, FFN_CHUNK * (j + 1))
        gate = _dot(h2, wi_ref[:, cols])
        up = _dot(h2, wi_ref[:, D_FF + FFN_CHUNK * j:D_FF + FFN_CHUNK * (j + 1)])
        d = _dot((_silu(gate) * up).astype(BF16), wf_ref[cols, :])
        acc = d if acc is None else acc + d
    o_ref[...] = _layer_norm(ALPHA * x1 + g2_ref[...] * acc, l2g_ref[...], l2b_ref[...])


def _outffn_call(x, ys, mod5, w_out_b, w_ffn_in_b, w_ffn_out_b, ln, layer, row0, tiles_per_row):
    m = x.shape[0]
    tm = min(FFN_ROWS, m)
    tiles_per_row = max(tiles_per_row * TOKEN_TILE // tm, 1)
    row = lambda w: pl.BlockSpec((tm, w), lambda i: (i, 0))
    once = pl.Buffered(1)
    vec = pl.BlockSpec((None, 1, D_MODEL), lambda i: (layer, 0, 0))
    in_specs = [row(D_MODEL), row(256), row(256), row(256), row(256),
                _mod_spec(layer, row0, tiles_per_row, 2), _mod_spec(layer, row0, tiles_per_row, 3),
                _mod_spec(layer, row0, tiles_per_row, 4), _mod_spec(layer, row0, tiles_per_row, 5),
                pl.BlockSpec((None, D_MODEL, D_MODEL), lambda i: (layer, 0, 0), pipeline_mode=once),
                pl.BlockSpec((None, D_MODEL, 2 * D_FF), lambda i: (layer, 0, 0), pipeline_mode=once),
                pl.BlockSpec((None, D_FF, D_MODEL), lambda i: (layer, 0, 0), pipeline_mode=once),
                vec, vec, vec, vec]
    return pl.pallas_call(
        _outffn_kernel,
        out_shape=jax.ShapeDtypeStruct((m, D_MODEL), F32),
        grid=(m // tm,), in_specs=in_specs, out_specs=row(D_MODEL),
        compiler_params=_params(("arbitrary",)), name="out_ffn",
    )(x, *ys, mod5, mod5, mod5, mod5, w_out_b, w_ffn_in_b, w_ffn_out_b, *ln)


def _rope_tables(seq_len):
    t = jnp.arange(seq_len)
    rowp = (t // GRID_W).astype(F32)
    colp = (t % GRID_W).astype(F32)
    quarter = HEAD_DIM // 4
    inv_freq = ROPE_THETA ** (-jnp.arange(quarter, dtype=F32) / quarter)
    ang_r = rowp[:, None] * inv_freq[None, :]
    ang_c = colp[:, None] * inv_freq[None, :]
    cos = jnp.concatenate([jnp.cos(ang_r)] * 2 + [jnp.cos(ang_c)] * 2, axis=-1)
    sin = jnp.concatenate([-jnp.sin(ang_r), jnp.sin(ang_r), -jnp.sin(ang_c), jnp.sin(ang_c)], axis=-1)
    return jnp.tile(cos, (1, N_HEADS)), jnp.tile(sin, (1, N_HEADS))


def _expand_cache(c, fill):
    pad = jnp.full(c.shape[:2] + (HEAD_DIM,), fill, c.dtype)
    return jnp.concatenate([c[:, :, 0], pad, c[:, :, 1], pad], axis=-1).astype(BF16)


def _split_w_in(w):
    pad = jnp.zeros(w.shape[:-1] + (TAIL_W - T_DT - 8,), w.dtype)
    tail = jnp.concatenate([w[..., MAIN_W + 8:], w[..., MAIN_W:MAIN_W + 8], pad], axis=-1)
    return w.astype(BF16), tail.astype(BF16)


def kernel(x_prompt, x_sample, cache_gqa_k, cache_gqa_v, cache_swa_k, cache_swa_v, state_ret, state_ssd,
           c, c_ctx, ada_w, ada_b, w_in, ret_decay, ret_gn_g, gqa_q_norm, gqa_k_norm,
           ssd_conv_w, ssd_conv_b, ssd_dt_bias, ssd_a_log, ssd_d, ssd_norm_g, swa_sink, w_out,
           ln1_g, ln1_b, w_ffn_in, w_ffn_out, ln2_g, ln2_b):
    nb_c, len_c, _ = x_prompt.shape
    nb_l, len_l, _ = x_sample.shape
    depth = ada_w.shape[0]

    cond8 = jnp.zeros((8, D_MODEL), F32).at[0].set(c_ctx).at[1:1 + nb_l].set(c)
    mod5 = _ada_call(cond8, ada_w, ada_b)

    w_main, w_tail = _split_w_in(w_in)
    w_out_b = w_out.astype(BF16)
    w_ffn_in_b = w_ffn_in.astype(BF16)
    w_ffn_out_b = w_ffn_out.astype(BF16)
    lane_head = jnp.arange(256) // HEAD_DIM
    seg = (lane_head[:, None] == lane_head[None, :]).astype(BF16)
    rope_tabs = _rope_tables(len_l)
    caches_g = (_expand_cache_layers(cache_gqa_k, 0.0), _expand_cache_layers(cache_gqa_v, 1.0))
    caches_w = (_expand_cache_layers(cache_swa_k, 0.0), _expand_cache_layers(cache_swa_v, 1.0))
    state_ssd_g = state_ssd.reshape(nb_l, depth, 2, 2, 128, SSD_STATE)

    def layer_consts(l):
        rd = ret_decay[l]
        return dict(
            rd_lane=jnp.repeat(rd, HEAD_DIM, axis=-1).reshape(2, 1, 256),
            rd_head=jnp.broadcast_to(rd[:, :, None, None], (2, N_HEADS, 1, CHUNK)),
            gn_g=ret_gn_g[l].reshape(1, 256),
            qn=jnp.tile(gqa_q_norm[l], N_HEADS).reshape(1, 256),
            kn=jnp.tile(gqa_k_norm[l], N_HEADS).reshape(1, 256),
            conv_b=ssd_conv_b[l].reshape(1, 768),
            dt_bias=jnp.zeros((1, 128), F32).at[0, :8].set(ssd_dt_bias[l].reshape(8)),
            a_log=jnp.zeros((1, 128), F32).at[0, :8].set(ssd_a_log[l].reshape(8)),
            d_lane=jnp.repeat(ssd_d[l], HEAD_DIM).reshape(1, 256),
            norm_g=ssd_norm_g[l].reshape(1, 256),
            sink=lambda tq: jnp.broadcast_to(swa_sink[l].reshape(2, 2, 1, 1), (2, 2, tq, 128)
                                             ).reshape(2, 2 * tq, 128),
        )

    ln = tuple(a.reshape(depth, 1, D_MODEL) for a in (ln1_g, ln1_b, ln2_g, ln2_b))

    def run_group(x, nb, seq_len, row0, latent):
        m = nb * seq_len
        tpr = (seq_len // TOKEN_TILE) if latent else (m // TOKEN_TILE)
        x = x.reshape(m, D_MODEL)
        new_kv = states = None
        if not latent:
            new_kv = tuple(jnp.zeros((nb, depth, seq_len, 128), F32) for _ in range(4))
            states = (jnp.zeros((nb, depth, 2, N_HEADS, HEAD_DIM, HEAD_DIM), F32),
                      jnp.zeros((nb, depth, 2, 2, 128, SSD_STATE), F32))
        for l in range(depth):
            k = layer_consts(l)
            proj = _inproj_call(x, mod5, w_main, w_tail, l, row0, tpr, k["qn"], k["kn"], seg,
                                rope_tabs if latent else None, seq_len, new_kv_prev=new_kv)
            pf, rqkv, qg, kg, vg, qw, kw, vw = proj[:8]
            ret_parts = _ret_specs(rqkv, pf, nb, seq_len, k["rd_lane"], k["rd_head"], k["gn_g"], seg, l,
                                   state0=state_ret if latent else None, emit_state=not latent)
            ssd_parts = _ssd_specs(pf, nb, seq_len, ssd_conv_w, k["conv_b"], k["dt_bias"], k["a_log"],
                                   k["d_lane"], k["norm_g"], l,
                                   state0=state_ssd_g if latent else None, emit_state=not latent)
            scans = _scan_call(ret_parts, ssd_parts, nb, seq_len, has_init=latent, emit_state=not latent,
                               states_prev=states)
            if latent:
                y_ret, y_ssd = scans
                y_gqa, y_swa = _attn_call(
                    [dict(q=qg, k=kg, v=vg, cache=(caches_g[0][l], caches_g[1][l])),
                     dict(q=qw, k=kw, v=vw, cache=(caches_w[0][l], caches_w[1][l]),
                          sink=k["sink"](min(ATTN_TILE, seq_len)), window=True)], nb, seq_len)
            else:
                y_ret, st_ret, y_ssd, st_ssd = scans
                states = (st_ret, st_ssd)
                new_kv = proj[8:12]
                y_gqa, = _attn_call([dict(q=qg, k=kg, v=vg)], nb, seq_len)
                y_swa, = _attn_call([dict(q=qw, k=kw, v=vw, sink=k["sink"](min(ATTN_TILE, seq_len)))],
                                    nb, seq_len)
            x = _outffn_call(x, (y_ret, y_gqa, y_ssd, y_swa), mod5, w_out_b, w_ffn_in_b, w_ffn_out_b,
                             ln, l, row0, tpr)
        extras = None
        if not latent:
            extras = tuple(a.reshape(nb, depth, seq_len, 2, HEAD_DIM) for a in new_kv) + (
                states[0], states[1].reshape(nb, depth, 2, N_HEADS, HEAD_DIM, SSD_STATE))
        return x.reshape(nb, seq_len, D_MODEL), extras

    y_prompt, extras = run_group(x_prompt, nb_c, len_c, 0, False)
    y_sample, _ = run_group(x_sample, nb_l, len_l, 1, True)
    return (y_prompt, y_sample) + extras


def _expand_cache_layers(cache, fill):
    return [_expand_cache(cache[:, l], fill) for l in range(cache.shape[1])]
```

```python
import functools
import math

import jax
import jax.numpy as jnp
from jax import lax
from jax.experimental import pallas as pl
from jax.experimental.pallas import tpu as pltpu

F32 = jnp.float32
BF16 = jnp.bfloat16

D_MODEL = 1024
DEPTH = 4
HEAD_DIM = 64
N_HEADS = 4
GRID_W = 64
ROPE_THETA = 10000.0
SSD_STATE = 128
D_FF = 2816
WINDOW = 128
PAST_LEN = 256
ALPHA = (2.0 * DEPTH) ** 0.25
CHUNK = 256
TOKEN_TILE = 256
INPROJ_TILE = 512
ATTN_TILE = 512
VMEM_LIMIT = 56 * 1024 * 1024
LOG2E = math.log2(math.e)
QK_SCALE = HEAD_DIM ** -0.5 * LOG2E

M_RET, M_RG, M_GQA, M_SZ, M_XBC, MAIN_W = 0, 768, 1024, 1536, 1792, 2560
T_SWA, T_DT, TAIL_W = 0, 512, 640
F_XBC, F_RG, F_SZ, F_DT = 0, 768, 1024, 1280
F_WIDTH = 1408


def _silu(x):
    return x / (1.0 + jnp.exp(-x))


def _dot(a, b):
    return jnp.dot(a, b, preferred_element_type=F32)


def _dot_nt(a, b):
    return lax.dot_general(a, b, (((1,), (1,)), ((), ())), preferred_element_type=F32)


def _dot_tn(a, b):
    return lax.dot_general(a, b, (((0,), (0,)), ((), ())), preferred_element_type=F32)


def _seg_sum(x, seg):
    hi = x.astype(BF16)
    lo = (x - hi.astype(F32)).astype(BF16)
    return _dot(hi, seg) + _dot(lo, seg)


def _params(sem):
    return pltpu.CompilerParams(dimension_semantics=sem, vmem_limit_bytes=VMEM_LIMIT)


def _ada_kernel(c_ref, w_ref, b_ref, o_ref):
    s = _silu(c_ref[...])
    o_ref[...] = _dot(s.astype(BF16), w_ref[...].astype(BF16)) + b_ref[...]


def _ada_call(cond8, ada_w, ada_b):
    tn = D_MODEL
    nt = 6
    out = pl.pallas_call(
        _ada_kernel,
        out_shape=jax.ShapeDtypeStruct((DEPTH, nt, 8, tn), F32),
        grid=(DEPTH, nt),
        in_specs=[pl.BlockSpec((8, D_MODEL), lambda l, j: (0, 0)),
                  pl.BlockSpec((None, D_MODEL, tn), lambda l, j: (l, 0, j)),
                  pl.BlockSpec((None, None, 1, tn), lambda l, j: (l, j, 0, 0))],
        out_specs=pl.BlockSpec((None, None, 8, tn), lambda l, j: (l, j, 0, 0)),
        compiler_params=_params(("arbitrary", "arbitrary")),
        name="ada_mod",
    )(cond8, ada_w, ada_b.reshape(DEPTH, nt, 1, tn))
    return jnp.transpose(out, (0, 2, 1, 3)).reshape(DEPTH, 8, nt, 1, tn)


def _mod_spec(layer, row0, tiles_per_row, k):
    return pl.BlockSpec((None, None, None, 1, D_MODEL),
                        lambda i: (layer, row0 + i // tiles_per_row, k, 0, 0))


def _rope(x, cos, sin_signed):
    w = x.shape[-1]
    lane = lax.broadcasted_iota(jnp.int32, x.shape, 1)
    partner = jnp.where((lane % 32) < 16, pltpu.roll(x, w - 16, 1), pltpu.roll(x, 16, 1))
    return x * cos + partner * sin_signed


def _expand_q(q):
    lane = lax.broadcasted_iota(jnp.int32, (q.shape[0], 128), 1)
    low = lane < HEAD_DIM
    blocks = []
    for j in range(2):
        pair = q[:, 128 * j:128 * (j + 1)]
        blocks.append(jnp.where(low, pair, 0.0))
        blocks.append(jnp.where(low, pltpu.roll(pair, HEAD_DIM, 1), 0.0))
    return jnp.concatenate(blocks, axis=1).astype(BF16)


def _expand_kv(k, fill):
    lane = lax.broadcasted_iota(jnp.int32, k.shape, 1)
    low = lane < HEAD_DIM
    return jnp.concatenate([jnp.where(low, k, fill),
                            jnp.where(low, pltpu.roll(k, HEAD_DIM, 1), fill)], axis=1).astype(BF16)


def _inproj_kernel(*refs, rope, n_alias):
    x_ref, sh_ref, sc_ref, w_ref, wt_ref, qn, kn, seg = refs[:8]
    i = 8
    if rope:
        cos_ref, sin_ref = refs[8:10]
        i = 10
    i += n_alias
    f_o, r_o, qg_o, kg_o, vg_o, qw_o, kw_o, vw_o = refs[i:i + 8]
    h = (x_ref[...] * (1.0 + sc_ref[...]) + sh_ref[...]).astype(BF16)

    f_o[:, F_XBC:F_XBC + 768] = _dot(h, w_ref[:, M_XBC:M_XBC + 768])
    f_o[:, F_RG:F_RG + 256] = _dot(h, w_ref[:, M_RG:M_RG + 256])
    f_o[:, F_SZ:F_SZ + 256] = _dot(h, w_ref[:, M_SZ:M_SZ + 256])
    f_o[:, F_DT:F_DT + 128] = _dot(h, wt_ref[:, T_DT:T_DT + 128])
    ret = _dot(h, w_ref[:, M_RET:M_RET + 768])
    r_o[...] = jnp.concatenate([ret[:, 0:256], ret[:, 256:512] * HEAD_DIM ** -0.5, ret[:, 512:768]],
                               axis=1).astype(BF16)

    seg_m = seg[...]

    def rms(v, g, s):
        ms = _seg_sum(v * v, s) * (1.0 / HEAD_DIM)
        return v * lax.rsqrt(ms + 1e-6) * g

    gqa = _dot(h, w_ref[:, M_GQA:M_GQA + 512])
    swa = _dot(h, wt_ref[:, T_SWA:T_SWA + 512])
    q_g = rms(gqa[:, 0:256], qn[...], seg_m)
    k_g = rms(gqa[:, 256:384], kn[...][:, :128], seg_m[:128, :128])
    q_w = swa[:, 0:256]
    k_w = swa[:, 256:384]
    if rope:
        cos = cos_ref[...]
        sin = sin_ref[...]
        q_g = _rope(q_g, cos, sin)
        q_w = _rope(q_w, cos, sin)
        k_g = _rope(k_g, cos[:, :128], sin[:, :128])
        k_w = _rope(k_w, cos[:, :128], sin[:, :128])
    else:
        for o_ref, piece in zip(refs[i + 8:i + 12], (k_g, gqa[:, 384:512], swa[:, 256:384], swa[:, 384:512])):
            o_ref[...] = piece.reshape(o_ref.shape)
    qg_o[...] = _expand_q(q_g * QK_SCALE)
    qw_o[...] = _expand_q(q_w * QK_SCALE)
    kg_o[...] = _expand_kv(k_g, 0.0)
    kw_o[...] = _expand_kv(k_w, 0.0)
    vg_o[...] = _expand_kv(gqa[:, 384:512], 1.0)
    vw_o[...] = _expand_kv(swa[:, 384:512], 1.0)


def _inproj_call(x, mod5, w_main, w_tail, layer, row0, tiles_per_row, qn256, kn256, seg, rope_tabs, seq_len,
                 new_kv_prev=None):
    m = x.shape[0]
    tm = INPROJ_TILE
    tiles_per_row = tiles_per_row * TOKEN_TILE // tm
    rope = rope_tabs is not None
    const = lambda shape: pl.BlockSpec(shape, lambda i: (0, 0))
    in_specs = [pl.BlockSpec((tm, D_MODEL), lambda i: (i, 0)),
                _mod_spec(layer, row0, tiles_per_row, 0),
                _mod_spec(layer, row0, tiles_per_row, 1),
                pl.BlockSpec((None, D_MODEL, MAIN_W), lambda i: (layer, 0, 0)),
                pl.BlockSpec((None, D_MODEL, TAIL_W), lambda i: (layer, 0, 0)),
                const((1, 256)), const((1, 256)), const((256, 256))]
    args = [x, mod5, mod5, w_main, w_tail, qn256, kn256, seg]
    if rope:
        tpb = seq_len // tm
        tab = pl.BlockSpec((tm, 256), lambda i: (i % tpb, 0))
        in_specs += [tab, tab]
        args += list(rope_tabs)
    widths = [(F_WIDTH, F32), (768, BF16), (512, BF16), (256, BF16), (256, BF16),
              (512, BF16), (256, BF16), (256, BF16)]
    out_shape = [jax.ShapeDtypeStruct((m, w), dt) for w, dt in widths]
    out_specs = [pl.BlockSpec((tm, w), lambda i: (i, 0)) for w, _ in widths]
    aliases = {}
    if not rope:
        bpt = tm // seq_len
        out_shape += [jax.ShapeDtypeStruct((m // seq_len, DEPTH, seq_len, 128), F32)] * 4
        out_specs += [pl.BlockSpec((bpt, None, seq_len, 128), lambda i: (i, layer, 0, 0))] * 4
        if new_kv_prev is not None:
            aliases = {len(args) + t: len(widths) + t for t in range(4)}
            in_specs += [pl.BlockSpec(memory_space=pl.ANY)] * 4
            args += list(new_kv_prev)
    return pl.pallas_call(
        functools.partial(_inproj_kernel, rope=rope, n_alias=len(aliases)),
        out_shape=out_shape, grid=(m // tm,), in_specs=in_specs, out_specs=out_specs,
        input_output_aliases=aliases,
        compiler_params=_params(("arbitrary",)), name="in_proj",
    )(*args)


def _online(m, acc, s_blocks, v_blocks):
    m_new = m
    for s in s_blocks:
        m_new = jnp.maximum(m_new, jnp.max(s, axis=-1, keepdims=True))
    acc = jnp.exp2(m - m_new) * acc
    for s, v in zip(s_blocks, v_blocks):
        acc = acc + _dot(jnp.exp2(s - m_new).astype(BF16), v)
    return m_new, acc


def _attend(q_ref, k_ref, v_ref, kc_ref, vc_ref, sink_ref, o_ref, *, qi, seq_len, tq, kv_chunk, window):
    rows = 2 * tq
    q2 = [jnp.concatenate([q_ref[:, 256 * j:256 * j + 128], q_ref[:, 256 * j + 128:256 * (j + 1)]], axis=0)
          for j in range(2)]

    def update(carry, kv_blocks, masks):
        out = []
        for j in range(2):
            lanes = slice(128 * j, 128 * (j + 1))
            s_blocks = []
            for (k_blk, _), mask in zip(kv_blocks, masks):
                s = _dot_nt(q2[j], k_blk[:, lanes])
                s_blocks.append(s if mask is None else jnp.where(mask, s, -jnp.inf))
            out.append(_online(*carry[j], s_blocks, [v_blk[:, lanes] for _, v_blk in kv_blocks]))
        return tuple(out)

    if sink_ref is not None:
        upper = lax.broadcasted_iota(jnp.int32, (rows, 128), 1) >= HEAD_DIM
        carry = tuple((jnp.max(sink_ref[j], axis=-1, keepdims=True) * LOG2E, jnp.where(upper, 1.0, 0.0))
                      for j in range(2))
    else:
        carry = tuple((jnp.full((rows, 1), -jnp.inf, F32), jnp.zeros((rows, 128), F32)) for _ in range(2))
    if window:
        span = tq + 2 * WINDOW
        start = pl.multiple_of(jnp.clip(qi * tq - WINDOW, 0, seq_len - span), 128)
        qpos = qi * tq + lax.broadcasted_iota(jnp.int32, (rows, span), 0) % tq
        kpos = start + lax.broadcasted_iota(jnp.int32, (rows, span), 1)
        blocks = [(k_ref[pl.ds(start, span), :], v_ref[pl.ds(start, span), :])]
        masks = [jnp.abs(qpos - kpos) <= WINDOW]
        if kc_ref is not None:
            blocks.append((kc_ref[...], vc_ref[...]))
            masks.append(None)
        carry = update(carry, blocks, masks)
    else:
        for c in range(seq_len // kv_chunk):
            blk = slice(c * kv_chunk, (c + 1) * kv_chunk)
            carry = update(carry, [(k_ref[blk, :], v_ref[blk, :])], [None])
        if kc_ref is not None:
            carry = update(carry, [(kc_ref[...], vc_ref[...])], [None])

    low = lax.broadcasted_iota(jnp.int32, (tq, 128), 1) < HEAD_DIM
    for j in range(2):
        acc = carry[j][1]
        o = acc / pltpu.roll(acc, HEAD_DIM, 1)
        o_ref[:, 128 * j:128 * (j + 1)] = jnp.where(low, o[:tq], pltpu.roll(o[tq:], HEAD_DIM, 1)).astype(o_ref.dtype)


def _attn_kernel(*refs, seq_len, tq, kv_chunk, mixers):
    n_in = sum(3 + 2 * c + s for c, s, _ in mixers)
    outs = refs[n_in:]
    i = 0
    for o_ref, (has_cache, has_sink, window) in zip(outs, mixers):
        q_ref, k_ref, v_ref = refs[i:i + 3]
        i += 3
        kc_ref, vc_ref = refs[i:i + 2] if has_cache else (None, None)
        i += 2 * has_cache
        sink_ref = refs[i] if has_sink else None
        i += has_sink
        _attend(q_ref, k_ref, v_ref, kc_ref, vc_ref, sink_ref, o_ref, qi=pl.program_id(1),
                seq_len=seq_len, tq=tq, kv_chunk=kv_chunk, window=window)


def _attn_call(mixers, nb, seq_len):
    tq = min(ATTN_TILE, seq_len)
    nq = seq_len // tq
    kv_chunk = min(1024, seq_len)
    in_specs, args, static = [], [], []
    for mx in mixers:
        in_specs += [pl.BlockSpec((tq, 512), lambda b, i: (b * nq + i, 0)),
                     pl.BlockSpec((seq_len, 256), lambda b, i: (b, 0)),
                     pl.BlockSpec((seq_len, 256), lambda b, i: (b, 0))]
        args += [mx["q"], mx["k"], mx["v"]]
        if mx.get("cache") is not None:
            in_specs += [pl.BlockSpec((None, PAST_LEN, 256), lambda b, i: (b, 0, 0))] * 2
            args += list(mx["cache"])
        if mx.get("sink") is not None:
            in_specs.append(pl.BlockSpec((2, 2 * tq, 128), lambda b, i: (0, 0, 0)))
            args.append(mx["sink"])
        static.append((mx.get("cache") is not None, mx.get("sink") is not None, bool(mx.get("window"))))
    out = jax.ShapeDtypeStruct((nb * seq_len, 256), BF16)
    out_spec = pl.BlockSpec((tq, 256), lambda b, i: (b * nq + i, 0))
    return pl.pallas_call(
        functools.partial(_attn_kernel, seq_len=seq_len, tq=tq, kv_chunk=kv_chunk, mixers=tuple(static)),
        out_shape=[out] * len(mixers), grid=(nb, nq), in_specs=in_specs, out_specs=[out_spec] * len(mixers),
        compiler_params=_params(("arbitrary", "arbitrary")), name="attention",
    )(*args)


def _chunk_of(s, nc):
    return jnp.where(s < nc, nc - 1 - s, s - nc)


STEP_CHUNKS = 2


def _when(always, pred):
    if always:
        return lambda fn: fn()
    return pl.when(pred)


def _run_sweeps(single, s, ns, backward, forward):
    if single:
        backward()
        forward()
    else:
        pl.when(s < ns)(backward)
        pl.when(s >= ns)(forward)


def _chunks_per_step(nc):
    return STEP_CHUNKS if nc % STEP_CHUNKS == 0 else 1


def _log_gamma(rd):
    return jnp.log1p(-jnp.exp(rd))


def _ret_kernel(*refs, nc, sub, has_init, emit_state):
    q_ref, k_ref, v_ref, g_ref, rdl_ref, rdh_ref, gn_ref, seg_ref = refs[:8]
    i = 8
    if has_init:
        s0_ref = refs[i]
        i += 1
    y_ref = refs[i]
    i += 1
    if emit_state:
        st_ref = refs[i]
        i += 1
    m_sc, dec_sc, qk_sc, sf_sc, sb_sc, sball_sc = refs[i:i + 6]
    c_len = CHUNK
    b = pl.program_id(0)
    s = pl.program_id(1)

    @pl.when(jnp.logical_and(b == 0, s == 0))
    def _():
        ri = lax.broadcasted_iota(jnp.int32, (c_len, c_len), 0)
        ci = lax.broadcasted_iota(jnp.int32, (c_len, c_len), 1)
        d = (ri - ci).astype(F32)
        for h in range(N_HEADS):
            lgf = _log_gamma(jnp.broadcast_to(rdh_ref[0, h], (c_len, c_len)))
            lgb = _log_gamma(jnp.broadcast_to(rdh_ref[1, h], (c_len, c_len)))
            m_sc[h] = jnp.where(d > 0, jnp.exp(d * lgf), jnp.where(d < 0, jnp.exp(-d * lgb), 2.0))
        for dr in range(2):
            rows = [jnp.exp(c_len * _log_gamma(jnp.broadcast_to(rdh_ref[dr, h], (HEAD_DIM, c_len))))
                    for h in range(N_HEADS)]
            dec_sc[dr] = jnp.concatenate(rows, axis=0)
        a = lax.broadcasted_iota(jnp.int32, (c_len, 256), 0).astype(F32)
        lgf = _log_gamma(rdl_ref[0])
        lgb = _log_gamma(rdl_ref[1])
        qk_sc[0] = jnp.exp((a + 1.0) * lgf)
        qk_sc[1] = jnp.exp((c_len - 1.0 - a) * lgf)
        qk_sc[2] = jnp.exp((c_len - a) * lgb)
        qk_sc[3] = jnp.exp(a * lgb)

    @pl.when(s == 0)
    def _():
        sf_sc[...] = jnp.zeros_like(sf_sc)
        sb_sc[...] = jnp.zeros_like(sb_sc)
        if has_init:
            for h in range(N_HEADS):
                blk = slice(HEAD_DIM * h, HEAD_DIM * (h + 1))
                sf_sc[blk, blk] = s0_ref[0, h]
                sb_sc[blk, blk] = s0_ref[1, h]

    ri = lax.broadcasted_iota(jnp.int32, (256, 256), 0) // HEAD_DIM
    ci = lax.broadcasted_iota(jnp.int32, (256, 256), 1) // HEAD_DIM
    diag = ri == ci
    ns = nc // sub
    single = ns == 1
    sf = 0 if single else s - ns

    def backward():
        state = sb_sc[...]
        for u in reversed(range(sub)):
            rows = slice(u * c_len, (u + 1) * c_len)
            sball_sc[(ns - 1 - s) * sub + u] = state
            kd = (k_ref[rows, :].astype(F32) * qk_sc[3]).astype(BF16)
            state = state * dec_sc[1] + jnp.where(diag, _dot_tn(kd, v_ref[rows, :]), 0.0)
        sb_sc[...] = state
        if emit_state:
            @_when(single, s == ns - 1)
            def _():
                for h in range(N_HEADS):
                    blk = slice(HEAD_DIM * h, HEAD_DIM * (h + 1))
                    st_ref[1, h] = sb_sc[blk, blk]

    def forward():
        state = sf_sc[...]
        lane_head = lax.broadcasted_iota(jnp.int32, (c_len, 256), 1) // HEAD_DIM
        seg = seg_ref[...]
        for u in range(sub):
            rows = slice(u * c_len, (u + 1) * c_len)
            qb = q_ref[rows, :]
            kb = k_ref[rows, :]
            vb = v_ref[rows, :]
            q = qb.astype(F32)
            o = (_dot(qb, state.astype(BF16)) * qk_sc[0]
                 + _dot(qb, sball_sc[sf * sub + u].astype(BF16)) * qk_sc[2])
            for h in range(N_HEADS):
                mine = lane_head == h
                sc = _dot_nt(jnp.where(mine, q, 0.0).astype(BF16), kb)
                pv = _dot((sc * m_sc[h]).astype(BF16), vb)
                o = o + jnp.where(mine, pv, 0.0)
            mu = _seg_sum(o, seg) * (1.0 / HEAD_DIM)
            dlt = o - mu
            var = _seg_sum(dlt * dlt, seg) * (1.0 / HEAD_DIM)
            on = dlt * lax.rsqrt(var + 1e-5) * gn_ref[...]
            y_ref[rows, :] = (on * _silu(g_ref[rows, :])).astype(y_ref.dtype)
            kd = (kb.astype(F32) * qk_sc[1]).astype(BF16)
            state = state * dec_sc[0] + jnp.where(diag, _dot_tn(kd, vb), 0.0)
        sf_sc[...] = state
        if emit_state:
            @_when(single, sf == ns - 1)
            def _():
                for h in range(N_HEADS):
                    blk = slice(HEAD_DIM * h, HEAD_DIM * (h + 1))
                    st_ref[0, h] = sf_sc[blk, blk]

    return backward, forward


def _ret_specs(rqkv, pf, nb, seq_len, rd_lane, rd_head, gn_g, seg, layer, state0=None, emit_state=False):
    c_len = CHUNK
    nc = seq_len // c_len
    sub = _chunks_per_step(nc)
    ns = nc // sub
    rows = sub * c_len

    def col(start):
        return pl.BlockSpec((rows, 256), lambda b, s: (b * ns + _chunk_of(s, ns), start // 256))

    in_specs = [col(0), col(256), col(512), col(F_RG),
                pl.BlockSpec((2, 1, 256), lambda b, s: (0, 0, 0)),
                pl.BlockSpec((2, N_HEADS, 1, c_len), lambda b, s: (0, 0, 0, 0)),
                pl.BlockSpec((1, 256), lambda b, s: (0, 0)),
                pl.BlockSpec((256, 256), lambda b, s: (0, 0))]
    args = [rqkv, rqkv, rqkv, pf, rd_lane, rd_head, gn_g, seg]
    if state0 is not None:
        in_specs.append(pl.BlockSpec((None, None, 2, N_HEADS, HEAD_DIM, HEAD_DIM),
                                     lambda b, s: (b, layer, 0, 0, 0, 0)))
        args.append(state0)
    outs = [jax.ShapeDtypeStruct((nb * seq_len, 256), BF16)]
    out_specs = [pl.BlockSpec((rows, 256), lambda b, s: (b * ns + jnp.maximum(s - ns, 0), 0))]
    if emit_state:
        outs.append(jax.ShapeDtypeStruct((nb, DEPTH, 2, N_HEADS, HEAD_DIM, HEAD_DIM), F32))
        out_specs.append(pl.BlockSpec((None, None, 2, N_HEADS, HEAD_DIM, HEAD_DIM),
                                      lambda b, s: (b, layer, 0, 0, 0, 0)))
    scratch = [pltpu.VMEM((N_HEADS, c_len, c_len), F32), pltpu.VMEM((2, 256, 256), F32),
               pltpu.VMEM((4, c_len, 256), F32), pltpu.VMEM((256, 256), F32),
               pltpu.VMEM((256, 256), F32), pltpu.VMEM((nc, 256, 256), F32)]
    return in_specs, args, outs, out_specs, scratch


def _cumsum_rows(x):
    n = x.shape[0]
    row = lax.broadcasted_iota(jnp.int32, x.shape, 0)
    sh = 1
    while sh < n:
        x = x + jnp.where(row >= sh, pltpu.roll(x, sh, 0), 0.0)
        sh *= 2
    return x


def _ssd_kernel(*refs, nc, sub, has_init, emit_state):
    xbc_ref, prev_ref, next_ref, z_ref, dt_ref, cw_ref, cb_ref, dtb_ref, al_ref, dl_ref, ng_ref = refs[:11]
    i = 11
    if has_init:
        h0_ref = refs[i]
        i += 1
    y_ref = refs[i]
    i += 1
    if emit_state:
        st_ref = refs[i]
        i += 1
    hf_sc, hb_sc, hball_sc, xs_sc, bm_sc, cm_sc, dt_sc, cum_sc, rev_sc = refs[i:i + 9]
    c_len = CHUNK
    ns = nc // sub
    s = pl.program_id(1)
    single = ns == 1
    sf = 0 if single else s - ns

    @pl.when(s == 0)
    def _():
        if has_init:
            hf_sc[...] = h0_ref[0]
            hb_sc[...] = h0_ref[1]
        else:
            hf_sc[...] = jnp.zeros_like(hf_sc)
            hb_sc[...] = jnp.zeros_like(hb_sc)

    low = lax.broadcasted_iota(jnp.int32, (c_len, 128), 1) < HEAD_DIM
    low_rows = lax.broadcasted_iota(jnp.int32, (128, 128), 0) < HEAD_DIM

    def col(a, lane):
        return jnp.broadcast_to(a[:, lane:lane + 1], (c_len, c_len))

    def pair(a0, a1):
        return jnp.where(low, a0[:, :128], a1[:, :128])

    def state_update(h, g, xs, bm, dt, cols, tot_row, dt_lanes):
        w = [jnp.exp2(tot_row[r][:, :128] - cols[r][:, :128]) * col(dt, dt_lanes[r])[:, :128] for r in range(2)]
        xw = (xs[:, 128 * g:128 * (g + 1)] * jnp.where(low, w[0], w[1])).astype(BF16)
        dec = jnp.where(low_rows, jnp.broadcast_to(jnp.exp2(tot_row[0][:, :128]), (128, 128)),
                        jnp.broadcast_to(jnp.exp2(tot_row[1][:, :128]), (128, 128)))
        return h * dec + _dot_tn(xw, bm[:, 128 * g:128 * (g + 1)])

    def backward():
        cg = ns - 1 - s
        n_rows = sub * c_len
        xbc = xbc_ref[...]
        before = jnp.where(cg > 0, prev_ref[7:8, :], 0.0)
        after = jnp.where(cg < ns - 1, next_ref[0:1, :], 0.0)
        row = lax.broadcasted_iota(jnp.int32, xbc.shape, 0)
        x_prev = jnp.where(row == 0, before, pltpu.roll(xbc, 1, 0))
        x_next = jnp.where(row == n_rows - 1, after, pltpu.roll(xbc, n_rows - 1, 0))
        conv = _silu(cw_ref[0:1, :] * x_prev + cw_ref[1:2, :] * xbc + cw_ref[2:3, :] * x_next + cb_ref[...])
        pre = dt_ref[...] + dtb_ref[...]
        dt_all = jnp.maximum(pre, 0.0) + jnp.log1p(jnp.exp(-jnp.abs(pre)))
        dta_all = dt_all * (-jnp.exp(al_ref[...]) * LOG2E)
        state = [hb_sc[g] for g in range(2)]
        for u in reversed(range(sub)):
            rows = slice(u * c_len, (u + 1) * c_len)
            c = cg * sub + u
            xs = conv[rows, 0:256]
            bm = conv[rows, 256:512].astype(BF16)
            dt = dt_all[rows, :]
            dta = dta_all[rows, :]
            cum = _cumsum_rows(dta)
            rev = cum[c_len - 1:c_len, :] - cum + dta
            xs_sc[c] = xs
            bm_sc[c] = bm
            cm_sc[c] = conv[rows, 512:768].astype(BF16)
            dt_sc[c] = dt
            cum_sc[c] = cum
            rev_sc[c] = rev
            for g in range(2):
                hball_sc[c, g] = state[g]
                cols = [col(rev, 4 + 2 * g + r) for r in range(2)]
                tot = [cl[0:1, :] for cl in cols]
                state[g] = state_update(state[g], g, xs, bm, dt, cols, tot, [4 + 2 * g, 5 + 2 * g])
        for g in range(2):
            hb_sc[g] = state[g]
        if emit_state:
            @_when(single, s == ns - 1)
            def _():
                st_ref[1] = hb_sc[...]

    def forward():
        ri = lax.broadcasted_iota(jnp.int32, (c_len, c_len), 0)
        ci = lax.broadcasted_iota(jnp.int32, (c_len, c_len), 1)
        below = ri > ci
        on_diag = ri == ci
        lane_head = lax.broadcasted_iota(jnp.int32, (c_len, 256), 1) // HEAD_DIM
        state = [hf_sc[g] for g in range(2)]
        for u in range(sub):
            rows = slice(u * c_len, (u + 1) * c_len)
            c = sf * sub + u
            xs = xs_sc[c]
            bm = bm_sc[c]
            cm = cm_sc[c]
            dt = dt_sc[c]
            cum = cum_sc[c]
            rev = rev_sc[c]
            log_dt = jnp.log2(dt)
            off_f = (log_dt - cum).T
            off_b = (log_dt - rev).T
            dt_t = dt.T
            xs_b = xs.astype(BF16)
            y = jnp.zeros((c_len, 256), F32)
            inter = []
            for g in range(2):
                gl = slice(128 * g, 128 * (g + 1))
                gmat = _dot_nt(cm[:, gl], bm[:, gl])
                cols_f, cols_b = [], []
                for r in range(2):
                    h = 2 * g + r
                    cf = col(cum, h)
                    cb = col(rev, 4 + h)
                    cols_f.append(cf)
                    cols_b.append(cb)
                    w = jnp.exp2(jnp.where(below, cf + off_f[h:h + 1, :], cb + off_b[4 + h:5 + h, :]))
                    w = w + jnp.where(on_diag, dt_t[h:h + 1, :], 0.0)
                    yh = _dot((gmat * w).astype(BF16), xs_b)
                    y = y + jnp.where(lane_head == h, yh, 0.0)
                yf = _dot_nt(cm[:, gl], state[g].astype(BF16))
                yb = _dot_nt(cm[:, gl], hball_sc[c, g].astype(BF16))
                inter.append(yf * pair(jnp.exp2(cols_f[0]), jnp.exp2(cols_f[1]))
                             + yb * pair(jnp.exp2(cols_b[0]), jnp.exp2(cols_b[1])))
                tot = [cl[c_len - 1:c_len, :] for cl in cols_f]
                state[g] = state_update(state[g], g, xs, bm, dt, cols_f, tot, [2 * g, 2 * g + 1])
            y = y + jnp.concatenate(inter, axis=1) + xs * dl_ref[...]
            y = y * _silu(z_ref[rows, :])
            ms = jnp.mean(y * y, axis=-1, keepdims=True)
            y_ref[rows, :] = (y * lax.rsqrt(ms + 1e-6) * ng_ref[...]).astype(y_ref.dtype)
        for g in range(2):
            hf_sc[g] = state[g]
        if emit_state:
            @_when(single, sf == ns - 1)
            def _():
                st_ref[0] = hf_sc[...]

    return backward, forward


def _ssd_specs(p, nb, seq_len, conv_w, conv_b, dt_bias, a_log, d_lane, norm_g, layer,
              state0=None, emit_state=False):
    c_len = CHUNK
    nc = seq_len // c_len
    sub = _chunks_per_step(nc)
    ns = nc // sub
    rows = sub * c_len
    m = nb * seq_len
    r8 = rows // 8
    chunk = lambda b, s: b * ns + jnp.maximum(ns - 1 - s, 0)
    in_specs = [pl.BlockSpec((rows, 768), lambda b, s: (chunk(b, s), F_XBC // 768)),
                pl.BlockSpec((8, 768), lambda b, s: (jnp.maximum(chunk(b, s) * r8 - 1, 0), F_XBC // 768)),
                pl.BlockSpec((8, 768), lambda b, s: (jnp.minimum((chunk(b, s) + 1) * r8, m // 8 - 1),
                                                     F_XBC // 768)),
                pl.BlockSpec((rows, 256), lambda b, s: (b * ns + jnp.maximum(s - ns, 0), F_SZ // 256)),
                pl.BlockSpec((rows, 128), lambda b, s: (chunk(b, s), F_DT // 128)),
                pl.BlockSpec((None, 3, 768), lambda b, s: (layer, 0, 0)),
                pl.BlockSpec((1, 768), lambda b, s: (0, 0)),
                pl.BlockSpec((1, 128), lambda b, s: (0, 0)),
                pl.BlockSpec((1, 128), lambda b, s: (0, 0)),
                pl.BlockSpec((1, 256), lambda b, s: (0, 0)),
                pl.BlockSpec((1, 256), lambda b, s: (0, 0))]
    args = [p, p, p, p, p, conv_w, conv_b, dt_bias, a_log, d_lane, norm_g]
    if state0 is not None:
        in_specs.append(pl.BlockSpec((None, None, 2, 2, 128, SSD_STATE),
                                     lambda b, s: (b, layer, 0, 0, 0, 0)))
        args.append(state0)
    outs = [jax.ShapeDtypeStruct((m, 256), BF16)]
    out_specs = [pl.BlockSpec((rows, 256), lambda b, s: (b * ns + jnp.maximum(s - ns, 0), 0))]
    if emit_state:
        outs.append(jax.ShapeDtypeStruct((nb, DEPTH, 2, 2, 128, SSD_STATE), F32))
        out_specs.append(pl.BlockSpec((None, None, 2, 2, 128, SSD_STATE),
                                      lambda b, s: (b, layer, 0, 0, 0, 0)))
    scratch = [pltpu.VMEM((2, 128, SSD_STATE), F32), pltpu.VMEM((2, 128, SSD_STATE), F32),
               pltpu.VMEM((nc, 2, 128, SSD_STATE), F32),
               pltpu.VMEM((nc, c_len, 256), F32), pltpu.VMEM((nc, c_len, 256), BF16),
               pltpu.VMEM((nc, c_len, 256), BF16), pltpu.VMEM((nc, c_len, 128), F32),
               pltpu.VMEM((nc, c_len, 128), F32), pltpu.VMEM((nc, c_len, 128), F32)]
    return in_specs, args, outs, out_specs, scratch


def _scan_kernel(*refs, counts, nc, sub, **static):
    (ri, si), (ro, so), (rs, ss), n_alias = counts
    ins, rest = refs[:ri + si], refs[ri + si + n_alias:]
    outs, scr = rest[:ro + so], rest[ro + so:]
    ret_bwd, ret_fwd = _ret_kernel(*ins[:ri], *outs[:ro], *scr[:rs], nc=nc, sub=sub, **static)
    ssd_bwd, ssd_fwd = _ssd_kernel(*ins[ri:], *outs[ro:], *scr[rs:], nc=nc, sub=sub, **static)
    ns = nc // sub
    if ns == 1:
        ret_bwd()
        ret_fwd()
        ssd_bwd()
        ssd_fwd()
        return

    def backward():
        ret_bwd()
        ssd_bwd()

    def forward():
        ret_fwd()
        ssd_fwd()

    _run_sweeps(False, pl.program_id(1), ns, backward, forward)


def _scan_call(ret_parts, ssd_parts, nb, seq_len, has_init, emit_state, states_prev=None):
    nc = seq_len // CHUNK
    sub = _chunks_per_step(nc)
    in_specs = ret_parts[0] + ssd_parts[0]
    args = ret_parts[1] + ssd_parts[1]
    aliases = {}
    if states_prev is not None:
        aliases = {len(args): 1, len(args) + 1: len(ret_parts[2]) + 1}
        in_specs = in_specs + [pl.BlockSpec(memory_space=pl.ANY)] * 2
        args = args + list(states_prev)
    counts = tuple((len(a), len(b)) for a, b in zip((ret_parts[0], ret_parts[2], ret_parts[4]),
                                                    (ssd_parts[0], ssd_parts[2], ssd_parts[4])))
    return pl.pallas_call(
        functools.partial(_scan_kernel, counts=counts + (len(aliases),), nc=nc, sub=sub, has_init=has_init,
                          emit_state=emit_state),
        out_shape=ret_parts[2] + ssd_parts[2], grid=(nb, 1 if nc == sub else 2 * nc // sub),
        in_specs=in_specs, out_specs=ret_parts[3] + ssd_parts[3],
        scratch_shapes=ret_parts[4] + ssd_parts[4], input_output_aliases=aliases,
        compiler_params=_params(("arbitrary", "arbitrary")), name="scans",
    )(*args)


def _layer_norm(x, g, b):
    mu = jnp.mean(x, axis=-1, keepdims=True)
    d = x - mu
    var = jnp.mean(d * d, axis=-1, keepdims=True)
    return d * lax.rsqrt(var + 1e-5) * g + b


FFN_CHUNK = 256
FFN_ROWS = 512


def _outffn_kernel(x_ref, yr_ref, yg_ref, ys_ref, yw_ref, g1_ref, sh2_ref, sc2_ref, g2_ref,
                   wo_ref, wi_ref, wf_ref, l1g_ref, l1b_ref, l2g_ref, l2b_ref, o_ref):
    mix = (_dot(yr_ref[...], wo_ref[0:256, :]) + _dot(yg_ref[...], wo_ref[256:512, :])
           + _dot(ys_ref[...], wo_ref[512:768, :]) + _dot(yw_ref[...], wo_ref[768:1024, :]))
    x1 = _layer_norm(ALPHA * x_ref[...] + g1_ref[...] * mix, l1g_ref[...], l1b_ref[...])
    h2 = (x1 * (1.0 + sc2_ref[...]) + sh2_ref[...]).astype(BF16)
    acc = None
    for j in range(D_FF // FFN_CHUNK):
        cols = slice(FFN_CHUNK * j, FFN_CHUNK * (j + 1))
        gate = _dot(h2, wi_ref[:, cols])
        up = _dot(h2, wi_ref[:, D_FF + FFN_CHUNK * j:D_FF + FFN_CHUNK * (j + 1)])
        d = _dot((_silu(gate) * up).astype(BF16), wf_ref[cols, :])
        acc = d if acc is None else acc + d
    o_ref[...] = _layer_norm(ALPHA * x1 + g2_ref[...] * acc, l2g_ref[...], l2b_ref[...])


def _outffn_call(x, ys, mod5, w_out_b, w_ffn_in_b, w_ffn_out_b, ln, layer, row0, tiles_per_row):
    m = x.shape[0]
    tm = min(FFN_ROWS, m)
    tiles_per_row = max(tiles_per_row * TOKEN_TILE // tm, 1)
    row = lambda w: pl.BlockSpec((tm, w), lambda i: (i, 0))
    once = pl.Buffered(1)
    vec = pl.BlockSpec((None, 1, D_MODEL), lambda i: (layer, 0, 0))
    in_specs = [row(D_MODEL), row(256), row(256), row(256), row(256),
                _mod_spec(layer, row0, tiles_per_row, 2), _mod_spec(layer, row0, tiles_per_row, 3),
                _mod_spec(layer, row0, tiles_per_row, 4), _mod_spec(layer, row0, tiles_per_row, 5),
                pl.BlockSpec((None, D_MODEL, D_MODEL), lambda i: (layer, 0, 0), pipeline_mode=once),
                pl.BlockSpec((None, D_MODEL, 2 * D_FF), lambda i: (layer, 0, 0), pipeline_mode=once),
                pl.BlockSpec((None, D_FF, D_MODEL), lambda i: (layer, 0, 0), pipeline_mode=once),
                vec, vec, vec, vec]
    return pl.pallas_call(
        _outffn_kernel,
        out_shape=jax.ShapeDtypeStruct((m, D_MODEL), F32),
        grid=(m // tm,), in_specs=in_specs, out_specs=row(D_MODEL),
        compiler_params=_params(("arbitrary",)), name="out_ffn",
    )(x, *ys, mod5, mod5, mod5, mod5, w_out_b, w_ffn_in_b, w_ffn_out_b, *ln)


def _rope_tables(seq_len):
    t = jnp.arange(seq_len)
    rowp = (t // GRID_W).astype(F32)
    colp = (t % GRID_W).astype(F32)
    quarter = HEAD_DIM // 4
    inv_freq = ROPE_THETA ** (-jnp.arange(quarter, dtype=F32) / quarter)
    ang_r = rowp[:, None] * inv_freq[None, :]
    ang_c = colp[:, None] * inv_freq[None, :]
    cos = jnp.concatenate([jnp.cos(ang_r)] * 2 + [jnp.cos(ang_c)] * 2, axis=-1)
    sin = jnp.concatenate([-jnp.sin(ang_r), jnp.sin(ang_r), -jnp.sin(ang_c), jnp.sin(ang_c)], axis=-1)
    return jnp.tile(cos, (1, N_HEADS)), jnp.tile(sin, (1, N_HEADS))


def _expand_cache(c, fill):
    pad = jnp.full(c.shape[:2] + (HEAD_DIM,), fill, c.dtype)
    return jnp.concatenate([c[:, :, 0], pad, c[:, :, 1], pad], axis=-1).astype(BF16)


def _split_w_in(w):
    pad = jnp.zeros(w.shape[:-1] + (TAIL_W - T_DT - 8,), w.dtype)
    tail = jnp.concatenate([w[..., MAIN_W + 8:], w[..., MAIN_W:MAIN_W + 8], pad], axis=-1)
    return w.astype(BF16), tail.astype(BF16)


def kernel(x_prompt, x_sample, cache_gqa_k, cache_gqa_v, cache_swa_k, cache_swa_v, state_ret, state_ssd,
           c, c_ctx, ada_w, ada_b, w_in, ret_decay, ret_gn_g, gqa_q_norm, gqa_k_norm,
           ssd_conv_w, ssd_conv_b, ssd_dt_bias, ssd_a_log, ssd_d, ssd_norm_g, swa_sink, w_out,
           ln1_g, ln1_b, w_ffn_in, w_ffn_out, ln2_g, ln2_b):
    nb_c, len_c, _ = x_prompt.shape
    nb_l, len_l, _ = x_sample.shape
    depth = ada_w.shape[0]

    cond8 = jnp.zeros((8, D_MODEL), F32).at[0].set(c_ctx).at[1:1 + nb_l].set(c)
    mod5 = _ada_call(cond8, ada_w, ada_b)

    w_main, w_tail = _split_w_in(w_in)
    w_out_b = w_out.astype(BF16)
    w_ffn_in_b = w_ffn_in.astype(BF16)
    w_ffn_out_b = w_ffn_out.astype(BF16)
    lane_head = jnp.arange(256) // HEAD_DIM
    seg = (lane_head[:, None] == lane_head[None, :]).astype(BF16)
    rope_tabs = _rope_tables(len_l)
    caches_g = (_expand_cache_layers(cache_gqa_k, 0.0), _expand_cache_layers(cache_gqa_v, 1.0))
    caches_w = (_expand_cache_layers(cache_swa_k, 0.0), _expand_cache_layers(cache_swa_v, 1.0))
    state_ssd_g = state_ssd.reshape(nb_l, depth, 2, 2, 128, SSD_STATE)

    def layer_consts(l):
        rd = ret_decay[l]
        return dict(
            rd_lane=jnp.repeat(rd, HEAD_DIM, axis=-1).reshape(2, 1, 256),
            rd_head=jnp.broadcast_to(rd[:, :, None, None], (2, N_HEADS, 1, CHUNK)),
            gn_g=ret_gn_g[l].reshape(1, 256),
            qn=jnp.tile(gqa_q_norm[l], N_HEADS).reshape(1, 256),
            kn=jnp.tile(gqa_k_norm[l], N_HEADS).reshape(1, 256),
            conv_b=ssd_conv_b[l].reshape(1, 768),
            dt_bias=jnp.zeros((1, 128), F32).at[0, :8].set(ssd_dt_bias[l].reshape(8)),
            a_log=jnp.zeros((1, 128), F32).at[0, :8].set(ssd_a_log[l].reshape(8)),
            d_lane=jnp.repeat(ssd_d[l], HEAD_DIM).reshape(1, 256),
            norm_g=ssd_norm_g[l].reshape(1, 256),
            sink=lambda tq: jnp.broadcast_to(swa_sink[l].reshape(2, 2, 1, 1), (2, 2, tq, 128)
                                             ).reshape(2, 2 * tq, 128),
        )

    ln = tuple(a.reshape(depth, 1, D_MODEL) for a in (ln1_g, ln1_b, ln2_g, ln2_b))

    def run_group(x, nb, seq_len, row0, latent):
        m = nb * seq_len
        tpr = (seq_len // TOKEN_TILE) if latent else (m // TOKEN_TILE)
        x = x.reshape(m, D_MODEL)
        new_kv = states = None
        if not latent:
            new_kv = tuple(jnp.zeros((nb, depth, seq_len, 128), F32) for _ in range(4))
            states = (jnp.zeros((nb, depth, 2, N_HEADS, HEAD_DIM, HEAD_DIM), F32),
                      jnp.zeros((nb, depth, 2, 2, 128, SSD_STATE), F32))
        for l in range(depth):
            k = layer_consts(l)
            proj = _inproj_call(x, mod5, w_main, w_tail, l, row0, tpr, k["qn"], k["kn"], seg,
                                rope_tabs if latent else None, seq_len, new_kv_prev=new_kv)
            pf, rqkv, qg, kg, vg, qw, kw, vw = proj[:8]
            ret_parts = _ret_specs(rqkv, pf, nb, seq_len, k["rd_lane"], k["rd_head"], k["gn_g"], seg, l,
                                   state0=state_ret if latent else None, emit_state=not latent)
            ssd_parts = _ssd_specs(pf, nb, seq_len, ssd_conv_w, k["conv_b"], k["dt_bias"], k["a_log"],
                                   k["d_lane"], k["norm_g"], l,
                                   state0=state_ssd_g if latent else None, emit_state=not latent)
            scans = _scan_call(ret_parts, ssd_parts, nb, seq_len, has_init=latent, emit_state=not latent,
                               states_prev=states)
            if latent:
                y_ret, y_ssd = scans
                y_gqa, y_swa = _attn_call(
                    [dict(q=qg, k=kg, v=vg, cache=(caches_g[0][l], caches_g[1][l])),
                     dict(q=qw, k=kw, v=vw, cache=(caches_w[0][l], caches_w[1][l]),
                          sink=k["sink"](min(ATTN_TILE, seq_len)), window=True)], nb, seq_len)
            else:
                y_ret, st_ret, y_ssd, st_ssd = scans
                states = (st_ret, st_ssd)
                new_kv = proj[8:12]
                y_gqa, = _attn_call([dict(q=qg, k=kg, v=vg)], nb, seq_len)
                y_swa, = _attn_call([dict(q=qw, k=kw, v=vw, sink=k["sink"](min(ATTN_TILE, seq_len)))],
                                    nb, seq_len)
            x = _outffn_call(x, (y_ret, y_gqa, y_ssd, y_swa), mod5, w_out_b, w_ffn_in_b, w_ffn_out_b,
                             ln, l, row0, tpr)
        extras = None
        if not latent:
            extras = tuple(a.reshape(nb, depth, seq_len, 2, HEAD_DIM) for a in new_kv) + (
                states[0], states[1].reshape(nb, depth, 2, N_HEADS, HEAD_DIM, SSD_STATE))
        return x.reshape(nb, seq_len, D_MODEL), extras

    y_prompt, extras = run_group(x_prompt, nb_c, len_c, 0, False)
    y_sample, _ = run_group(x_sample, nb_l, len_l, 1, True)
    return (y_prompt, y_sample) + extras


def _expand_cache_layers(cache, fill):
    return [_expand_cache(cache[:, l], fill) for l in range(cache.shape[1])]
```

```python
import functools
import math

import jax
import jax.numpy as jnp
from jax import lax
from jax.experimental import pallas as pl
from jax.experimental.pallas import tpu as pltpu

F32 = jnp.float32
BF16 = jnp.bfloat16

D_MODEL = 1024
DEPTH = 4
HEAD_DIM = 64
N_HEADS = 4
GRID_W = 64
ROPE_THETA = 10000.0
SSD_STATE = 128
D_FF = 2816
WINDOW = 128
PAST_LEN = 256
ALPHA = (2.0 * DEPTH) ** 0.25
CHUNK = 256
TOKEN_TILE = 256
INPROJ_TILE = 512
ATTN_TILE = 512
KV_CHUNK = 4096
VMEM_LIMIT = 56 * 1024 * 1024
LOG2E = math.log2(math.e)
QK_SCALE = HEAD_DIM ** -0.5 * LOG2E

M_RET, M_RG, M_GQA, M_SZ, M_XBC, MAIN_W = 0, 768, 1024, 1536, 1792, 2560
T_SWA, T_DT, TAIL_W = 0, 512, 640
F_XBC, F_RG, F_SZ, F_DT = 0, 768, 1024, 1280
F_WIDTH = 1408


def _silu(x):
    return x / (1.0 + jnp.exp(-x))


def _dot(a, b):
    return jnp.dot(a, b, preferred_element_type=F32)


def _dot_nt(a, b):
    return lax.dot_general(a, b, (((1,), (1,)), ((), ())), preferred_element_type=F32)


def _dot_tn(a, b):
    return lax.dot_general(a, b, (((0,), (0,)), ((), ())), preferred_element_type=F32)


def _seg_sum(x, seg):
    hi = x.astype(BF16)
    lo = (x - hi.astype(F32)).astype(BF16)
    return _dot(hi, seg) + _dot(lo, seg)


def _params(sem):
    return pltpu.CompilerParams(dimension_semantics=sem, vmem_limit_bytes=VMEM_LIMIT)


def _ada_kernel(c_ref, w_ref, b_ref, o_ref):
    s = _silu(c_ref[...])
    o_ref[...] = _dot(s.astype(BF16), w_ref[...].astype(BF16)) + b_ref[...]


def _ada_call(cond8, ada_w, ada_b):
    tn = D_MODEL
    nt = 6
    out = pl.pallas_call(
        _ada_kernel,
        out_shape=jax.ShapeDtypeStruct((DEPTH, nt, 8, tn), F32),
        grid=(DEPTH, nt),
        in_specs=[pl.BlockSpec((8, D_MODEL), lambda l, j: (0, 0)),
                  pl.BlockSpec((None, D_MODEL, tn), lambda l, j: (l, 0, j)),
                  pl.BlockSpec((None, None, 1, tn), lambda l, j: (l, j, 0, 0))],
        out_specs=pl.BlockSpec((None, None, 8, tn), lambda l, j: (l, j, 0, 0)),
        compiler_params=_params(("arbitrary", "arbitrary")),
        name="ada_mod",
    )(cond8, ada_w, ada_b.reshape(DEPTH, nt, 1, tn))
    return jnp.transpose(out, (0, 2, 1, 3)).reshape(DEPTH, 8, nt, 1, tn)


def _mod_spec(layer, row0, tiles_per_row, k):
    return pl.BlockSpec((None, None, None, 1, D_MODEL),
                        lambda i: (layer, row0 + i // tiles_per_row, k, 0, 0))


def _rope(x, cos, sin_signed):
    w = x.shape[-1]
    lane = lax.broadcasted_iota(jnp.int32, x.shape, 1)
    partner = jnp.where((lane % 32) < 16, pltpu.roll(x, w - 16, 1), pltpu.roll(x, 16, 1))
    return x * cos + partner * sin_signed


def _expand_q(q):
    lane = lax.broadcasted_iota(jnp.int32, (q.shape[0], 128), 1)
    low = lane < HEAD_DIM
    blocks = []
    for j in range(2):
        pair = q[:, 128 * j:128 * (j + 1)]
        blocks.append(jnp.where(low, pair, 0.0))
        blocks.append(jnp.where(low, pltpu.roll(pair, HEAD_DIM, 1), 0.0))
    return jnp.concatenate(blocks, axis=1).astype(BF16)


def _expand_kv(k, fill):
    lane = lax.broadcasted_iota(jnp.int32, k.shape, 1)
    low = lane < HEAD_DIM
    return jnp.concatenate([jnp.where(low, k, fill),
                            jnp.where(low, pltpu.roll(k, HEAD_DIM, 1), fill)], axis=1).astype(BF16)


def _inproj_kernel(*refs, rope, n_alias):
    x_ref, sh_ref, sc_ref, w_ref, wt_ref, qn, kn, seg = refs[:8]
    i = 8
    if rope:
        cos_ref, sin_ref = refs[8:10]
        i = 10
    i += n_alias
    f_o, r_o, qg_o, kg_o, vg_o, qw_o, kw_o, vw_o = refs[i:i + 8]
    h = (x_ref[...] * (1.0 + sc_ref[...]) + sh_ref[...]).astype(BF16)

    f_o[:, F_XBC:F_XBC + 768] = _dot(h, w_ref[:, M_XBC:M_XBC + 768])
    f_o[:, F_RG:F_RG + 256] = _dot(h, w_ref[:, M_RG:M_RG + 256])
    f_o[:, F_SZ:F_SZ + 256] = _dot(h, w_ref[:, M_SZ:M_SZ + 256])
    f_o[:, F_DT:F_DT + 128] = _dot(h, wt_ref[:, T_DT:T_DT + 128])
    ret = _dot(h, w_ref[:, M_RET:M_RET + 768])
    r_o[...] = jnp.concatenate([ret[:, 0:256], ret[:, 256:512] * HEAD_DIM ** -0.5, ret[:, 512:768]],
                               axis=1).astype(BF16)

    seg_m = seg[...]

    def rms(v, g, s):
        ms = _seg_sum(v * v, s) * (1.0 / HEAD_DIM)
        return v * lax.rsqrt(ms + 1e-6) * g

    gqa = _dot(h, w_ref[:, M_GQA:M_GQA + 512])
    swa = _dot(h, wt_ref[:, T_SWA:T_SWA + 512])
    q_g = rms(gqa[:, 0:256], qn[...], seg_m)
    k_g = rms(gqa[:, 256:384], kn[...][:, :128], seg_m[:128, :128])
    q_w = swa[:, 0:256]
    k_w = swa[:, 256:384]
    if rope:
        cos = cos_ref[...]
        sin = sin_ref[...]
        q_g = _rope(q_g, cos, sin)
        q_w = _rope(q_w, cos, sin)
        k_g = _rope(k_g, cos[:, :128], sin[:, :128])
        k_w = _rope(k_w, cos[:, :128], sin[:, :128])
    else:
        for o_ref, piece in zip(refs[i + 8:i + 12], (k_g, gqa[:, 384:512], swa[:, 256:384], swa[:, 384:512])):
            o_ref[...] = piece.reshape(o_ref.shape)
    qg_o[...] = _expand_q(q_g * QK_SCALE)
    qw_o[...] = _expand_q(q_w * QK_SCALE)
    kg_o[...] = _expand_kv(k_g, 0.0)
    kw_o[...] = _expand_kv(k_w, 0.0)
    vg_o[...] = _expand_kv(gqa[:, 384:512], 1.0)
    vw_o[...] = _expand_kv(swa[:, 384:512], 1.0)


def _inproj_call(x, mod5, w_main, w_tail, layer, row0, tiles_per_row, qn256, kn256, seg, rope_tabs, seq_len,
                 new_kv_prev=None):
    m = x.shape[0]
    tm = INPROJ_TILE
    tiles_per_row = tiles_per_row * TOKEN_TILE // tm
    rope = rope_tabs is not None
    const = lambda shape: pl.BlockSpec(shape, lambda i: (0, 0))
    in_specs = [pl.BlockSpec((tm, D_MODEL), lambda i: (i, 0)),
                _mod_spec(layer, row0, tiles_per_row, 0),
                _mod_spec(layer, row0, tiles_per_row, 1),
                pl.BlockSpec((None, D_MODEL, MAIN_W), lambda i: (layer, 0, 0)),
                pl.BlockSpec((None, D_MODEL, TAIL_W), lambda i: (layer, 0, 0)),
                const((1, 256)), const((1, 256)), const((256, 256))]
    args = [x, mod5, mod5, w_main, w_tail, qn256, kn256, seg]
    if rope:
        tpb = seq_len // tm
        tab = pl.BlockSpec((tm, 256), lambda i: (i % tpb, 0))
        in_specs += [tab, tab]
        args += list(rope_tabs)
    widths = [(F_WIDTH, F32), (768, BF16), (512, BF16), (256, BF16), (256, BF16),
              (512, BF16), (256, BF16), (256, BF16)]
    out_shape = [jax.ShapeDtypeStruct((m, w), dt) for w, dt in widths]
    out_specs = [pl.BlockSpec((tm, w), lambda i: (i, 0)) for w, _ in widths]
    aliases = {}
    if not rope:
        bpt = tm // seq_len
        out_shape += [jax.ShapeDtypeStruct((m // seq_len, DEPTH, seq_len, 128), F32)] * 4
        out_specs += [pl.BlockSpec((bpt, None, seq_len, 128), lambda i: (i, layer, 0, 0))] * 4
        if new_kv_prev is not None:
            aliases = {len(args) + t: len(widths) + t for t in range(4)}
            in_specs += [pl.BlockSpec(memory_space=pl.ANY)] * 4
            args += list(new_kv_prev)
    return pl.pallas_call(
        functools.partial(_inproj_kernel, rope=rope, n_alias=len(aliases)),
        out_shape=out_shape, grid=(m // tm,), in_specs=in_specs, out_specs=out_specs,
        input_output_aliases=aliases,
        compiler_params=_params(("arbitrary",)), name="in_proj",
    )(*args)


def _online(m, acc, s_blocks, v_blocks):
    m_new = m
    for s in s_blocks:
        m_new = jnp.maximum(m_new, jnp.max(s, axis=-1, keepdims=True))
    acc = jnp.exp2(m - m_new) * acc
    for s, v in zip(s_blocks, v_blocks):
        acc = acc + _dot(jnp.exp2(s - m_new).astype(BF16), v)
    return m_new, acc


def _attend(q_ref, k_ref, v_ref, kc_ref, vc_ref, sink_ref, o_ref, *, qi, seq_len, tq, kv_chunk, window):
    rows = 2 * tq
    q2 = [jnp.concatenate([q_ref[:, 256 * j:256 * j + 128], q_ref[:, 256 * j + 128:256 * (j + 1)]], axis=0)
          for j in range(2)]

    def update(carry, kv_blocks, masks):
        out = []
        for j in range(2):
            lanes = slice(128 * j, 128 * (j + 1))
            s_blocks = []
            for (k_blk, _), mask in zip(kv_blocks, masks):
                s = _dot_nt(q2[j], k_blk[:, lanes])
                s_blocks.append(s if mask is None else jnp.where(mask, s, -jnp.inf))
            out.append(_online(*carry[j], s_blocks, [v_blk[:, lanes] for _, v_blk in kv_blocks]))
        return tuple(out)

    if sink_ref is not None:
        upper = lax.broadcasted_iota(jnp.int32, (rows, 128), 1) >= HEAD_DIM
        carry = tuple((jnp.max(sink_ref[j], axis=-1, keepdims=True) * LOG2E, jnp.where(upper, 1.0, 0.0))
                      for j in range(2))
    else:
        carry = tuple((jnp.full((rows, 1), -jnp.inf, F32), jnp.zeros((rows, 128), F32)) for _ in range(2))
    if window:
        span = tq + 2 * WINDOW
        start = pl.multiple_of(jnp.clip(qi * tq - WINDOW, 0, seq_len - span), 128)
        qpos = qi * tq + lax.broadcasted_iota(jnp.int32, (rows, span), 0) % tq
        kpos = start + lax.broadcasted_iota(jnp.int32, (rows, span), 1)
        blocks = [(k_ref[pl.ds(start, span), :], v_ref[pl.ds(start, span), :])]
        masks = [jnp.abs(qpos - kpos) <= WINDOW]
        if kc_ref is not None:
            blocks.append((kc_ref[...], vc_ref[...]))
            masks.append(None)
        carry = update(carry, blocks, masks)
    else:
        for c in range(seq_len // kv_chunk):
            blk = slice(c * kv_chunk, (c + 1) * kv_chunk)
            carry = update(carry, [(k_ref[blk, :], v_ref[blk, :])], [None])
        if kc_ref is not None:
            carry = update(carry, [(kc_ref[...], vc_ref[...])], [None])

    low = lax.broadcasted_iota(jnp.int32, (tq, 128), 1) < HEAD_DIM
    for j in range(2):
        acc = carry[j][1]
        o = acc / pltpu.roll(acc, HEAD_DIM, 1)
        o_ref[:, 128 * j:128 * (j + 1)] = jnp.where(low, o[:tq], pltpu.roll(o[tq:], HEAD_DIM, 1)).astype(o_ref.dtype)


def _attn_kernel(*refs, seq_len, tq, kv_chunk, mixers):
    n_in = sum(3 + 2 * c + s for c, s, _ in mixers)
    outs = refs[n_in:]
    i = 0
    for o_ref, (has_cache, has_sink, window) in zip(outs, mixers):
        q_ref, k_ref, v_ref = refs[i:i + 3]
        i += 3
        kc_ref, vc_ref = refs[i:i + 2] if has_cache else (None, None)
        i += 2 * has_cache
        sink_ref = refs[i] if has_sink else None
        i += has_sink
        _attend(q_ref, k_ref, v_ref, kc_ref, vc_ref, sink_ref, o_ref, qi=pl.program_id(1),
                seq_len=seq_len, tq=tq, kv_chunk=kv_chunk, window=window)


def _attn_call(mixers, nb, seq_len):
    tq = min(ATTN_TILE, seq_len)
    nq = seq_len // tq
    kv_chunk = min(KV_CHUNK, seq_len)
    in_specs, args, static = [], [], []
    for mx in mixers:
        in_specs += [pl.BlockSpec((tq, 512), lambda b, i: (b * nq + i, 0)),
                     pl.BlockSpec((seq_len, 256), lambda b, i: (b, 0)),
                     pl.BlockSpec((seq_len, 256), lambda b, i: (b, 0))]
        args += [mx["q"], mx["k"], mx["v"]]
        if mx.get("cache") is not None:
            in_specs += [pl.BlockSpec((None, PAST_LEN, 256), lambda b, i: (b, 0, 0))] * 2
            args += list(mx["cache"])
        if mx.get("sink") is not None:
            in_specs.append(pl.BlockSpec((2, 2 * tq, 128), lambda b, i: (0, 0, 0)))
            args.append(mx["sink"])
        static.append((mx.get("cache") is not None, mx.get("sink") is not None, bool(mx.get("window"))))
    out = jax.ShapeDtypeStruct((nb * seq_len, 256), BF16)
    out_spec = pl.BlockSpec((tq, 256), lambda b, i: (b * nq + i, 0))
    return pl.pallas_call(
        functools.partial(_attn_kernel, seq_len=seq_len, tq=tq, kv_chunk=kv_chunk, mixers=tuple(static)),
        out_shape=[out] * len(mixers), grid=(nb, nq), in_specs=in_specs, out_specs=[out_spec] * len(mixers),
        compiler_params=_params(("arbitrary", "arbitrary")), name="attention",
    )(*args)


def _chunk_of(s, nc):
    return jnp.where(s < nc, nc - 1 - s, s - nc)


STEP_CHUNKS = 2


def _when(always, pred):
    if always:
        return lambda fn: fn()
    return pl.when(pred)


def _run_sweeps(single, s, ns, backward, forward):
    if single:
        backward()
        forward()
    else:
        pl.when(s < ns)(backward)
        pl.when(s >= ns)(forward)


def _chunks_per_step(nc):
    return STEP_CHUNKS if nc % STEP_CHUNKS == 0 else 1


def _log_gamma(rd):
    return jnp.log1p(-jnp.exp(rd))


def _ret_kernel(*refs, nc, sub, has_init, emit_state):
    q_ref, k_ref, v_ref, g_ref, rdl_ref, rdh_ref, gn_ref, seg_ref = refs[:8]
    i = 8
    if has_init:
        s0_ref = refs[i]
        i += 1
    y_ref = refs[i]
    i += 1
    if emit_state:
        st_ref = refs[i]
        i += 1
    m_sc, dec_sc, qk_sc, sf_sc, sb_sc, sball_sc = refs[i:i + 6]
    c_len = CHUNK
    b = pl.program_id(0)
    s = pl.program_id(1)

    @pl.when(jnp.logical_and(b == 0, s == 0))
    def _():
        ri = lax.broadcasted_iota(jnp.int32, (c_len, c_len), 0)
        ci = lax.broadcasted_iota(jnp.int32, (c_len, c_len), 1)
        d = (ri - ci).astype(F32)
        for h in range(N_HEADS):
            lgf = _log_gamma(jnp.broadcast_to(rdh_ref[0, h], (c_len, c_len)))
            lgb = _log_gamma(jnp.broadcast_to(rdh_ref[1, h], (c_len, c_len)))
            m_sc[h] = jnp.where(d > 0, jnp.exp(d * lgf), jnp.where(d < 0, jnp.exp(-d * lgb), 2.0))
        for dr in range(2):
            rows = [jnp.exp(c_len * _log_gamma(jnp.broadcast_to(rdh_ref[dr, h], (HEAD_DIM, c_len))))
                    for h in range(N_HEADS)]
            dec_sc[dr] = jnp.concatenate(rows, axis=0)
        a = lax.broadcasted_iota(jnp.int32, (c_len, 256), 0).astype(F32)
        lgf = _log_gamma(rdl_ref[0])
        lgb = _log_gamma(rdl_ref[1])
        qk_sc[0] = jnp.exp((a + 1.0) * lgf)
        qk_sc[1] = jnp.exp((c_len - 1.0 - a) * lgf)
        qk_sc[2] = jnp.exp((c_len - a) * lgb)
        qk_sc[3] = jnp.exp(a * lgb)

    @pl.when(s == 0)
    def _():
        sf_sc[...] = jnp.zeros_like(sf_sc)
        sb_sc[...] = jnp.zeros_like(sb_sc)
        if has_init:
            for h in range(N_HEADS):
                blk = slice(HEAD_DIM * h, HEAD_DIM * (h + 1))
                sf_sc[blk, blk] = s0_ref[0, h]
                sb_sc[blk, blk] = s0_ref[1, h]

    ri = lax.broadcasted_iota(jnp.int32, (256, 256), 0) // HEAD_DIM
    ci = lax.broadcasted_iota(jnp.int32, (256, 256), 1) // HEAD_DIM
    diag = ri == ci
    ns = nc // sub
    single = ns == 1
    sf = 0 if single else s - ns

    def backward():
        state = sb_sc[...]
        for u in reversed(range(sub)):
            rows = slice(u * c_len, (u + 1) * c_len)
            sball_sc[(ns - 1 - s) * sub + u] = state
            kd = (k_ref[rows, :].astype(F32) * qk_sc[3]).astype(BF16)
            state = state * dec_sc[1] + jnp.where(diag, _dot_tn(kd, v_ref[rows, :]), 0.0)
        sb_sc[...] = state
        if emit_state:
            @_when(single, s == ns - 1)
            def _():
                for h in range(N_HEADS):
                    blk = slice(HEAD_DIM * h, HEAD_DIM * (h + 1))
                    st_ref[1, h] = sb_sc[blk, blk]

    def forward():
        state = sf_sc[...]
        lane_head = lax.broadcasted_iota(jnp.int32, (c_len, 256), 1) // HEAD_DIM
        seg = seg_ref[...]
        for u in range(sub):
            rows = slice(u * c_len, (u + 1) * c_len)
            qb = q_ref[rows, :]
            kb = k_ref[rows, :]
            vb = v_ref[rows, :]
            q = qb.astype(F32)
            o = (_dot(qb, state.astype(BF16)) * qk_sc[0]
                 + _dot(qb, sball_sc[sf * sub + u].astype(BF16)) * qk_sc[2])
            for h in range(N_HEADS):
                mine = lane_head == h
                sc = _dot_nt(jnp.where(mine, q, 0.0).astype(BF16), kb)
                pv = _dot((sc * m_sc[h]).astype(BF16), vb)
                o = o + jnp.where(mine, pv, 0.0)
            mu = _seg_sum(o, seg) * (1.0 / HEAD_DIM)
            dlt = o - mu
            var = _seg_sum(dlt * dlt, seg) * (1.0 / HEAD_DIM)
            on = dlt * lax.rsqrt(var + 1e-5) * gn_ref[...]
            y_ref[rows, :] = (on * _silu(g_ref[rows, :])).astype(y_ref.dtype)
            kd = (kb.astype(F32) * qk_sc[1]).astype(BF16)
            state = state * dec_sc[0] + jnp.where(diag, _dot_tn(kd, vb), 0.0)
        sf_sc[...] = state
        if emit_state:
            @_when(single, sf == ns - 1)
            def _():
                for h in range(N_HEADS):
                    blk = slice(HEAD_DIM * h, HEAD_DIM * (h + 1))
                    st_ref[0, h] = sf_sc[blk, blk]

    return backward, forward


def _ret_specs(rqkv, pf, nb, seq_len, rd_lane, rd_head, gn_g, seg, layer, state0=None, emit_state=False):
    c_len = CHUNK
    nc = seq_len // c_len
    sub = _chunks_per_step(nc)
    ns = nc // sub
    rows = sub * c_len

    def col(start):
        return pl.BlockSpec((rows, 256), lambda b, s: (b * ns + _chunk_of(s, ns), start // 256))

    in_specs = [col(0), col(256), col(512), col(F_RG),
                pl.BlockSpec((2, 1, 256), lambda b, s: (0, 0, 0)),
                pl.BlockSpec((2, N_HEADS, 1, c_len), lambda b, s: (0, 0, 0, 0)),
                pl.BlockSpec((1, 256), lambda b, s: (0, 0)),
                pl.BlockSpec((256, 256), lambda b, s: (0, 0))]
    args = [rqkv, rqkv, rqkv, pf, rd_lane, rd_head, gn_g, seg]
    if state0 is not None:
        in_specs.append(pl.BlockSpec((None, None, 2, N_HEADS, HEAD_DIM, HEAD_DIM),
                                     lambda b, s: (b, layer, 0, 0, 0, 0)))
        args.append(state0)
    outs = [jax.ShapeDtypeStruct((nb * seq_len, 256), BF16)]
    out_specs = [pl.BlockSpec((rows, 256), lambda b, s: (b * ns + jnp.maximum(s - ns, 0), 0))]
    if emit_state:
        outs.append(jax.ShapeDtypeStruct((nb, DEPTH, 2, N_HEADS, HEAD_DIM, HEAD_DIM), F32))
        out_specs.append(pl.BlockSpec((None, None, 2, N_HEADS, HEAD_DIM, HEAD_DIM),
                                      lambda b, s: (b, layer, 0, 0, 0, 0)))
    scratch = [pltpu.VMEM((N_HEADS, c_len, c_len), F32), pltpu.VMEM((2, 256, 256), F32),
               pltpu.VMEM((4, c_len, 256), F32), pltpu.VMEM((256, 256), F32),
               pltpu.VMEM((256, 256), F32), pltpu.VMEM((nc, 256, 256), F32)]
    return in_specs, args, outs, out_specs, scratch


def _cumsum_rows(x):
    n = x.shape[0]
    row = lax.broadcasted_iota(jnp.int32, x.shape, 0)
    sh = 1
    while sh < n:
        x = x + jnp.where(row >= sh, pltpu.roll(x, sh, 0), 0.0)
        sh *= 2
    return x


def _ssd_kernel(*refs, nc, sub, has_init, emit_state):
    xbc_ref, prev_ref, next_ref, z_ref, dt_ref, cw_ref, cb_ref, dtb_ref, al_ref, dl_ref, ng_ref = refs[:11]
    i = 11
    if has_init:
        h0_ref = refs[i]
        i += 1
    y_ref = refs[i]
    i += 1
    if emit_state:
        st_ref = refs[i]
        i += 1
    hf_sc, hb_sc, hball_sc, xs_sc, bm_sc, cm_sc, dt_sc, cum_sc, rev_sc = refs[i:i + 9]
    c_len = CHUNK
    ns = nc // sub
    s = pl.program_id(1)
    single = ns == 1
    sf = 0 if single else s - ns

    @pl.when(s == 0)
    def _():
        if has_init:
            hf_sc[...] = h0_ref[0]
            hb_sc[...] = h0_ref[1]
        else:
            hf_sc[...] = jnp.zeros_like(hf_sc)
            hb_sc[...] = jnp.zeros_like(hb_sc)

    low = lax.broadcasted_iota(jnp.int32, (c_len, 128), 1) < HEAD_DIM
    low_rows = lax.broadcasted_iota(jnp.int32, (128, 128), 0) < HEAD_DIM

    def col(a, lane):
        return jnp.broadcast_to(a[:, lane:lane + 1], (c_len, c_len))

    def pair(a0, a1):
        return jnp.where(low, a0[:, :128], a1[:, :128])

    def state_update(h, g, xs, bm, dt, cols, tot_row, dt_lanes):
        w = [jnp.exp2(tot_row[r][:, :128] - cols[r][:, :128]) * col(dt, dt_lanes[r])[:, :128] for r in range(2)]
        xw = (xs[:, 128 * g:128 * (g + 1)] * jnp.where(low, w[0], w[1])).astype(BF16)
        dec = jnp.where(low_rows, jnp.broadcast_to(jnp.exp2(tot_row[0][:, :128]), (128, 128)),
                        jnp.broadcast_to(jnp.exp2(tot_row[1][:, :128]), (128, 128)))
        return h * dec + _dot_tn(xw, bm[:, 128 * g:128 * (g + 1)])

    def backward():
        cg = ns - 1 - s
        n_rows = sub * c_len
        xbc = xbc_ref[...]
        before = jnp.where(cg > 0, prev_ref[7:8, :], 0.0)
        after = jnp.where(cg < ns - 1, next_ref[0:1, :], 0.0)
        row = lax.broadcasted_iota(jnp.int32, xbc.shape, 0)
        x_prev = jnp.where(row == 0, before, pltpu.roll(xbc, 1, 0))
        x_next = jnp.where(row == n_rows - 1, after, pltpu.roll(xbc, n_rows - 1, 0))
        conv = _silu(cw_ref[0:1, :] * x_prev + cw_ref[1:2, :] * xbc + cw_ref[2:3, :] * x_next + cb_ref[...])
        pre = dt_ref[...] + dtb_ref[...]
        dt_all = jnp.maximum(pre, 0.0) + jnp.log1p(jnp.exp(-jnp.abs(pre)))
        dta_all = dt_all * (-jnp.exp(al_ref[...]) * LOG2E)
        state = [hb_sc[g] for g in range(2)]
        for u in reversed(range(sub)):
            rows = slice(u * c_len, (u + 1) * c_len)
            c = cg * sub + u
            xs = conv[rows, 0:256]
            bm = conv[rows, 256:512].astype(BF16)
            dt = dt_all[rows, :]
            dta = dta_all[rows, :]
            cum = _cumsum_rows(dta)
            rev = cum[c_len - 1:c_len, :] - cum + dta
            xs_sc[c] = xs
            bm_sc[c] = bm
            cm_sc[c] = conv[rows, 512:768].astype(BF16)
            dt_sc[c] = dt
            cum_sc[c] = cum
            rev_sc[c] = rev
            for g in range(2):
                hball_sc[c, g] = state[g]
                cols = [col(rev, 4 + 2 * g + r) for r in range(2)]
                tot = [cl[0:1, :] for cl in cols]
                state[g] = state_update(state[g], g, xs, bm, dt, cols, tot, [4 + 2 * g, 5 + 2 * g])
        for g in range(2):
            hb_sc[g] = state[g]
        if emit_state:
            @_when(single, s == ns - 1)
            def _():
                st_ref[1] = hb_sc[...]

    def forward():
        ri = lax.broadcasted_iota(jnp.int32, (c_len, c_len), 0)
        ci = lax.broadcasted_iota(jnp.int32, (c_len, c_len), 1)
        below = ri > ci
        on_diag = ri == ci
        lane_head = lax.broadcasted_iota(jnp.int32, (c_len, 256), 1) // HEAD_DIM
        state = [hf_sc[g] for g in range(2)]
        for u in range(sub):
            rows = slice(u * c_len, (u + 1) * c_len)
            c = sf * sub + u
            xs = xs_sc[c]
            bm = bm_sc[c]
            cm = cm_sc[c]
            dt = dt_sc[c]
            cum = cum_sc[c]
            rev = rev_sc[c]
            log_dt = jnp.log2(dt)
            off_f = (log_dt - cum).T
            off_b = (log_dt - rev).T
            dt_t = dt.T
            xs_b = xs.astype(BF16)
            y = jnp.zeros((c_len, 256), F32)
            inter = []
            for g in range(2):
                gl = slice(128 * g, 128 * (g + 1))
                gmat = _dot_nt(cm[:, gl], bm[:, gl])
                cols_f, cols_b = [], []
                for r in range(2):
                    h = 2 * g + r
                    cf = col(cum, h)
                    cb = col(rev, 4 + h)
                    cols_f.append(cf)
                    cols_b.append(cb)
                    w = jnp.exp2(jnp.where(below, cf + off_f[h:h + 1, :], cb + off_b[4 + h:5 + h, :]))
                    w = w + jnp.where(on_diag, dt_t[h:h + 1, :], 0.0)
                    yh = _dot((gmat * w).astype(BF16), xs_b)
                    y = y + jnp.where(lane_head == h, yh, 0.0)
                yf = _dot_nt(cm[:, gl], state[g].astype(BF16))
                yb = _dot_nt(cm[:, gl], hball_sc[c, g].astype(BF16))
                inter.append(yf * pair(jnp.exp2(cols_f[0]), jnp.exp2(cols_f[1]))
                             + yb * pair(jnp.exp2(cols_b[0]), jnp.exp2(cols_b[1])))
                tot = [cl[c_len - 1:c_len, :] for cl in cols_f]
                state[g] = state_update(state[g], g, xs, bm, dt, cols_f, tot, [2 * g, 2 * g + 1])
            y = y + jnp.concatenate(inter, axis=1) + xs * dl_ref[...]
            y = y * _silu(z_ref[rows, :])
            ms = jnp.mean(y * y, axis=-1, keepdims=True)
            y_ref[rows, :] = (y * lax.rsqrt(ms + 1e-6) * ng_ref[...]).astype(y_ref.dtype)
        for g in range(2):
            hf_sc[g] = state[g]
        if emit_state:
            @_when(single, sf == ns - 1)
            def _():
                st_ref[0] = hf_sc[...]

    return backward, forward


def _ssd_specs(p, nb, seq_len, conv_w, conv_b, dt_bias, a_log, d_lane, norm_g, layer,
              state0=None, emit_state=False):
    c_len = CHUNK
    nc = seq_len // c_len
    sub = _chunks_per_step(nc)
    ns = nc // sub
    rows = sub * c_len
    m = nb * seq_len
    r8 = rows // 8
    chunk = lambda b, s: b * ns + jnp.maximum(ns - 1 - s, 0)
    in_specs = [pl.BlockSpec((rows, 768), lambda b, s: (chunk(b, s), F_XBC // 768)),
                pl.BlockSpec((8, 768), lambda b, s: (jnp.maximum(chunk(b, s) * r8 - 1, 0), F_XBC // 768)),
                pl.BlockSpec((8, 768), lambda b, s: (jnp.minimum((chunk(b, s) + 1) * r8, m // 8 - 1),
                                                     F_XBC // 768)),
                pl.BlockSpec((rows, 256), lambda b, s: (b * ns + jnp.maximum(s - ns, 0), F_SZ // 256)),
                pl.BlockSpec((rows, 128), lambda b, s: (chunk(b, s), F_DT // 128)),
                pl.BlockSpec((None, 3, 768), lambda b, s: (layer, 0, 0)),
                pl.BlockSpec((1, 768), lambda b, s: (0, 0)),
                pl.BlockSpec((1, 128), lambda b, s: (0, 0)),
                pl.BlockSpec((1, 128), lambda b, s: (0, 0)),
                pl.BlockSpec((1, 256), lambda b, s: (0, 0)),
                pl.BlockSpec((1, 256), lambda b, s: (0, 0))]
    args = [p, p, p, p, p, conv_w, conv_b, dt_bias, a_log, d_lane, norm_g]
    if state0 is not None:
        in_specs.append(pl.BlockSpec((None, None, 2, 2, 128, SSD_STATE),
                                     lambda b, s: (b, layer, 0, 0, 0, 0)))
        args.append(state0)
    outs = [jax.ShapeDtypeStruct((m, 256), BF16)]
    out_specs = [pl.BlockSpec((rows, 256), lambda b, s: (b * ns + jnp.maximum(s - ns, 0), 0))]
    if emit_state:
        outs.append(jax.ShapeDtypeStruct((nb, DEPTH, 2, 2, 128, SSD_STATE), F32))
        out_specs.append(pl.BlockSpec((None, None, 2, 2, 128, SSD_STATE),
                                      lambda b, s: (b, layer, 0, 0, 0, 0)))
    scratch = [pltpu.VMEM((2, 128, SSD_STATE), F32), pltpu.VMEM((2, 128, SSD_STATE), F32),
               pltpu.VMEM((nc, 2, 128, SSD_STATE), F32),
               pltpu.VMEM((nc, c_len, 256), F32), pltpu.VMEM((nc, c_len, 256), BF16),
               pltpu.VMEM((nc, c_len, 256), BF16), pltpu.VMEM((nc, c_len, 128), F32),
               pltpu.VMEM((nc, c_len, 128), F32), pltpu.VMEM((nc, c_len, 128), F32)]
    return in_specs, args, outs, out_specs, scratch


def _scan_kernel(*refs, counts, nc, sub, **static):
    (ri, si), (ro, so), (rs, ss), n_alias = counts
    ins, rest = refs[:ri + si], refs[ri + si + n_alias:]
    outs, scr = rest[:ro + so], rest[ro + so:]
    ret_bwd, ret_fwd = _ret_kernel(*ins[:ri], *outs[:ro], *scr[:rs], nc=nc, sub=sub, **static)
    ssd_bwd, ssd_fwd = _ssd_kernel(*ins[ri:], *outs[ro:], *scr[rs:], nc=nc, sub=sub, **static)
    ns = nc // sub
    if ns == 1:
        ret_bwd()
        ret_fwd()
        ssd_bwd()
        ssd_fwd()
        return

    def backward():
        ret_bwd()
        ssd_bwd()

    def forward():
        ret_fwd()
        ssd_fwd()

    _run_sweeps(False, pl.program_id(1), ns, backward, forward)


def _scan_call(ret_parts, ssd_parts, nb, seq_len, has_init, emit_state, states_prev=None):
    nc = seq_len // CHUNK
    sub = _chunks_per_step(nc)
    in_specs = ret_parts[0] + ssd_parts[0]
    args = ret_parts[1] + ssd_parts[1]
    aliases = {}
    if states_prev is not None:
        aliases = {len(args): 1, len(args) + 1: len(ret_parts[2]) + 1}
        in_specs = in_specs + [pl.BlockSpec(memory_space=pl.ANY)] * 2
        args = args + list(states_prev)
    counts = tuple((len(a), len(b)) for a, b in zip((ret_parts[0], ret_parts[2], ret_parts[4]),
                                                    (ssd_parts[0], ssd_parts[2], ssd_parts[4])))
    return pl.pallas_call(
        functools.partial(_scan_kernel, counts=counts + (len(aliases),), nc=nc, sub=sub, has_init=has_init,
                          emit_state=emit_state),
        out_shape=ret_parts[2] + ssd_parts[2], grid=(nb, 1 if nc == sub else 2 * nc // sub),
        in_specs=in_specs, out_specs=ret_parts[3] + ssd_parts[3],
        scratch_shapes=ret_parts[4] + ssd_parts[4], input_output_aliases=aliases,
        compiler_params=_params(("arbitrary", "arbitrary")), name="scans",
    )(*args)


def _layer_norm(x, g, b):
    mu = jnp.mean(x, axis=-1, keepdims=True)
    d = x - mu
    var = jnp.mean(d * d, axis=-1, keepdims=True)
    return d * lax.rsqrt(var + 1e-5) * g + b


FFN_CHUNK = 256
FFN_ROWS = 512


def _outffn_kernel(x_ref, yr_ref, yg_ref, ys_ref, yw_ref, g1_ref, sh2_ref, sc2_ref, g2_ref,
                   wo_ref, wi_ref, wf_ref, l1g_ref, l1b_ref, l2g_ref, l2b_ref, o_ref):
    mix = (_dot(yr_ref[...], wo_ref[0:256, :]) + _dot(yg_ref[...], wo_ref[256:512, :])
           + _dot(ys_ref[...], wo_ref[512:768, :]) + _dot(yw_ref[...], wo_ref[768:1024, :]))
    x1 = _layer_norm(ALPHA * x_ref[...] + g1_ref[...] * mix, l1g_ref[...], l1b_ref[...])
    h2 = (x1 * (1.0 + sc2_ref[...]) + sh2_ref[...]).astype(BF16)
    acc = None
    for j in range(D_FF // FFN_CHUNK):
        cols = slice(FFN_CHUNK * j, FFN_CHUNK * (j + 1))
        gate = _dot(h2, wi_ref[:, cols])
        up = _dot(h2, wi_ref[:, D_FF + FFN_CHUNK * j:D_FF + FFN_CHUNK * (j + 1)])
        d = _dot((_silu(gate) * up).astype(BF16), wf_ref[cols, :])
        acc = d if acc is None else acc + d
    o_ref[...] = _layer_norm(ALPHA * x1 + g2_ref[...] * acc, l2g_ref[...], l2b_ref[...])


def _outffn_call(x, ys, mod5, w_out_b, w_ffn_in_b, w_ffn_out_b, ln, layer, row0, tiles_per_row):
    m = x.shape[0]
    tm = min(FFN_ROWS, m)
    tiles_per_row = max(tiles_per_row * TOKEN_TILE // tm, 1)
    row = lambda w: pl.BlockSpec((tm, w), lambda i: (i, 0))
    once = pl.Buffered(1)
    vec = pl.BlockSpec((None, 1, D_MODEL), lambda i: (layer, 0, 0))
    in_specs = [row(D_MODEL), row(256), row(256), row(256), row(256),
                _mod_spec(layer, row0, tiles_per_row, 2), _mod_spec(layer, row0, tiles_per_row, 3),
                _mod_spec(layer, row0, tiles_per_row, 4), _mod_spec(layer, row0, tiles_per_row, 5),
                pl.BlockSpec((None, D_MODEL, D_MODEL), lambda i: (layer, 0, 0), pipeline_mode=once),
                pl.BlockSpec((None, D_MODEL, 2 * D_FF), lambda i: (layer, 0, 0), pipeline_mode=once),
                pl.BlockSpec((None, D_FF, D_MODEL), lambda i: (layer, 0, 0), pipeline_mode=once),
                vec, vec, vec, vec]
    return pl.pallas_call(
        _outffn_kernel,
        out_shape=jax.ShapeDtypeStruct((m, D_MODEL), F32),
        grid=(m // tm,), in_specs=in_specs, out_specs=row(D_MODEL),
        compiler_params=_params(("arbitrary",)), name="out_ffn",
    )(x, *ys, mod5, mod5, mod5, mod5, w_out_b, w_ffn_in_b, w_ffn_out_b, *ln)


def _rope_tables(seq_len):
    t = jnp.arange(seq_len)
    rowp = (t // GRID_W).astype(F32)
    colp = (t % GRID_W).astype(F32)
    quarter = HEAD_DIM // 4
    inv_freq = ROPE_THETA ** (-jnp.arange(quarter, dtype=F32) / quarter)
    ang_r = rowp[:, None] * inv_freq[None, :]
    ang_c = colp[:, None] * inv_freq[None, :]
    cos = jnp.concatenate([jnp.cos(ang_r)] * 2 + [jnp.cos(ang_c)] * 2, axis=-1)
    sin = jnp.concatenate([-jnp.sin(ang_r), jnp.sin(ang_r), -jnp.sin(ang_c), jnp.sin(ang_c)], axis=-1)
    return jnp.tile(cos, (1, N_HEADS)), jnp.tile(sin, (1, N_HEADS))


def _expand_cache(c, fill):
    pad = jnp.full(c.shape[:2] + (HEAD_DIM,), fill, c.dtype)
    return jnp.concatenate([c[:, :, 0], pad, c[:, :, 1], pad], axis=-1).astype(BF16)


def _split_w_in(w):
    pad = jnp.zeros(w.shape[:-1] + (TAIL_W - T_DT - 8,), w.dtype)
    tail = jnp.concatenate([w[..., MAIN_W + 8:], w[..., MAIN_W:MAIN_W + 8], pad], axis=-1)
    return w.astype(BF16), tail.astype(BF16)


def kernel(x_prompt, x_sample, cache_gqa_k, cache_gqa_v, cache_swa_k, cache_swa_v, state_ret, state_ssd,
           c, c_ctx, ada_w, ada_b, w_in, ret_decay, ret_gn_g, gqa_q_norm, gqa_k_norm,
           ssd_conv_w, ssd_conv_b, ssd_dt_bias, ssd_a_log, ssd_d, ssd_norm_g, swa_sink, w_out,
           ln1_g, ln1_b, w_ffn_in, w_ffn_out, ln2_g, ln2_b):
    nb_c, len_c, _ = x_prompt.shape
    nb_l, len_l, _ = x_sample.shape
    depth = ada_w.shape[0]

    cond8 = jnp.zeros((8, D_MODEL), F32).at[0].set(c_ctx).at[1:1 + nb_l].set(c)
    mod5 = _ada_call(cond8, ada_w, ada_b)

    w_main, w_tail = _split_w_in(w_in)
    w_out_b = w_out.astype(BF16)
    w_ffn_in_b = w_ffn_in.astype(BF16)
    w_ffn_out_b = w_ffn_out.astype(BF16)
    lane_head = jnp.arange(256) // HEAD_DIM
    seg = (lane_head[:, None] == lane_head[None, :]).astype(BF16)
    rope_tabs = _rope_tables(len_l)
    caches_g = (_expand_cache_layers(cache_gqa_k, 0.0), _expand_cache_layers(cache_gqa_v, 1.0))
    caches_w = (_expand_cache_layers(cache_swa_k, 0.0), _expand_cache_layers(cache_swa_v, 1.0))
    state_ssd_g = state_ssd.reshape(nb_l, depth, 2, 2, 128, SSD_STATE)

    def layer_consts(l):
        rd = ret_decay[l]
        return dict(
            rd_lane=jnp.repeat(rd, HEAD_DIM, axis=-1).reshape(2, 1, 256),
            rd_head=jnp.broadcast_to(rd[:, :, None, None], (2, N_HEADS, 1, CHUNK)),
            gn_g=ret_gn_g[l].reshape(1, 256),
            qn=jnp.tile(gqa_q_norm[l], N_HEADS).reshape(1, 256),
            kn=jnp.tile(gqa_k_norm[l], N_HEADS).reshape(1, 256),
            conv_b=ssd_conv_b[l].reshape(1, 768),
            dt_bias=jnp.zeros((1, 128), F32).at[0, :8].set(ssd_dt_bias[l].reshape(8)),
            a_log=jnp.zeros((1, 128), F32).at[0, :8].set(ssd_a_log[l].reshape(8)),
            d_lane=jnp.repeat(ssd_d[l], HEAD_DIM).reshape(1, 256),
            norm_g=ssd_norm_g[l].reshape(1, 256),
            sink=lambda tq: jnp.broadcast_to(swa_sink[l].reshape(2, 2, 1, 1), (2, 2, tq, 128)
                                             ).reshape(2, 2 * tq, 128),
        )

    ln = tuple(a.reshape(depth, 1, D_MODEL) for a in (ln1_g, ln1_b, ln2_g, ln2_b))

    def run_group(x, nb, seq_len, row0, latent):
        m = nb * seq_len
        tpr = (seq_len // TOKEN_TILE) if latent else (m // TOKEN_TILE)
        x = x.reshape(m, D_MODEL)
        new_kv = states = None
        if not latent:
            new_kv = tuple(jnp.zeros((nb, depth, seq_len, 128), F32) for _ in range(4))
            states = (jnp.zeros((nb, depth, 2, N_HEADS, HEAD_DIM, HEAD_DIM), F32),
                      jnp.zeros((nb, depth, 2, 2, 128, SSD_STATE), F32))
        for l in range(depth):
            k = layer_consts(l)
            proj = _inproj_call(x, mod5, w_main, w_tail, l, row0, tpr, k["qn"], k["kn"], seg,
                                rope_tabs if latent else None, seq_len, new_kv_prev=new_kv)
            pf, rqkv, qg, kg, vg, qw, kw, vw = proj[:8]
            ret_parts = _ret_specs(rqkv, pf, nb, seq_len, k["rd_lane"], k["rd_head"], k["gn_g"], seg, l,
                                   state0=state_ret if latent else None, emit_state=not latent)
            ssd_parts = _ssd_specs(pf, nb, seq_len, ssd_conv_w, k["conv_b"], k["dt_bias"], k["a_log"],
                                   k["d_lane"], k["norm_g"], l,
                                   state0=state_ssd_g if latent else None, emit_state=not latent)
            scans = _scan_call(ret_parts, ssd_parts, nb, seq_len, has_init=latent, emit_state=not latent,
                               states_prev=states)
            if latent:
                y_ret, y_ssd = scans
                y_gqa, y_swa = _attn_call(
                    [dict(q=qg, k=kg, v=vg, cache=(caches_g[0][l], caches_g[1][l])),
                     dict(q=qw, k=kw, v=vw, cache=(caches_w[0][l], caches_w[1][l]),
                          sink=k["sink"](min(ATTN_TILE, seq_len)), window=True)], nb, seq_len)
            else:
                y_ret, st_ret, y_ssd, st_ssd = scans
                states = (st_ret, st_ssd)
                new_kv = proj[8:12]
                y_gqa, = _attn_call([dict(q=qg, k=kg, v=vg)], nb, seq_len)
                y_swa, = _attn_call([dict(q=qw, k=kw, v=vw, sink=k["sink"](min(ATTN_TILE, seq_len)))],
                                    nb, seq_len)
            x = _outffn_call(x, (y_ret, y_gqa, y_ssd, y_swa), mod5, w_out_b, w_ffn_in_b, w_ffn_out_b,
                             ln, l, row0, tpr)
        extras = None
        if not latent:
            extras = tuple(a.reshape(nb, depth, seq_len, 2, HEAD_DIM) for a in new_kv) + (
                states[0], states[1].reshape(nb, depth, 2, N_HEADS, HEAD_DIM, SSD_STATE))
        return x.reshape(nb, seq_len, D_MODEL), extras

    y_prompt, extras = run_group(x_prompt, nb_c, len_c, 0, False)
    y_sample, _ = run_group(x_sample, nb_l, len_l, 1, True)
    return (y_prompt, y_sample) + extras


def _expand_cache_layers(cache, fill):
    return [_expand_cache(cache[:, l], fill) for l in range(cache.shape[1])]
```

```python
import functools
import math

import jax
import jax.numpy as jnp
from jax import lax
from jax.experimental import pallas as pl
from jax.experimental.pallas import tpu as pltpu

F32 = jnp.float32
BF16 = jnp.bfloat16

D_MODEL = 1024
DEPTH = 4
HEAD_DIM = 64
N_HEADS = 4
GRID_W = 64
ROPE_THETA = 10000.0
SSD_STATE = 128
D_FF = 2816
WINDOW = 128
PAST_LEN = 256
ALPHA = (2.0 * DEPTH) ** 0.25
CHUNK = 256
TOKEN_TILE = 256
INPROJ_TILE = 512
ATTN_TILE = 512
KV_CHUNK = 4096
VMEM_LIMIT = 56 * 1024 * 1024
LOG2E = math.log2(math.e)
QK_SCALE = HEAD_DIM ** -0.5 * LOG2E

M_RET, M_RG, M_GQA, M_SZ, M_XBC, MAIN_W = 0, 768, 1024, 1536, 1792, 2560
T_SWA, T_DT, TAIL_W = 0, 512, 640
F_XBC, F_RG, F_SZ, F_DT = 0, 768, 1024, 1280
F_WIDTH = 1408


def _silu(x):
    return x / (1.0 + jnp.exp(-x))


def _dot(a, b):
    return jnp.dot(a, b, preferred_element_type=F32)


def _dot_nt(a, b):
    return lax.dot_general(a, b, (((1,), (1,)), ((), ())), preferred_element_type=F32)


def _dot_tn(a, b):
    return lax.dot_general(a, b, (((0,), (0,)), ((), ())), preferred_element_type=F32)


def _seg_sum(x, seg):
    hi = x.astype(BF16)
    lo = (x - hi.astype(F32)).astype(BF16)
    return _dot(hi, seg) + _dot(lo, seg)


def _params(sem):
    return pltpu.CompilerParams(dimension_semantics=sem, vmem_limit_bytes=VMEM_LIMIT)


def _ada_kernel(c_ref, w_ref, b_ref, o_ref):
    s = _silu(c_ref[...])
    o_ref[...] = _dot(s.astype(BF16), w_ref[...].astype(BF16)) + b_ref[...]


def _ada_call(cond8, ada_w, ada_b):
    tn = D_MODEL
    nt = 6
    out = pl.pallas_call(
        _ada_kernel,
        out_shape=jax.ShapeDtypeStruct((DEPTH, nt, 8, tn), F32),
        grid=(DEPTH, nt),
        in_specs=[pl.BlockSpec((8, D_MODEL), lambda l, j: (0, 0)),
                  pl.BlockSpec((None, D_MODEL, tn), lambda l, j: (l, 0, j)),
                  pl.BlockSpec((None, None, 1, tn), lambda l, j: (l, j, 0, 0))],
        out_specs=pl.BlockSpec((None, None, 8, tn), lambda l, j: (l, j, 0, 0)),
        compiler_params=_params(("arbitrary", "arbitrary")),
        name="ada_mod",
    )(cond8, ada_w, ada_b.reshape(DEPTH, nt, 1, tn))
    return jnp.transpose(out, (0, 2, 1, 3)).reshape(DEPTH, 8, nt, 1, tn)


def _mod_spec(layer, row0, tiles_per_row, k):
    return pl.BlockSpec((None, None, None, 1, D_MODEL),
                        lambda i: (layer, row0 + i // tiles_per_row, k, 0, 0))


def _rope(x, cos, sin_signed):
    w = x.shape[-1]
    lane = lax.broadcasted_iota(jnp.int32, x.shape, 1)
    partner = jnp.where((lane % 32) < 16, pltpu.roll(x, w - 16, 1), pltpu.roll(x, 16, 1))
    return x * cos + partner * sin_signed


def _expand_q(q):
    lane = lax.broadcasted_iota(jnp.int32, (q.shape[0], 128), 1)
    low = lane < HEAD_DIM
    blocks = []
    for j in range(2):
        pair = q[:, 128 * j:128 * (j + 1)]
        blocks.append(jnp.where(low, pair, 0.0))
        blocks.append(jnp.where(low, pltpu.roll(pair, HEAD_DIM, 1), 0.0))
    return jnp.concatenate(blocks, axis=1).astype(BF16)


def _expand_kv(k, fill):
    lane = lax.broadcasted_iota(jnp.int32, k.shape, 1)
    low = lane < HEAD_DIM
    return jnp.concatenate([jnp.where(low, k, fill),
                            jnp.where(low, pltpu.roll(k, HEAD_DIM, 1), fill)], axis=1).astype(BF16)


def _inproj_kernel(*refs, rope, n_alias):
    x_ref, sh_ref, sc_ref, w_ref, wt_ref, qn, kn, seg = refs[:8]
    i = 8
    if rope:
        cos_ref, sin_ref = refs[8:10]
        i = 10
    i += n_alias
    f_o, r_o, qg_o, kg_o, vg_o, qw_o, kw_o, vw_o = refs[i:i + 8]
    h = (x_ref[...] * (1.0 + sc_ref[...]) + sh_ref[...]).astype(BF16)

    f_o[:, F_XBC:F_XBC + 768] = _dot(h, w_ref[:, M_XBC:M_XBC + 768])
    f_o[:, F_RG:F_RG + 256] = _dot(h, w_ref[:, M_RG:M_RG + 256])
    f_o[:, F_SZ:F_SZ + 256] = _dot(h, w_ref[:, M_SZ:M_SZ + 256])
    f_o[:, F_DT:F_DT + 128] = _dot(h, wt_ref[:, T_DT:T_DT + 128])
    ret = _dot(h, w_ref[:, M_RET:M_RET + 768])
    r_o[...] = jnp.concatenate([ret[:, 0:256], ret[:, 256:512] * HEAD_DIM ** -0.5, ret[:, 512:768]],
                               axis=1).astype(BF16)

    seg_m = seg[...]

    def rms(v, g, s):
        ms = _seg_sum(v * v, s) * (1.0 / HEAD_DIM)
        return v * lax.rsqrt(ms + 1e-6) * g

    gqa = _dot(h, w_ref[:, M_GQA:M_GQA + 512])
    swa = _dot(h, wt_ref[:, T_SWA:T_SWA + 512])
    q_g = rms(gqa[:, 0:256], qn[...], seg_m)
    k_g = rms(gqa[:, 256:384], kn[...][:, :128], seg_m[:128, :128])
    q_w = swa[:, 0:256]
    k_w = swa[:, 256:384]
    if rope:
        cos = cos_ref[...]
        sin = sin_ref[...]
        q_g = _rope(q_g, cos, sin)
        q_w = _rope(q_w, cos, sin)
        k_g = _rope(k_g, cos[:, :128], sin[:, :128])
        k_w = _rope(k_w, cos[:, :128], sin[:, :128])
    else:
        for o_ref, piece in zip(refs[i + 8:i + 12], (k_g, gqa[:, 384:512], swa[:, 256:384], swa[:, 384:512])):
            o_ref[...] = piece.reshape(o_ref.shape)
    qg_o[...] = _expand_q(q_g * QK_SCALE)
    qw_o[...] = _expand_q(q_w * QK_SCALE)
    kg_o[...] = _expand_kv(k_g, 0.0)
    kw_o[...] = _expand_kv(k_w, 0.0)
    vg_o[...] = _expand_kv(gqa[:, 384:512], 1.0)
    vw_o[...] = _expand_kv(swa[:, 384:512], 1.0)


def _inproj_call(x, mod5, w_main, w_tail, layer, row0, tiles_per_row, qn256, kn256, seg, rope_tabs, seq_len,
                 new_kv_prev=None):
    m = x.shape[0]
    tm = INPROJ_TILE
    tiles_per_row = tiles_per_row * TOKEN_TILE // tm
    rope = rope_tabs is not None
    const = lambda shape: pl.BlockSpec(shape, lambda i: (0, 0))
    in_specs = [pl.BlockSpec((tm, D_MODEL), lambda i: (i, 0)),
                _mod_spec(layer, row0, tiles_per_row, 0),
                _mod_spec(layer, row0, tiles_per_row, 1),
                pl.BlockSpec((None, D_MODEL, MAIN_W), lambda i: (layer, 0, 0)),
                pl.BlockSpec((None, D_MODEL, TAIL_W), lambda i: (layer, 0, 0)),
                const((1, 256)), const((1, 256)), const((256, 256))]
    args = [x, mod5, mod5, w_main, w_tail, qn256, kn256, seg]
    if rope:
        tpb = seq_len // tm
        tab = pl.BlockSpec((tm, 256), lambda i: (i % tpb, 0))
        in_specs += [tab, tab]
        args += list(rope_tabs)
    widths = [(F_WIDTH, F32), (768, BF16), (512, BF16), (256, BF16), (256, BF16),
              (512, BF16), (256, BF16), (256, BF16)]
    out_shape = [jax.ShapeDtypeStruct((m, w), dt) for w, dt in widths]
    out_specs = [pl.BlockSpec((tm, w), lambda i: (i, 0)) for w, _ in widths]
    aliases = {}
    if not rope:
        bpt = tm // seq_len
        out_shape += [jax.ShapeDtypeStruct((m // seq_len, DEPTH, seq_len, 128), F32)] * 4
        out_specs += [pl.BlockSpec((bpt, None, seq_len, 128), lambda i: (i, layer, 0, 0))] * 4
        if new_kv_prev is not None:
            aliases = {len(args) + t: len(widths) + t for t in range(4)}
            in_specs += [pl.BlockSpec(memory_space=pl.ANY)] * 4
            args += list(new_kv_prev)
    return pl.pallas_call(
        functools.partial(_inproj_kernel, rope=rope, n_alias=len(aliases)),
        out_shape=out_shape, grid=(m // tm,), in_specs=in_specs, out_specs=out_specs,
        input_output_aliases=aliases,
        compiler_params=_params(("arbitrary",)), name="in_proj",
    )(*args)


def _online(m, acc, s_blocks, v_blocks):
    m_new = m
    for s in s_blocks:
        m_new = jnp.maximum(m_new, jnp.max(s, axis=-1, keepdims=True))
    acc = jnp.exp2(m - m_new) * acc
    for s, v in zip(s_blocks, v_blocks):
        acc = acc + _dot(jnp.exp2(s - m_new).astype(BF16), v)
    return m_new, acc


def _attend(q_ref, k_ref, v_ref, kc_ref, vc_ref, sink_ref, o_ref, *, qi, seq_len, tq, kv_chunk, window):
    rows = 2 * tq
    q2 = [jnp.concatenate([q_ref[:, 256 * j:256 * j + 128], q_ref[:, 256 * j + 128:256 * (j + 1)]], axis=0)
          for j in range(2)]

    def update(carry, kv_blocks, masks):
        out = []
        for j in range(2):
            lanes = slice(128 * j, 128 * (j + 1))
            s_blocks = []
            for (k_blk, _), mask in zip(kv_blocks, masks):
                s = _dot_nt(q2[j], k_blk[:, lanes])
                s_blocks.append(s if mask is None else jnp.where(mask, s, -jnp.inf))
            out.append(_online(*carry[j], s_blocks, [v_blk[:, lanes] for _, v_blk in kv_blocks]))
        return tuple(out)

    if sink_ref is not None:
        upper = lax.broadcasted_iota(jnp.int32, (rows, 128), 1) >= HEAD_DIM
        carry = tuple((jnp.max(sink_ref[j], axis=-1, keepdims=True) * LOG2E, jnp.where(upper, 1.0, 0.0))
                      for j in range(2))
    else:
        carry = tuple((jnp.full((rows, 1), -jnp.inf, F32), jnp.zeros((rows, 128), F32)) for _ in range(2))
    if window:
        span = tq + 2 * WINDOW
        start = pl.multiple_of(jnp.clip(qi * tq - WINDOW, 0, seq_len - span), 128)
        qpos = qi * tq + lax.broadcasted_iota(jnp.int32, (rows, span), 0) % tq
        kpos = start + lax.broadcasted_iota(jnp.int32, (rows, span), 1)
        blocks = [(k_ref[pl.ds(start, span), :], v_ref[pl.ds(start, span), :])]
        masks = [jnp.abs(qpos - kpos) <= WINDOW]
        if kc_ref is not None:
            blocks.append((kc_ref[...], vc_ref[...]))
            masks.append(None)
        carry = update(carry, blocks, masks)
    else:
        for c in range(seq_len // kv_chunk):
            blk = slice(c * kv_chunk, (c + 1) * kv_chunk)
            carry = update(carry, [(k_ref[blk, :], v_ref[blk, :])], [None])
        if kc_ref is not None:
            carry = update(carry, [(kc_ref[...], vc_ref[...])], [None])

    low = lax.broadcasted_iota(jnp.int32, (tq, 128), 1) < HEAD_DIM
    for j in range(2):
        acc = carry[j][1]
        o = acc / pltpu.roll(acc, HEAD_DIM, 1)
        o_ref[:, 128 * j:128 * (j + 1)] = jnp.where(low, o[:tq], pltpu.roll(o[tq:], HEAD_DIM, 1)).astype(o_ref.dtype)


def _attn_kernel(*refs, seq_len, tq, kv_chunk, mixers):
    n_in = sum(3 + 2 * c + s for c, s, _ in mixers)
    outs = refs[n_in:]
    i = 0
    for o_ref, (has_cache, has_sink, window) in zip(outs, mixers):
        q_ref, k_ref, v_ref = refs[i:i + 3]
        i += 3
        kc_ref, vc_ref = refs[i:i + 2] if has_cache else (None, None)
        i += 2 * has_cache
        sink_ref = refs[i] if has_sink else None
        i += has_sink
        _attend(q_ref, k_ref, v_ref, kc_ref, vc_ref, sink_ref, o_ref, qi=pl.program_id(1),
                seq_len=seq_len, tq=tq, kv_chunk=kv_chunk, window=window)


def _attn_call(mixers, nb, seq_len):
    tq = min(ATTN_TILE, seq_len)
    nq = seq_len // tq
    kv_chunk = min(KV_CHUNK, seq_len)
    in_specs, args, static = [], [], []
    for mx in mixers:
        in_specs += [pl.BlockSpec((tq, 512), lambda b, i: (b * nq + i, 0)),
                     pl.BlockSpec((seq_len, 256), lambda b, i: (b, 0)),
                     pl.BlockSpec((seq_len, 256), lambda b, i: (b, 0))]
        args += [mx["q"], mx["k"], mx["v"]]
        if mx.get("cache") is not None:
            in_specs += [pl.BlockSpec((None, PAST_LEN, 256), lambda b, i: (b, 0, 0))] * 2
            args += list(mx["cache"])
        if mx.get("sink") is not None:
            in_specs.append(pl.BlockSpec((2, 2 * tq, 128), lambda b, i: (0, 0, 0)))
            args.append(mx["sink"])
        static.append((mx.get("cache") is not None, mx.get("sink") is not None, bool(mx.get("window"))))
    out = jax.ShapeDtypeStruct((nb * seq_len, 256), BF16)
    out_spec = pl.BlockSpec((tq, 256), lambda b, i: (b * nq + i, 0))
    return pl.pallas_call(
        functools.partial(_attn_kernel, seq_len=seq_len, tq=tq, kv_chunk=kv_chunk, mixers=tuple(static)),
        out_shape=[out] * len(mixers), grid=(nb, nq), in_specs=in_specs, out_specs=[out_spec] * len(mixers),
        compiler_params=_params(("arbitrary", "arbitrary")), name="attention",
    )(*args)


def _chunk_of(s, nc):
    return jnp.where(s < nc, nc - 1 - s, s - nc)


STEP_CHUNKS = 4


def _when(always, pred):
    if always:
        return lambda fn: fn()
    return pl.when(pred)


def _run_sweeps(single, s, ns, backward, forward):
    if single:
        backward()
        forward()
    else:
        pl.when(s < ns)(backward)
        pl.when(s >= ns)(forward)


def _chunks_per_step(nc):
    return STEP_CHUNKS if nc % STEP_CHUNKS == 0 else 1


def _log_gamma(rd):
    return jnp.log1p(-jnp.exp(rd))


def _ret_kernel(*refs, nc, sub, has_init, emit_state):
    q_ref, k_ref, v_ref, g_ref, rdl_ref, rdh_ref, gn_ref, seg_ref = refs[:8]
    i = 8
    if has_init:
        s0_ref = refs[i]
        i += 1
    y_ref = refs[i]
    i += 1
    if emit_state:
        st_ref = refs[i]
        i += 1
    m_sc, dec_sc, qk_sc, sf_sc, sb_sc, sball_sc = refs[i:i + 6]
    c_len = CHUNK
    b = pl.program_id(0)
    s = pl.program_id(1)

    @pl.when(jnp.logical_and(b == 0, s == 0))
    def _():
        ri = lax.broadcasted_iota(jnp.int32, (c_len, c_len), 0)
        ci = lax.broadcasted_iota(jnp.int32, (c_len, c_len), 1)
        d = (ri - ci).astype(F32)
        for h in range(N_HEADS):
            lgf = _log_gamma(jnp.broadcast_to(rdh_ref[0, h], (c_len, c_len)))
            lgb = _log_gamma(jnp.broadcast_to(rdh_ref[1, h], (c_len, c_len)))
            m_sc[h] = jnp.where(d > 0, jnp.exp(d * lgf), jnp.where(d < 0, jnp.exp(-d * lgb), 2.0))
        for dr in range(2):
            rows = [jnp.exp(c_len * _log_gamma(jnp.broadcast_to(rdh_ref[dr, h], (HEAD_DIM, c_len))))
                    for h in range(N_HEADS)]
            dec_sc[dr] = jnp.concatenate(rows, axis=0)
        a = lax.broadcasted_iota(jnp.int32, (c_len, 256), 0).astype(F32)
        lgf = _log_gamma(rdl_ref[0])
        lgb = _log_gamma(rdl_ref[1])
        qk_sc[0] = jnp.exp((a + 1.0) * lgf)
        qk_sc[1] = jnp.exp((c_len - 1.0 - a) * lgf)
        qk_sc[2] = jnp.exp((c_len - a) * lgb)
        qk_sc[3] = jnp.exp(a * lgb)

    @pl.when(s == 0)
    def _():
        sf_sc[...] = jnp.zeros_like(sf_sc)
        sb_sc[...] = jnp.zeros_like(sb_sc)
        if has_init:
            for h in range(N_HEADS):
                blk = slice(HEAD_DIM * h, HEAD_DIM * (h + 1))
                sf_sc[blk, blk] = s0_ref[0, h]
                sb_sc[blk, blk] = s0_ref[1, h]

    ri = lax.broadcasted_iota(jnp.int32, (256, 256), 0) // HEAD_DIM
    ci = lax.broadcasted_iota(jnp.int32, (256, 256), 1) // HEAD_DIM
    diag = ri == ci
    ns = nc // sub
    single = ns == 1
    sf = 0 if single else s - ns

    def backward():
        state = sb_sc[...]
        for u in reversed(range(sub)):
            rows = slice(u * c_len, (u + 1) * c_len)
            sball_sc[(ns - 1 - s) * sub + u] = state
            kd = (k_ref[rows, :].astype(F32) * qk_sc[3]).astype(BF16)
            state = state * dec_sc[1] + jnp.where(diag, _dot_tn(kd, v_ref[rows, :]), 0.0)
        sb_sc[...] = state
        if emit_state:
            @_when(single, s == ns - 1)
            def _():
                for h in range(N_HEADS):
                    blk = slice(HEAD_DIM * h, HEAD_DIM * (h + 1))
                    st_ref[1, h] = sb_sc[blk, blk]

    def forward():
        state = sf_sc[...]
        lane_head = lax.broadcasted_iota(jnp.int32, (c_len, 256), 1) // HEAD_DIM
        seg = seg_ref[...]
        for u in range(sub):
            rows = slice(u * c_len, (u + 1) * c_len)
            qb = q_ref[rows, :]
            kb = k_ref[rows, :]
            vb = v_ref[rows, :]
            q = qb.astype(F32)
            o = (_dot(qb, state.astype(BF16)) * qk_sc[0]
                 + _dot(qb, sball_sc[sf * sub + u].astype(BF16)) * qk_sc[2])
            for h in range(N_HEADS):
                mine = lane_head == h
                sc = _dot_nt(jnp.where(mine, q, 0.0).astype(BF16), kb)
                pv = _dot((sc * m_sc[h]).astype(BF16), vb)
                o = o + jnp.where(mine, pv, 0.0)
            mu = _seg_sum(o, seg) * (1.0 / HEAD_DIM)
            dlt = o - mu
            var = _seg_sum(dlt * dlt, seg) * (1.0 / HEAD_DIM)
            on = dlt * lax.rsqrt(var + 1e-5) * gn_ref[...]
            y_ref[rows, :] = (on * _silu(g_ref[rows, :])).astype(y_ref.dtype)
            kd = (kb.astype(F32) * qk_sc[1]).astype(BF16)
            state = state * dec_sc[0] + jnp.where(diag, _dot_tn(kd, vb), 0.0)
        sf_sc[...] = state
        if emit_state:
            @_when(single, sf == ns - 1)
            def _():
                for h in range(N_HEADS):
                    blk = slice(HEAD_DIM * h, HEAD_DIM * (h + 1))
                    st_ref[0, h] = sf_sc[blk, blk]

    return backward, forward


def _ret_specs(rqkv, pf, nb, seq_len, rd_lane, rd_head, gn_g, seg, layer, state0=None, emit_state=False):
    c_len = CHUNK
    nc = seq_len // c_len
    sub = _chunks_per_step(nc)
    ns = nc // sub
    rows = sub * c_len

    def col(start):
        return pl.BlockSpec((rows, 256), lambda b, s: (b * ns + _chunk_of(s, ns), start // 256))

    in_specs = [col(0), col(256), col(512), col(F_RG),
                pl.BlockSpec((2, 1, 256), lambda b, s: (0, 0, 0)),
                pl.BlockSpec((2, N_HEADS, 1, c_len), lambda b, s: (0, 0, 0, 0)),
                pl.BlockSpec((1, 256), lambda b, s: (0, 0)),
                pl.BlockSpec((256, 256), lambda b, s: (0, 0))]
    args = [rqkv, rqkv, rqkv, pf, rd_lane, rd_head, gn_g, seg]
    if state0 is not None:
        in_specs.append(pl.BlockSpec((None, None, 2, N_HEADS, HEAD_DIM, HEAD_DIM),
                                     lambda b, s: (b, layer, 0, 0, 0, 0)))
        args.append(state0)
    outs = [jax.ShapeDtypeStruct((nb * seq_len, 256), BF16)]
    out_specs = [pl.BlockSpec((rows, 256), lambda b, s: (b * ns + jnp.maximum(s - ns, 0), 0))]
    if emit_state:
        outs.append(jax.ShapeDtypeStruct((nb, DEPTH, 2, N_HEADS, HEAD_DIM, HEAD_DIM), F32))
        out_specs.append(pl.BlockSpec((None, None, 2, N_HEADS, HEAD_DIM, HEAD_DIM),
                                      lambda b, s: (b, layer, 0, 0, 0, 0)))
    scratch = [pltpu.VMEM((N_HEADS, c_len, c_len), F32), pltpu.VMEM((2, 256, 256), F32),
               pltpu.VMEM((4, c_len, 256), F32), pltpu.VMEM((256, 256), F32),
               pltpu.VMEM((256, 256), F32), pltpu.VMEM((nc, 256, 256), F32)]
    return in_specs, args, outs, out_specs, scratch


def _cumsum_rows(x):
    n = x.shape[0]
    row = lax.broadcasted_iota(jnp.int32, x.shape, 0)
    sh = 1
    while sh < n:
        x = x + jnp.where(row >= sh, pltpu.roll(x, sh, 0), 0.0)
        sh *= 2
    return x


def _ssd_kernel(*refs, nc, sub, has_init, emit_state):
    xbc_ref, prev_ref, next_ref, z_ref, dt_ref, cw_ref, cb_ref, dtb_ref, al_ref, dl_ref, ng_ref = refs[:11]
    i = 11
    if has_init:
        h0_ref = refs[i]
        i += 1
    y_ref = refs[i]
    i += 1
    if emit_state:
        st_ref = refs[i]
        i += 1
    hf_sc, hb_sc, hball_sc, xs_sc, bm_sc, cm_sc, dt_sc, cum_sc, rev_sc = refs[i:i + 9]
    c_len = CHUNK
    ns = nc // sub
    s = pl.program_id(1)
    single = ns == 1
    sf = 0 if single else s - ns

    @pl.when(s == 0)
    def _():
        if has_init:
            hf_sc[...] = h0_ref[0]
            hb_sc[...] = h0_ref[1]
        else:
            hf_sc[...] = jnp.zeros_like(hf_sc)
            hb_sc[...] = jnp.zeros_like(hb_sc)

    low = lax.broadcasted_iota(jnp.int32, (c_len, 128), 1) < HEAD_DIM
    low_rows = lax.broadcasted_iota(jnp.int32, (128, 128), 0) < HEAD_DIM

    def col(a, lane):
        return jnp.broadcast_to(a[:, lane:lane + 1], (c_len, c_len))

    def pair(a0, a1):
        return jnp.where(low, a0[:, :128], a1[:, :128])

    def state_update(h, g, xs, bm, dt, cols, tot_row, dt_lanes):
        w = [jnp.exp2(tot_row[r][:, :128] - cols[r][:, :128]) * col(dt, dt_lanes[r])[:, :128] for r in range(2)]
        xw = (xs[:, 128 * g:128 * (g + 1)] * jnp.where(low, w[0], w[1])).astype(BF16)
        dec = jnp.where(low_rows, jnp.broadcast_to(jnp.exp2(tot_row[0][:, :128]), (128, 128)),
                        jnp.broadcast_to(jnp.exp2(tot_row[1][:, :128]), (128, 128)))
        return h * dec + _dot_tn(xw, bm[:, 128 * g:128 * (g + 1)])

    def backward():
        cg = ns - 1 - s
        n_rows = sub * c_len
        xbc = xbc_ref[...]
        before = jnp.where(cg > 0, prev_ref[7:8, :], 0.0)
        after = jnp.where(cg < ns - 1, next_ref[0:1, :], 0.0)
        row = lax.broadcasted_iota(jnp.int32, xbc.shape, 0)
        x_prev = jnp.where(row == 0, before, pltpu.roll(xbc, 1, 0))
        x_next = jnp.where(row == n_rows - 1, after, pltpu.roll(xbc, n_rows - 1, 0))
        conv = _silu(cw_ref[0:1, :] * x_prev + cw_ref[1:2, :] * xbc + cw_ref[2:3, :] * x_next + cb_ref[...])
        pre = dt_ref[...] + dtb_ref[...]
        dt_all = jnp.maximum(pre, 0.0) + jnp.log1p(jnp.exp(-jnp.abs(pre)))
        dta_all = dt_all * (-jnp.exp(al_ref[...]) * LOG2E)
        state = [hb_sc[g] for g in range(2)]
        for u in reversed(range(sub)):
            rows = slice(u * c_len, (u + 1) * c_len)
            c = cg * sub + u
            xs = conv[rows, 0:256]
            bm = conv[rows, 256:512].astype(BF16)
            dt = dt_all[rows, :]
            dta = dta_all[rows, :]
            cum = _cumsum_rows(dta)
            rev = cum[c_len - 1:c_len, :] - cum + dta
            xs_sc[c] = xs
            bm_sc[c] = bm
            cm_sc[c] = conv[rows, 512:768].astype(BF16)
            dt_sc[c] = dt
            cum_sc[c] = cum
            rev_sc[c] = rev
            for g in range(2):
                hball_sc[c, g] = state[g]
                cols = [col(rev, 4 + 2 * g + r) for r in range(2)]
                tot = [cl[0:1, :] for cl in cols]
                state[g] = state_update(state[g], g, xs, bm, dt, cols, tot, [4 + 2 * g, 5 + 2 * g])
        for g in range(2):
            hb_sc[g] = state[g]
        if emit_state:
            @_when(single, s == ns - 1)
            def _():
                st_ref[1] = hb_sc[...]

    def forward():
        ri = lax.broadcasted_iota(jnp.int32, (c_len, c_len), 0)
        ci = lax.broadcasted_iota(jnp.int32, (c_len, c_len), 1)
        below = ri > ci
        on_diag = ri == ci
        lane_head = lax.broadcasted_iota(jnp.int32, (c_len, 256), 1) // HEAD_DIM
        state = [hf_sc[g] for g in range(2)]
        for u in range(sub):
            rows = slice(u * c_len, (u + 1) * c_len)
            c = sf * sub + u
            xs = xs_sc[c]
            bm = bm_sc[c]
            cm = cm_sc[c]
            dt = dt_sc[c]
            cum = cum_sc[c]
            rev = rev_sc[c]
            log_dt = jnp.log2(dt)
            off_f = (log_dt - cum).T
            off_b = (log_dt - rev).T
            dt_t = dt.T
            xs_b = xs.astype(BF16)
            y = jnp.zeros((c_len, 256), F32)
            inter = []
            for g in range(2):
                gl = slice(128 * g, 128 * (g + 1))
                gmat = _dot_nt(cm[:, gl], bm[:, gl])
                cols_f, cols_b = [], []
                for r in range(2):
                    h = 2 * g + r
                    cf = col(cum, h)
                    cb = col(rev, 4 + h)
                    cols_f.append(cf)
                    cols_b.append(cb)
                    w = jnp.exp2(jnp.where(below, cf + off_f[h:h + 1, :], cb + off_b[4 + h:5 + h, :]))
                    w = w + jnp.where(on_diag, dt_t[h:h + 1, :], 0.0)
                    yh = _dot((gmat * w).astype(BF16), xs_b)
                    y = y + jnp.where(lane_head == h, yh, 0.0)
                yf = _dot_nt(cm[:, gl], state[g].astype(BF16))
                yb = _dot_nt(cm[:, gl], hball_sc[c, g].astype(BF16))
                inter.append(yf * pair(jnp.exp2(cols_f[0]), jnp.exp2(cols_f[1]))
                             + yb * pair(jnp.exp2(cols_b[0]), jnp.exp2(cols_b[1])))
                tot = [cl[c_len - 1:c_len, :] for cl in cols_f]
                state[g] = state_update(state[g], g, xs, bm, dt, cols_f, tot, [2 * g, 2 * g + 1])
            y = y + jnp.concatenate(inter, axis=1) + xs * dl_ref[...]
            y = y * _silu(z_ref[rows, :])
            ms = jnp.mean(y * y, axis=-1, keepdims=True)
            y_ref[rows, :] = (y * lax.rsqrt(ms + 1e-6) * ng_ref[...]).astype(y_ref.dtype)
        for g in range(2):
            hf_sc[g] = state[g]
        if emit_state:
            @_when(single, sf == ns - 1)
            def _():
                st_ref[0] = hf_sc[...]

    return backward, forward


def _ssd_specs(p, nb, seq_len, conv_w, conv_b, dt_bias, a_log, d_lane, norm_g, layer,
              state0=None, emit_state=False):
    c_len = CHUNK
    nc = seq_len // c_len
    sub = _chunks_per_step(nc)
    ns = nc // sub
    rows = sub * c_len
    m = nb * seq_len
    r8 = rows // 8
    chunk = lambda b, s: b * ns + jnp.maximum(ns - 1 - s, 0)
    in_specs = [pl.BlockSpec((rows, 768), lambda b, s: (chunk(b, s), F_XBC // 768)),
                pl.BlockSpec((8, 768), lambda b, s: (jnp.maximum(chunk(b, s) * r8 - 1, 0), F_XBC // 768)),
                pl.BlockSpec((8, 768), lambda b, s: (jnp.minimum((chunk(b, s) + 1) * r8, m // 8 - 1),
                                                     F_XBC // 768)),
                pl.BlockSpec((rows, 256), lambda b, s: (b * ns + jnp.maximum(s - ns, 0), F_SZ // 256)),
                pl.BlockSpec((rows, 128), lambda b, s: (chunk(b, s), F_DT // 128)),
                pl.BlockSpec((None, 3, 768), lambda b, s: (layer, 0, 0)),
                pl.BlockSpec((1, 768), lambda b, s: (0, 0)),
                pl.BlockSpec((1, 128), lambda b, s: (0, 0)),
                pl.BlockSpec((1, 128), lambda b, s: (0, 0)),
                pl.BlockSpec((1, 256), lambda b, s: (0, 0)),
                pl.BlockSpec((1, 256), lambda b, s: (0, 0))]
    args = [p, p, p, p, p, conv_w, conv_b, dt_bias, a_log, d_lane, norm_g]
    if state0 is not None:
        in_specs.append(pl.BlockSpec((None, None, 2, 2, 128, SSD_STATE),
                                     lambda b, s: (b, layer, 0, 0, 0, 0)))
        args.append(state0)
    outs = [jax.ShapeDtypeStruct((m, 256), BF16)]
    out_specs = [pl.BlockSpec((rows, 256), lambda b, s: (b * ns + jnp.maximum(s - ns, 0), 0))]
    if emit_state:
        outs.append(jax.ShapeDtypeStruct((nb, DEPTH, 2, 2, 128, SSD_STATE), F32))
        out_specs.append(pl.BlockSpec((None, None, 2, 2, 128, SSD_STATE),
                                      lambda b, s: (b, layer, 0, 0, 0, 0)))
    scratch = [pltpu.VMEM((2, 128, SSD_STATE), F32), pltpu.VMEM((2, 128, SSD_STATE), F32),
               pltpu.VMEM((nc, 2, 128, SSD_STATE), F32),
               pltpu.VMEM((nc, c_len, 256), F32), pltpu.VMEM((nc, c_len, 256), BF16),
               pltpu.VMEM((nc, c_len, 256), BF16), pltpu.VMEM((nc, c_len, 128), F32),
               pltpu.VMEM((nc, c_len, 128), F32), pltpu.VMEM((nc, c_len, 128), F32)]
    return in_specs, args, outs, out_specs, scratch


def _scan_kernel(*refs, counts, nc, sub, **static):
    (ri, si), (ro, so), (rs, ss), n_alias = counts
    ins, rest = refs[:ri + si], refs[ri + si + n_alias:]
    outs, scr = rest[:ro + so], rest[ro + so:]
    ret_bwd, ret_fwd = _ret_kernel(*ins[:ri], *outs[:ro], *scr[:rs], nc=nc, sub=sub, **static)
    ssd_bwd, ssd_fwd = _ssd_kernel(*ins[ri:], *outs[ro:], *scr[rs:], nc=nc, sub=sub, **static)
    ns = nc // sub
    if ns == 1:
        ret_bwd()
        ret_fwd()
        ssd_bwd()
        ssd_fwd()
        return

    def backward():
        ret_bwd()
        ssd_bwd()

    def forward():
        ret_fwd()
        ssd_fwd()

    _run_sweeps(False, pl.program_id(1), ns, backward, forward)


def _scan_call(ret_parts, ssd_parts, nb, seq_len, has_init, emit_state, states_prev=None):
    nc = seq_len // CHUNK
    sub = _chunks_per_step(nc)
    in_specs = ret_parts[0] + ssd_parts[0]
    args = ret_parts[1] + ssd_parts[1]
    aliases = {}
    if states_prev is not None:
        aliases = {len(args): 1, len(args) + 1: len(ret_parts[2]) + 1}
        in_specs = in_specs + [pl.BlockSpec(memory_space=pl.ANY)] * 2
        args = args + list(states_prev)
    counts = tuple((len(a), len(b)) for a, b in zip((ret_parts[0], ret_parts[2], ret_parts[4]),
                                                    (ssd_parts[0], ssd_parts[2], ssd_parts[4])))
    return pl.pallas_call(
        functools.partial(_scan_kernel, counts=counts + (len(aliases),), nc=nc, sub=sub, has_init=has_init,
                          emit_state=emit_state),
        out_shape=ret_parts[2] + ssd_parts[2], grid=(nb, 1 if nc == sub else 2 * nc // sub),
        in_specs=in_specs, out_specs=ret_parts[3] + ssd_parts[3],
        scratch_shapes=ret_parts[4] + ssd_parts[4], input_output_aliases=aliases,
        compiler_params=_params(("arbitrary", "arbitrary")), name="scans",
    )(*args)


def _layer_norm(x, g, b):
    mu = jnp.mean(x, axis=-1, keepdims=True)
    d = x - mu
    var = jnp.mean(d * d, axis=-1, keepdims=True)
    return d * lax.rsqrt(var + 1e-5) * g + b


FFN_CHUNK = 256
FFN_ROWS = 512


def _outffn_kernel(x_ref, yr_ref, yg_ref, ys_ref, yw_ref, g1_ref, sh2_ref, sc2_ref, g2_ref,
                   wo_ref, wi_ref, wf_ref, l1g_ref, l1b_ref, l2g_ref, l2b_ref, o_ref):
    mix = (_dot(yr_ref[...], wo_ref[0:256, :]) + _dot(yg_ref[...], wo_ref[256:512, :])
           + _dot(ys_ref[...], wo_ref[512:768, :]) + _dot(yw_ref[...], wo_ref[768:1024, :]))
    x1 = _layer_norm(ALPHA * x_ref[...] + g1_ref[...] * mix, l1g_ref[...], l1b_ref[...])
    h2 = (x1 * (1.0 + sc2_ref[...]) + sh2_ref[...]).astype(BF16)
    acc = None
    for j in range(D_FF // FFN_CHUNK):
        cols = slice(FFN_CHUNK * j, FFN_CHUNK * (j + 1))
        gate = _dot(h2, wi_ref[:, cols])
        up = _dot(h2, wi_ref[:, D_FF + FFN_CHUNK * j:D_FF + FFN_CHUNK * (j + 1)])
        d = _dot((_silu(gate) * up).astype(BF16), wf_ref[cols, :])
        acc = d if acc is None else acc + d
    o_ref[...] = _layer_norm(ALPHA * x1 + g2_ref[...] * acc, l2g_ref[...], l2b_ref[...])


def _outffn_call(x, ys, mod5, w_out_b, w_ffn_in_b, w_ffn_out_b, ln, layer, row0, tiles_per_row):
    m = x.shape[0]
    tm = min(FFN_ROWS, m)
    tiles_per_row = max(tiles_per_row * TOKEN_TILE // tm, 1)
    row = lambda w: pl.BlockSpec((tm, w), lambda i: (i, 0))
    once = pl.Buffered(1)
    vec = pl.BlockSpec((None, 1, D_MODEL), lambda i: (layer, 0, 0))
    in_specs = [row(D_MODEL), row(256), row(256), row(256), row(256),
                _mod_spec(layer, row0, tiles_per_row, 2), _mod_spec(layer, row0, tiles_per_row, 3),
                _mod_spec(layer, row0, tiles_per_row, 4), _mod_spec(layer, row0, tiles_per_row, 5),
                pl.BlockSpec((None, D_MODEL, D_MODEL), lambda i: (layer, 0, 0), pipeline_mode=once),
                pl.BlockSpec((None, D_MODEL, 2 * D_FF), lambda i: (layer, 0, 0), pipeline_mode=once),
                pl.BlockSpec((None, D_FF, D_MODEL), lambda i: (layer, 0, 0), pipeline_mode=once),
                vec, vec, vec, vec]
    return pl.pallas_call(
        _outffn_kernel,
        out_shape=jax.ShapeDtypeStruct((m, D_MODEL), F32),
        grid=(m // tm,), in_specs=in_specs, out_specs=row(D_MODEL),
        compiler_params=_params(("arbitrary",)), name="out_ffn",
    )(x, *ys, mod5, mod5, mod5, mod5, w_out_b, w_ffn_in_b, w_ffn_out_b, *ln)


def _rope_tables(seq_len):
    t = jnp.arange(seq_len)
    rowp = (t // GRID_W).astype(F32)
    colp = (t % GRID_W).astype(F32)
    quarter = HEAD_DIM // 4
    inv_freq = ROPE_THETA ** (-jnp.arange(quarter, dtype=F32) / quarter)
    ang_r = rowp[:, None] * inv_freq[None, :]
    ang_c = colp[:, None] * inv_freq[None, :]
    cos = jnp.concatenate([jnp.cos(ang_r)] * 2 + [jnp.cos(ang_c)] * 2, axis=-1)
    sin = jnp.concatenate([-jnp.sin(ang_r), jnp.sin(ang_r), -jnp.sin(ang_c), jnp.sin(ang_c)], axis=-1)
    return jnp.tile(cos, (1, N_HEADS)), jnp.tile(sin, (1, N_HEADS))


def _expand_cache(c, fill):
    pad = jnp.full(c.shape[:2] + (HEAD_DIM,), fill, c.dtype)
    return jnp.concatenate([c[:, :, 0], pad, c[:, :, 1], pad], axis=-1).astype(BF16)


def _split_w_in(w):
    pad = jnp.zeros(w.shape[:-1] + (TAIL_W - T_DT - 8,), w.dtype)
    tail = jnp.concatenate([w[..., MAIN_W + 8:], w[..., MAIN_W:MAIN_W + 8], pad], axis=-1)
    return w.astype(BF16), tail.astype(BF16)


def kernel(x_prompt, x_sample, cache_gqa_k, cache_gqa_v, cache_swa_k, cache_swa_v, state_ret, state_ssd,
           c, c_ctx, ada_w, ada_b, w_in, ret_decay, ret_gn_g, gqa_q_norm, gqa_k_norm,
           ssd_conv_w, ssd_conv_b, ssd_dt_bias, ssd_a_log, ssd_d, ssd_norm_g, swa_sink, w_out,
           ln1_g, ln1_b, w_ffn_in, w_ffn_out, ln2_g, ln2_b):
    nb_c, len_c, _ = x_prompt.shape
    nb_l, len_l, _ = x_sample.shape
    depth = ada_w.shape[0]

    cond8 = jnp.zeros((8, D_MODEL), F32).at[0].set(c_ctx).at[1:1 + nb_l].set(c)
    mod5 = _ada_call(cond8, ada_w, ada_b)

    w_main, w_tail = _split_w_in(w_in)
    w_out_b = w_out.astype(BF16)
    w_ffn_in_b = w_ffn_in.astype(BF16)
    w_ffn_out_b = w_ffn_out.astype(BF16)
    lane_head = jnp.arange(256) // HEAD_DIM
    seg = (lane_head[:, None] == lane_head[None, :]).astype(BF16)
    rope_tabs = _rope_tables(len_l)
    caches_g = (_expand_cache_layers(cache_gqa_k, 0.0), _expand_cache_layers(cache_gqa_v, 1.0))
    caches_w = (_expand_cache_layers(cache_swa_k, 0.0), _expand_cache_layers(cache_swa_v, 1.0))
    state_ssd_g = state_ssd.reshape(nb_l, depth, 2, 2, 128, SSD_STATE)

    def layer_consts(l):
        rd = ret_decay[l]
        return dict(
            rd_lane=jnp.repeat(rd, HEAD_DIM, axis=-1).reshape(2, 1, 256),
            rd_head=jnp.broadcast_to(rd[:, :, None, None], (2, N_HEADS, 1, CHUNK)),
            gn_g=ret_gn_g[l].reshape(1, 256),
            qn=jnp.tile(gqa_q_norm[l], N_HEADS).reshape(1, 256),
            kn=jnp.tile(gqa_k_norm[l], N_HEADS).reshape(1, 256),
            conv_b=ssd_conv_b[l].reshape(1, 768),
            dt_bias=jnp.zeros((1, 128), F32).at[0, :8].set(ssd_dt_bias[l].reshape(8)),
            a_log=jnp.zeros((1, 128), F32).at[0, :8].set(ssd_a_log[l].reshape(8)),
            d_lane=jnp.repeat(ssd_d[l], HEAD_DIM).reshape(1, 256),
            norm_g=ssd_norm_g[l].reshape(1, 256),
            sink=lambda tq: jnp.broadcast_to(swa_sink[l].reshape(2, 2, 1, 1), (2, 2, tq, 128)
                                             ).reshape(2, 2 * tq, 128),
        )

    ln = tuple(a.reshape(depth, 1, D_MODEL) for a in (ln1_g, ln1_b, ln2_g, ln2_b))

    def run_group(x, nb, seq_len, row0, latent):
        m = nb * seq_len
        tpr = (seq_len // TOKEN_TILE) if latent else (m // TOKEN_TILE)
        x = x.reshape(m, D_MODEL)
        new_kv = states = None
        if not latent:
            new_kv = tuple(jnp.zeros((nb, depth, seq_len, 128), F32) for _ in range(4))
            states = (jnp.zeros((nb, depth, 2, N_HEADS, HEAD_DIM, HEAD_DIM), F32),
                      jnp.zeros((nb, depth, 2, 2, 128, SSD_STATE), F32))
        for l in range(depth):
            k = layer_consts(l)
            proj = _inproj_call(x, mod5, w_main, w_tail, l, row0, tpr, k["qn"], k["kn"], seg,
                                rope_tabs if latent else None, seq_len, new_kv_prev=new_kv)
            pf, rqkv, qg, kg, vg, qw, kw, vw = proj[:8]
            ret_parts = _ret_specs(rqkv, pf, nb, seq_len, k["rd_lane"], k["rd_head"], k["gn_g"], seg, l,
                                   state0=state_ret if latent else None, emit_state=not latent)
            ssd_parts = _ssd_specs(pf, nb, seq_len, ssd_conv_w, k["conv_b"], k["dt_bias"], k["a_log"],
                                   k["d_lane"], k["norm_g"], l,
                                   state0=state_ssd_g if latent else None, emit_state=not latent)
            scans = _scan_call(ret_parts, ssd_parts, nb, seq_len, has_init=latent, emit_state=not latent,
                               states_prev=states)
            if latent:
                y_ret, y_ssd = scans
                y_gqa, y_swa = _attn_call(
                    [dict(q=qg, k=kg, v=vg, cache=(caches_g[0][l], caches_g[1][l])),
                     dict(q=qw, k=kw, v=vw, cache=(caches_w[0][l], caches_w[1][l]),
                          sink=k["sink"](min(ATTN_TILE, seq_len)), window=True)], nb, seq_len)
            else:
                y_ret, st_ret, y_ssd, st_ssd = scans
                states = (st_ret, st_ssd)
                new_kv = proj[8:12]
                y_gqa, = _attn_call([dict(q=qg, k=kg, v=vg)], nb, seq_len)
                y_swa, = _attn_call([dict(q=qw, k=kw, v=vw, sink=k["sink"](min(ATTN_TILE, seq_len)))],
                                    nb, seq_len)
            x = _outffn_call(x, (y_ret, y_gqa, y_ssd, y_swa), mod5, w_out_b, w_ffn_in_b, w_ffn_out_b,
                             ln, l, row0, tpr)
        extras = None
        if not latent:
            extras = tuple(a.reshape(nb, depth, seq_len, 2, HEAD_DIM) for a in new_kv) + (
                states[0], states[1].reshape(nb, depth, 2, N_HEADS, HEAD_DIM, SSD_STATE))
        return x.reshape(nb, seq_len, D_MODEL), extras

    y_prompt, extras = run_group(x_prompt, nb_c, len_c, 0, False)
    y_sample, _ = run_group(x_sample, nb_l, len_l, 1, True)
    return (y_prompt, y_sample) + extras


def _expand_cache_layers(cache, fill):
    return [_expand_cache(cache[:, l], fill) for l in range(cache.shape[1])]
```
